```python
import math
import jax, jax.numpy as jnp
from jax import lax
import numpy as np

D_MODEL = 1024
BATCH = 16
SEQ = 256
DEPTH = 4
DEC_BATCH = 4
DEC_SEQ = 1024
PAST_LEN = 512

GRID_W = 64
EPS = 1e-6
N_SSM_LAYERS = (DEPTH + 1) // 2
N_ATTN_LAYERS = DEPTH // 2
SSD_HEADDIM = 64
SSD_INNER = D_MODEL
SSD_HEADS = SSD_INNER // SSD_HEADDIM
SSD_GROUPS = 2
SSD_STATE = 128
SSD_CONV = 4
SSD_CHUNK = 128
SSD_CONV_DIM = SSD_INNER + 2 * SSD_GROUPS * SSD_STATE
LRU_WIDTH = D_MODEL
LRU_BW = 64
LRU_BLOCKS = LRU_WIDTH // LRU_BW
LRU_CONV = 4
LRU_C = 8.0
SSM_IN = SSD_INNER + SSD_CONV_DIM + 2 * SSD_HEADS + 2 * LRU_WIDTH
CONV_WIDTH = D_MODEL
CONF_K = 31
HEAD_DIM = 128
N_HEADS = D_MODEL // HEAD_DIM
N_KV_HEADS = 2
Q_BLOCK = 128
ROPE_THETA = 10000.0
ROPE_AXIS_DIM = HEAD_DIM // 2
CA_IN = 2 * CONV_WIDTH + (N_HEADS + 2 * N_KV_HEADS) * HEAD_DIM
D_FF = 2816
FFN_CONV = 3

kernel_name = 'hybrid_prefix_diffusion_step'


def rmsnorm(x, g):
    xf = x.astype(jnp.float32)
    y = xf * lax.rsqrt(jnp.mean(xf * xf, axis=-1, keepdims=True) + EPS)
    return (y * g.astype(jnp.float32)).astype(x.dtype)


def layernorm(x, g, b):
    xf = x.astype(jnp.float32)
    mu = jnp.mean(xf, axis=-1, keepdims=True)
    var = jnp.mean(jnp.square(xf - mu), axis=-1, keepdims=True)
    y = (xf - mu) * lax.rsqrt(var + EPS)
    return (y * g.astype(jnp.float32) + b.astype(jnp.float32)).astype(x.dtype)


def dwconv(x, w, b):
    k = w.shape[0]
    right = (k - 1) // 2
    y = lax.conv_general_dilated(x, w[:, None, :].astype(x.dtype), window_strides=(1,),
                                 padding=[(k - 1 - right, right)],
                                 dimension_numbers=('NWC', 'WIO', 'NWC'),
                                 feature_group_count=x.shape[-1])
    return y + b.astype(x.dtype)


def modulation(cvec, w, b):
    m = jax.nn.silu(cvec) @ w + b
    return jnp.split(m[:, None, :], 6, axis=-1)


def modulate(x, g, shift, scale):
    return rmsnorm(x, g) * (1 + scale) + shift


def gated_residual(x, o, g, gate):
    return x + gate * rmsnorm(o, g)


def flip(x):
    return jnp.flip(x, axis=1)


def segsum(a):
    t = a.shape[-1]
    ae = jnp.broadcast_to(a[..., None], a.shape + (t,))
    ae = jnp.where(jnp.tril(jnp.ones((t, t), bool), -1), ae, 0.0)
    cs = jnp.cumsum(ae, axis=-2)
    return jnp.where(jnp.tril(jnp.ones((t, t), bool)), cs, -jnp.inf)


def ssd_scan(x, dt, a_coef, bm, cm, h0):
    b, L, H, P = x.shape
    G, N = bm.shape[2], bm.shape[3]
    R = H // G
    T = SSD_CHUNK
    nc = L // T
    dtype = x.dtype
    xd = (x * dt[..., None].astype(dtype)).reshape(b, nc, T, G, R, P)
    bc = bm.reshape(b, nc, T, G, N)
    cc = cm.reshape(b, nc, T, G, N)
    a = (dt * a_coef).reshape(b, nc, T, G, R).transpose(0, 3, 4, 1, 2)
    a_cum = jnp.cumsum(a, axis=-1)
    l_mat = jnp.exp(segsum(a)).astype(dtype)
    y_diag = jnp.einsum('bclgn,bcsgn,bgrcls,bcsgrp->bclgrp', cc, bc, l_mat, xd)
    decay_in = jnp.exp(a_cum[..., -1:] - a_cum).astype(dtype)
    states = jnp.einsum('bclgn,bgrcl,bclgrp->bcgrpn', bc, decay_in, xd)
    states = jnp.concatenate([h0.reshape(b, 1, G, R, P, N).astype(states.dtype), states], axis=1)
    chunk_decay = jnp.exp(segsum(jnp.pad(a_cum[..., -1], ((0, 0), (0, 0), (0, 0), (1, 0))))).astype(dtype)
    states = jnp.einsum('bgrzc,bcgrpn->bzgrpn', chunk_decay, states)
    decay_out = jnp.exp(a_cum).astype(dtype)
    y_off = jnp.einsum('bclgn,bcgrpn,bgrcl->bclgrp', cc, states[:, :-1], decay_out)
    return (y_diag + y_off).reshape(b, L, H, P), states[:, -1].reshape(b, H, P, N)


def scan_combine(left, right):
    a_l, b_l = left
    a_r, b_r = right
    return a_l * a_r, a_r * b_l + b_r


def rglru(x, wa, ba, wx, bx, lam, h0):
    b, L, W = x.shape
    xb = x.reshape(b, L, LRU_BLOCKS, LRU_BW)
    r = jax.nn.sigmoid((jnp.einsum('blnd,nde->blne', xb, wa).reshape(b, L, W) + ba).astype(jnp.float32))
    i = jax.nn.sigmoid((jnp.einsum('blnd,nde->blne', xb, wx).reshape(b, L, W) + bx).astype(jnp.float32))
    log_a = -LRU_C * r * jax.nn.softplus(-lam.astype(jnp.float32))
    a = jnp.exp(log_a)
    u = jnp.sqrt(-jnp.expm1(2.0 * log_a)) * i * x.astype(jnp.float32)
    u = u.at[:, 0].add(a[:, 0] * h0.astype(jnp.float32))
    _, h = lax.associative_scan(scan_combine, (a, u), axis=1)
    h = h.astype(x.dtype)
    return h, h[:, -1]


def ssm_mixer(h, w_in, conv_w, conv_b, a_log, dt_bias, d_skip, norm_w, lconv_w, lconv_b,
              wa, ba, wx, bx, lam, w_out, ssd_h0, lru_h0):
    b, L, _ = h.shape
    o1 = SSD_INNER
    o2 = o1 + SSD_CONV_DIM
    o3 = o2 + 2 * SSD_HEADS
    o4 = o3 + LRU_WIDTH
    z, xbc, dt_raw, xl, gl = jnp.split(h @ w_in, [o1, o2, o3, o4], axis=-1)
    xbc = jax.nn.silu(dwconv(xbc, conv_w, conv_b))
    xs, bm, cm = jnp.split(xbc, [SSD_INNER, SSD_INNER + SSD_GROUPS * SSD_STATE], axis=-1)
    xs = xs.reshape(b, L, SSD_HEADS, SSD_HEADDIM)
    bm = bm.reshape(b, L, SSD_GROUPS, SSD_STATE)
    cm = cm.reshape(b, L, SSD_GROUPS, SSD_STATE)
    dt = jax.nn.softplus(dt_raw.reshape(b, L, 2, SSD_HEADS).astype(jnp.float32) + dt_bias.astype(jnp.float32))
    a_coef = -jnp.exp(a_log.astype(jnp.float32))
    y_f, s_f = ssd_scan(xs, dt[:, :, 0], a_coef[0], bm, cm, ssd_h0[:, 0])
    y_b, s_b = ssd_scan(flip(xs), flip(dt[:, :, 1]), a_coef[1], flip(bm), flip(cm), ssd_h0[:, 1])
    y = y_f + flip(y_b) + xs * d_skip[:, None].astype(xs.dtype)
    y = rmsnorm(y.reshape(b, L, SSD_INNER) * jax.nn.silu(z), norm_w)
    xl = dwconv(xl, lconv_w, lconv_b)
    hl_f, l_f = rglru(xl, wa[0], ba[0], wx[0], bx[0], lam[0], lru_h0[:, 0])
    hl_b, l_b = rglru(flip(xl), wa[1], ba[1], wx[1], bx[1], lam[1], lru_h0[:, 1])
    yl = jax.nn.gelu(gl) * (hl_f + flip(hl_b))
    out = jnp.concatenate([y, yl], axis=-1) @ w_out
    return out, jnp.stack([s_f, s_b], axis=1), jnp.stack([l_f, l_b], axis=1)


def axial_rope_tables(rows):
    row_pos = jnp.repeat(jnp.arange(rows, dtype=jnp.float32), GRID_W)
    col_pos = jnp.tile(jnp.arange(GRID_W, dtype=jnp.float32), rows)
    inv_freq = jnp.power(ROPE_THETA, -jnp.arange(0, ROPE_AXIS_DIM, 2, dtype=jnp.float32) / ROPE_AXIS_DIM)
    ang_r = row_pos[:, None] * inv_freq
    ang_c = col_pos[:, None] * inv_freq
    return (jnp.cos(ang_r), jnp.sin(ang_r), jnp.cos(ang_c), jnp.sin(ang_c))


def rope_half(x, cos, sin):
    x1, x2 = jnp.split(x, 2, axis=-1)
    cos = cos[None, :, None, :].astype(x.dtype)
    sin = sin[None, :, None, :].astype(x.dtype)
    return jnp.concatenate([x1 * cos - x2 * sin, x1 * sin + x2 * cos], axis=-1)


def apply_axial_rope(x, rope):
    cos_r, sin_r, cos_c, sin_c = rope
    xr, xc = jnp.split(x, 2, axis=-1)
    return jnp.concatenate([rope_half(xr, cos_r, sin_r), rope_half(xc, cos_c, sin_c)], axis=-1)


def block_attention(q, k, v):
    b, L, H, d = q.shape
    g = k.shape[2]
    r = H // g
    qb = q.reshape(b, L // Q_BLOCK, Q_BLOCK, g, r, d).transpose(1, 0, 2, 3, 4, 5)
    scale = HEAD_DIM ** -0.5

    def one_block(qblk):
        s = jnp.einsum('bqgrd,bkgd->bgrqk', qblk, k).astype(jnp.float32) * scale
        p = jax.nn.softmax(s, axis=-1).astype(v.dtype)
        return jnp.einsum('bgrqk,bkgd->bqgrd', p, v)

    o = lax.map(one_block, qb)
    return o.transpose(1, 0, 2, 3, 4, 5).reshape(b, L, H * d)


def conv_attn_mixer(h, w_in, dw_w, dw_b, ln_g, ln_b, qg, kg, w_out, ctx_k, ctx_v, rope):
    b, L, _ = h.shape
    q_w = N_HEADS * HEAD_DIM
    kv_w = N_KV_HEADS * HEAD_DIM
    glu_a, glu_g, q, k, v = jnp.split(h @ w_in, [CONV_WIDTH, 2 * CONV_WIDTH, 2 * CONV_WIDTH + q_w,
                                                  2 * CONV_WIDTH + q_w + kv_w], axis=-1)
    cv = dwconv(glu_a * jax.nn.sigmoid(glu_g), dw_w, dw_b)
    cv = jax.nn.silu(layernorm(cv, ln_g, ln_b))
    q = rmsnorm(q.reshape(b, L, N_HEADS, HEAD_DIM), qg)
    k = rmsnorm(k.reshape(b, L, N_KV_HEADS, HEAD_DIM), kg)
    v = v.reshape(b, L, N_KV_HEADS, HEAD_DIM)
    if rope is None:
        o = block_attention(q, k, v)
    else:
        k_all = jnp.concatenate([ctx_k.astype(k.dtype), apply_axial_rope(k, rope)], axis=1)
        v_all = jnp.concatenate([ctx_v.astype(v.dtype), v], axis=1)
        o = block_attention(apply_axial_rope(q, rope), k_all, v_all)
    out = jnp.concatenate([cv, o], axis=-1) @ w_out
    return out, k, v


def conv_ffn(h, w_in, conv_w, conv_b, w_out):
    gate, val = jnp.split(h @ w_in, 2, axis=-1)
    return (jax.nn.silu(dwconv(gate, conv_w, conv_b)) * val) @ w_out


def setup_inputs(seed: int = 0) -> dict:
    key = jax.random.key(seed)
    ks = iter(jax.random.split(key, 48))
    f32 = jnp.float32

    def nrm(shape, scale):
        return scale * jax.random.normal(next(ks), shape, f32)

    def gain(shape):
        return 1.0 + nrm(shape, 0.02)

    def unif(shape, lo, hi):
        return jax.random.uniform(next(ks), shape, f32, lo, hi)

    dt0 = jnp.exp(unif((N_SSM_LAYERS, 2, SSD_HEADS), math.log(1e-3), math.log(1e-1)))
    a0 = unif((N_SSM_LAYERS, 2, LRU_WIDTH), 0.9, 0.999)
    return {
        'x_prompt': nrm((BATCH, SEQ, D_MODEL), 1.0),
        'x_sample': nrm((DEC_BATCH, DEC_SEQ, D_MODEL), 1.0),
        'state_ssd': nrm((DEC_BATCH, N_SSM_LAYERS, 2, SSD_HEADS, SSD_HEADDIM, SSD_STATE), 0.1),
        'state_lru': nrm((DEC_BATCH, N_SSM_LAYERS, 2, LRU_WIDTH), 0.5),
        'cache_k': nrm((DEC_BATCH, N_ATTN_LAYERS, PAST_LEN, N_KV_HEADS, HEAD_DIM), 1.0),
        'cache_v': nrm((DEC_BATCH, N_ATTN_LAYERS, PAST_LEN, N_KV_HEADS, HEAD_DIM), 1.0),
        'c': nrm((DEC_BATCH, D_MODEL), 1.0),
        'c_ctx': nrm((D_MODEL,), 1.0),
        'w_mod': nrm((DEPTH, D_MODEL, 6 * D_MODEL), 0.5 * D_MODEL ** -0.5),
        'b_mod': nrm((DEPTH, 6 * D_MODEL), 0.01),
        'norm_g': gain((DEPTH, 4, D_MODEL)),
        'w_in_ssm': nrm((N_SSM_LAYERS, D_MODEL, SSM_IN), D_MODEL ** -0.5),
        'ssd_conv_w': nrm((N_SSM_LAYERS, SSD_CONV, SSD_CONV_DIM), SSD_CONV ** -0.5),
        'ssd_conv_b': nrm((N_SSM_LAYERS, SSD_CONV_DIM), 0.02),
        'ssd_a_log': jnp.log(unif((N_SSM_LAYERS, 2, SSD_HEADS), 1.0, 16.0)),
        'ssd_dt_bias': dt0 + jnp.log(-jnp.expm1(-dt0)),
        'ssd_d': gain((N_SSM_LAYERS, SSD_HEADS)),
        'ssd_norm_w': gain((N_SSM_LAYERS, SSD_INNER)),
        'lru_conv_w': nrm((N_SSM_LAYERS, LRU_CONV, LRU_WIDTH), LRU_CONV ** -0.5),
        'lru_conv_b': nrm((N_SSM_LAYERS, LRU_WIDTH), 0.02),
        'lru_wa': nrm((N_SSM_LAYERS, 2, LRU_BLOCKS, LRU_BW, LRU_BW), LRU_BW ** -0.5),
        'lru_ba': nrm((N_SSM_LAYERS, 2, LRU_WIDTH), 0.02),
        'lru_wx': nrm((N_SSM_LAYERS, 2, LRU_BLOCKS, LRU_BW, LRU_BW), LRU_BW ** -0.5),
        'lru_bx': nrm((N_SSM_LAYERS, 2, LRU_WIDTH), 0.02),
        'lru_lambda': jnp.log(a0) - jnp.log1p(-a0),
        'w_out_ssm': nrm((N_SSM_LAYERS, SSD_INNER + LRU_WIDTH, D_MODEL), (SSD_INNER + LRU_WIDTH) ** -0.5),
        'w_in_ca': nrm((N_ATTN_LAYERS, D_MODEL, CA_IN), D_MODEL ** -0.5),
        'conf_dw_w': nrm((N_ATTN_LAYERS, CONF_K, CONV_WIDTH), CONF_K ** -0.5),
        'conf_dw_b': nrm((N_ATTN_LAYERS, CONV_WIDTH), 0.02),
        'conf_ln_g': gain((N_ATTN_LAYERS, CONV_WIDTH)),
        'conf_ln_b': nrm((N_ATTN_LAYERS, CONV_WIDTH), 0.02),
        'q_norm_g': gain((N_ATTN_LAYERS, HEAD_DIM)),
        'k_norm_g': gain((N_ATTN_LAYERS, HEAD_DIM)),
        'w_out_ca': nrm((N_ATTN_LAYERS, CONV_WIDTH + N_HEADS * HEAD_DIM, D_MODEL), (CONV_WIDTH + N_HEADS * HEAD_DIM) ** -0.5),
        'ffn_w_in': nrm((DEPTH, D_MODEL, 2 * D_FF), D_MODEL ** -0.5),
        'ffn_conv_w': nrm((DEPTH, FFN_CONV, D_FF), FFN_CONV ** -0.5),
        'ffn_conv_b': nrm((DEPTH, D_FF), 0.02),
        'ffn_w_out': nrm((DEPTH, D_FF, D_MODEL), D_FF ** -0.5),
    }


def reference(x_prompt, x_sample, state_ssd, state_lru, cache_k, cache_v, c, c_ctx,
              w_mod, b_mod, norm_g, w_in_ssm, ssd_conv_w, ssd_conv_b, ssd_a_log, ssd_dt_bias,
              ssd_d, ssd_norm_w, lru_conv_w, lru_conv_b, lru_wa, lru_ba, lru_wx, lru_bx,
              lru_lambda, w_out_ssm, w_in_ca, conf_dw_w, conf_dw_b, conf_ln_g, conf_ln_b,
              q_norm_g, k_norm_g, w_out_ca, ffn_w_in, ffn_conv_w, ffn_conv_b, ffn_w_out):
    rows = x_sample.shape[1] // GRID_W
    rope = axial_rope_tables(rows)
    b_p = x_prompt.shape[0]
    ssd_zero = jnp.zeros((b_p, 2, SSD_HEADS, SSD_HEADDIM, SSD_STATE), x_prompt.dtype)
    lru_zero = jnp.zeros((b_p, 2, LRU_WIDTH), x_prompt.dtype)
    xp, xs = x_prompt, x_sample
    ssd_out, lru_out, k_out, v_out = [], [], [], []
    for layer in range(DEPTH):
        i = layer // 2
        mp = modulation(c_ctx[None], w_mod[layer], b_mod[layer])
        ms = modulation(c, w_mod[layer], b_mod[layer])
        hp = modulate(xp, norm_g[layer, 0], mp[0], mp[1])
        hs = modulate(xs, norm_g[layer, 0], ms[0], ms[1])
        if layer % 2 == 0:
            ssm_w = (w_in_ssm[i], ssd_conv_w[i], ssd_conv_b[i], ssd_a_log[i], ssd_dt_bias[i], ssd_d[i],
                     ssd_norm_w[i], lru_conv_w[i], lru_conv_b[i], lru_wa[i], lru_ba[i], lru_wx[i],
                     lru_bx[i], lru_lambda[i], w_out_ssm[i])
            op, s_p, l_p = ssm_mixer(hp, *ssm_w, ssd_zero, lru_zero)
            os_, _, _ = ssm_mixer(hs, *ssm_w, state_ssd[:, i], state_lru[:, i])
            ssd_out.append(s_p)
            lru_out.append(l_p)
        else:
            ca_w = (w_in_ca[i], conf_dw_w[i], conf_dw_b[i], conf_ln_g[i], conf_ln_b[i],
                    q_norm_g[i], k_norm_g[i], w_out_ca[i])
            op, k_p, v_p = conv_attn_mixer(hp, *ca_w, None, None, None)
            os_, _, _ = conv_attn_mixer(hs, *ca_w, cache_k[:, i], cache_v[:, i], rope)
            k_out.append(k_p)
            v_out.append(v_p)
        xp = gated_residual(xp, op, norm_g[layer, 1], mp[2])
        xs = gated_residual(xs, os_, norm_g[layer, 1], ms[2])
        ffn_w = (ffn_w_in[layer], ffn_conv_w[layer], ffn_conv_b[layer], ffn_w_out[layer])
        xp = gated_residual(xp, conv_ffn(modulate(xp, norm_g[layer, 2], mp[3], mp[4]), *ffn_w), norm_g[layer, 3], mp[5])
        xs = gated_residual(xs, conv_ffn(modulate(xs, norm_g[layer, 2], ms[3], ms[4]), *ffn_w), norm_g[layer, 3], ms[5])
    new_state_ssd = jnp.stack(ssd_out, axis=1)
    new_state_lru = jnp.stack(lru_out, axis=1)
    new_cache_k = jnp.stack(k_out, axis=1)
    new_cache_v = jnp.stack(v_out, axis=1)
    return (xp, xs, new_state_ssd, new_state_lru, new_cache_k, new_cache_v)
```

```python
import functools
import math

import jax
import jax.numpy as jnp
from jax import lax
from jax.experimental import pallas as pl
from jax.experimental.pallas import tpu as pltpu

F32 = jnp.float32
BF16 = jnp.bfloat16

D_MODEL = 1024
BATCH = 16
SEQ = 256
DEPTH = 4
DEC_BATCH = 4
DEC_SEQ = 1024
PAST_LEN = 512
GRID_W = 64
EPS = 1e-6
SSD_HEADDIM = 64
SSD_INNER = D_MODEL
SSD_HEADS = SSD_INNER // SSD_HEADDIM
SSD_GROUPS = 2
SSD_STATE = 128
SSD_CONV = 4
SSD_CHUNK = 128
SSD_CONV_DIM = SSD_INNER + 2 * SSD_GROUPS * SSD_STATE
LRU_WIDTH = D_MODEL
LRU_BW = 64
LRU_BLOCKS = LRU_WIDTH // LRU_BW
LRU_CONV = 4
LRU_C = 8.0
CONV_WIDTH = D_MODEL
CONF_K = 31
HEAD_DIM = 128
N_HEADS = D_MODEL // HEAD_DIM
N_KV_HEADS = 2
ROPE_THETA = 10000.0
ROPE_AXIS_DIM = HEAD_DIM // 2
D_FF = 2816
FFN_CONV = 3

N_PROMPT_TOK = BATCH * SEQ
N_SAMPLE_TOK = DEC_BATCH * DEC_SEQ
N_TOK = N_PROMPT_TOK + N_SAMPLE_TOK
N_MOD_ROWS = 8
LANES = 128
SUBLANES = 8
MXU_DIM = 256
DT_PAD = LANES
VMEM_LIMIT = 58 * 1024 * 1024

TM_LINEAR = 512
TM_FFN = 1024


def _cparams(n_axes):
    return pltpu.CompilerParams(
        dimension_semantics=("arbitrary",) * n_axes,
        vmem_limit_bytes=VMEM_LIMIT)


def _const_spec(shape):
    nd = len(shape)
    return pl.BlockSpec(shape, lambda *_: (0,) * nd, pipeline_mode=pl.Buffered(1))


def _mod_row(i, tm):
    start = i * tm
    return jnp.where(start < N_PROMPT_TOK, 0, 1 + (start - N_PROMPT_TOK) // DEC_SEQ)


def _mod_spec(tm):
    return pl.BlockSpec((None, 1, D_MODEL), lambda i: (_mod_row(i, tm), 0, 0))


def _sigmoid(x):
    return jax.nn.sigmoid(x)


def _silu(x):
    return x * _sigmoid(x)


def _softplus(x):
    return jnp.maximum(x, 0.0) + jnp.log1p(jnp.exp(-jnp.abs(x)))


def _gelu_tanh(x):
    return 0.5 * x * (1.0 + jnp.tanh(math.sqrt(2.0 / math.pi) * (x + 0.044715 * (x * x * x))))


def _rms(x, g):
    ms = jnp.mean(x * x, axis=-1, keepdims=True)
    return (x * lax.rsqrt(ms + EPS)) * g


def _dot(a, b):
    return jnp.dot(a, b, preferred_element_type=F32)


def _dot_nt(a, b):
    return lax.dot_general(a, b, (((1,), (1,)), ((), ())), preferred_element_type=F32)


def _mod_kernel(c_ref, w_ref, b_ref, o_ref):
    c = c_ref[...]
    s = _silu(c).astype(BF16)
    o_ref[...] = _dot(s, w_ref[...].astype(BF16)) + b_ref[...]


def _modulation_all(cvec, w_mod, b_mod):
    tn = 1536
    n_out = 6 * D_MODEL
    out = pl.pallas_call(
        _mod_kernel,
        grid=(DEPTH, n_out // tn),
        in_specs=[
            pl.BlockSpec((N_MOD_ROWS, D_MODEL), lambda l, j: (0, 0)),
            pl.BlockSpec((None, D_MODEL, tn), lambda l, j: (l, 0, j)),
            pl.BlockSpec((None, 1, tn), lambda l, j: (l, 0, j)),
        ],
        out_specs=pl.BlockSpec((None, N_MOD_ROWS, tn), lambda l, j: (l, 0, j)),
        out_shape=jax.ShapeDtypeStruct((DEPTH, N_MOD_ROWS, n_out), F32),
        compiler_params=_cparams(2),
        name="modulation",
    )(cvec, w_mod, b_mod.reshape(DEPTH, 1, n_out))
    out = out.reshape(DEPTH, N_MOD_ROWS, 6, 1, D_MODEL)
    return jnp.transpose(out, (0, 2, 1, 3, 4))


def _inproj_kernel(x_ref, g_ref, shift_ref, scale_ref, *rest, n_out, chunk):
    w_refs = rest[:n_out]
    o_refs = rest[n_out:2 * n_out]
    h_ref = rest[2 * n_out]
    h = _rms(x_ref[...], g_ref[...]) * (1.0 + scale_ref[...]) + shift_ref[...]
    h_ref[...] = h.astype(BF16)
    for w_ref, o_ref in zip(w_refs, o_refs):
        n = w_ref.shape[1]
        for c0 in range(0, n, chunk):
            c1 = min(c0 + chunk, n)
            o_ref[:, c0:c1] = _dot(h_ref[...], w_ref[:, c0:c1])


def _inproj(x, g, shift, scale, weights, name):
    tm = TM_LINEAR
    n_out = len(weights)
    row_spec = pl.BlockSpec((tm, D_MODEL), lambda i: (i, 0))
    in_specs = [row_spec, _const_spec((1, D_MODEL)), _mod_spec(tm), _mod_spec(tm)]
    in_specs += [_const_spec(w.shape) for w in weights]
    out_specs = [pl.BlockSpec((tm, w.shape[1]), lambda i: (i, 0)) for w in weights]
    out_shape = [jax.ShapeDtypeStruct((N_TOK, w.shape[1]), F32) for w in weights]
    return pl.pallas_call(
        functools.partial(_inproj_kernel, n_out=n_out, chunk=512),
        grid=(N_TOK // tm,),
        in_specs=in_specs,
        out_specs=out_specs,
        out_shape=out_shape,
        scratch_shapes=[pltpu.VMEM((tm, D_MODEL), BF16)],
        compiler_params=_cparams(1),
        name=name,
    )(x, g, shift, scale, *weights)


def _outproj_kernel(a_ref, b_ref, wa_ref, wb_ref, x_ref, g_ref, gate_ref, o_ref):
    acc = _dot(a_ref[...], wa_ref[...]) + _dot(b_ref[...], wb_ref[...])
    o_ref[...] = x_ref[...] + gate_ref[...] * _rms(acc, g_ref[...])


def _outproj(a, b, wa, wb, x, g, gate, name):
    tm = TM_LINEAR
    return pl.pallas_call(
        _outproj_kernel,
        grid=(N_TOK // tm,),
        in_specs=[
            pl.BlockSpec((tm, a.shape[1]), lambda i: (i, 0)),
            pl.BlockSpec((tm, b.shape[1]), lambda i: (i, 0)),
            _const_spec(wa.shape),
            _const_spec(wb.shape),
            pl.BlockSpec((tm, D_MODEL), lambda i: (i, 0)),
            _const_spec((1, D_MODEL)),
            _mod_spec(tm),
        ],
        out_specs=pl.BlockSpec((tm, D_MODEL), lambda i: (i, 0)),
        out_shape=jax.ShapeDtypeStruct((N_TOK, D_MODEL), F32),
        compiler_params=_cparams(1),
        name=name,
    )(a, b, wa, wb, x, g, gate)


def _ffn_kernel(x_ref, g2_ref, shift_ref, scale_ref, wg_ref, wv_ref, cw_ref, cb_ref,
                wo_ref, g3_ref, gate_ref, o_ref, h_ref, acc_ref, *, chunk):
    tm = x_ref.shape[0]
    i = pl.program_id(0)
    h = _rms(x_ref[...], g2_ref[...]) * (1.0 + scale_ref[...]) + shift_ref[...]
    h_ref[...] = h.astype(BF16)
    lseq = jnp.where(i * tm < N_PROMPT_TOK, SEQ, DEC_SEQ)
    pos = lax.broadcasted_iota(jnp.int32, (tm, 1), 0) & (lseq - 1)
    first = pos == 0
    last = pos == lseq - 1
    for c0 in range(0, D_FF, chunk):
        cs = slice(c0, c0 + chunk)
        gt = _dot(h_ref[...], wg_ref[:, cs])
        vl = _dot(h_ref[...], wv_ref[:, cs])
        g_prev = jnp.where(first, 0.0, pltpu.roll(gt, 1, 0))
        g_next = jnp.where(last, 0.0, pltpu.roll(gt, tm - 1, 0))
        conv = (g_prev * cw_ref[0:1, cs] + gt * cw_ref[1:2, cs]
                + g_next * cw_ref[2:3, cs] + cb_ref[:, cs])
        act = (_silu(conv) * vl).astype(BF16)
        contrib = _dot(act, wo_ref[cs, :])
        if c0 == 0:
            acc_ref[...] = contrib
        else:
            acc_ref[...] += contrib
    o_ref[...] = x_ref[...] + gate_ref[...] * _rms(acc_ref[...], g3_ref[...])


def _ffn(x, g2, shift, scale, wg, wv, cw, cb, wo, g3, gate, name):
    tm = TM_FFN
    row_spec = pl.BlockSpec((tm, D_MODEL), lambda i: (i, 0))
    return pl.pallas_call(
        functools.partial(_ffn_kernel, chunk=MXU_DIM),
        grid=(N_TOK // tm,),
        in_specs=[
            row_spec, _const_spec((1, D_MODEL)), _mod_spec(tm), _mod_spec(tm),
            _const_spec(wg.shape), _const_spec(wv.shape), _const_spec(cw.shape),
            _const_spec(cb.shape), _const_spec(wo.shape), _const_spec((1, D_MODEL)),
            _mod_spec(tm),
        ],
        out_specs=row_spec,
        out_shape=jax.ShapeDtypeStruct((N_TOK, D_MODEL), F32),
        scratch_shapes=[pltpu.VMEM((tm, D_MODEL), BF16), pltpu.VMEM((tm, D_MODEL), F32)],
        compiler_params=_cparams(1),
        name=name,
    )(x, g2, shift, scale, wg, wv, cw, cb, wo, g3, gate)


def _seq_specs(group, widths):
    if group == "prompt":
        return [pl.BlockSpec((SEQ, w), lambda b: (b, 0)) for w in widths]
    off = N_PROMPT_TOK // DEC_SEQ
    return [pl.BlockSpec((DEC_SEQ, w), lambda b: (b + off, 0)) for w in widths]


def _short_conv_chunk(pad_ref, cw_ref, cb_ref, base, t, halo, taps, left):
    win = pad_ref[pl.ds(base, t + 2 * halo), :]
    n = t + 2 * halo
    acc = cb_ref[...]
    for j in range(taps):
        s = (left - j) % n
        rolled = win if s == 0 else pltpu.roll(win, s, 0)
        acc = acc + rolled[halo:halo + t] * cw_ref[j:j + 1, :]
    return acc


def _fill_padded(pad_ref, src, length, halo):
    width = pad_ref.shape[1]
    pad_ref[0:halo, :] = jnp.zeros((halo, width), F32)
    pad_ref[halo + length:2 * halo + length, :] = jnp.zeros((halo, width), F32)
    pad_ref[halo:halo + length, :] = src


def _lane_pairs(m, first_col, n_pairs, rows):
    lane = lax.broadcasted_iota(jnp.int32, (rows, LANES), 1)
    lo_half = lane < SSD_HEADDIM
    pieces = []
    for k in range(n_pairs):
        c = first_col + 2 * k
        lo = jnp.broadcast_to(m[:, c:c + 1], (rows, LANES))
        hi = jnp.broadcast_to(m[:, c + 1:c + 2], (rows, LANES))
        pieces.append(jnp.where(lo_half, lo, hi))
    return jnp.concatenate(pieces, axis=1)


def _ssd_kernel(*refs, length, has_h0, has_state_out):
    it = iter(refs)
    z_ref, xbc_ref, dt_ref = next(it), next(it), next(it)
    cw_ref, cb_ref, dtb_ref, alog_ref, dskip_ref, nw_ref = (next(it) for _ in range(6))
    h0_ref = next(it) if has_h0 else None
    if not has_state_out:
        next(it)
    y_ref = next(it)
    st_ref = next(it) if has_state_out else None
    pad_s, xs_s, bc_s, a_s, dtsp_s, yacc_s, state_s = (next(it) for _ in range(7))

    t = SSD_CHUNK
    nc = length // t
    halo = SUBLANES
    gw = SSD_INNER // SSD_GROUPS
    pairs_per_group = SSD_HEADS // SSD_GROUPS // 2

    _fill_padded(pad_s, xbc_ref[...], length, halo)
    a_row = -jnp.exp(alog_ref[...])

    def prep(c, carry):
        base = pl.multiple_of(c * t, t)
        rows = pl.ds(base, t)
        conv = _silu(_short_conv_chunk(pad_s, cw_ref, cb_ref, base, t, halo, SSD_CONV, 2))
        xs = conv[:, :SSD_INNER]
        xs_s[rows, :] = xs
        bc_s[rows, :] = conv[:, SSD_INNER:]
        dtsp = _softplus(dt_ref[rows, :] + dtb_ref[...])
        dtsp_s[rows, :] = dtsp
        a_s[rows, :] = dtsp * a_row
        yacc_s[rows, :] = xs * dskip_ref[...]
        return carry

    lax.fori_loop(0, nc, prep, 0)

    for d in range(2):
        if has_h0:
            for k in range(SSD_INNER // LANES):
                ks = slice(k * LANES, (k + 1) * LANES)
                state_s[d, :, ks] = h0_ref[d, ks, :].T
        else:
            state_s[d] = jnp.zeros((SSD_STATE, SSD_INNER), F32)

    ri = lax.broadcasted_iota(jnp.int32, (t, t), 0)
    ci = lax.broadcasted_iota(jnp.int32, (t, t), 1)
    keep = (ci <= ri, ci >= ri)
    tri = tuple(k.astype(F32) for k in keep)
    lane = lax.broadcasted_iota(jnp.int32, (t, LANES), 1)
    lo_half = lane < SSD_HEADDIM

    def chunk_step(c, carry):
        for d in range(2):
            cidx = c if d == 0 else nc - 1 - c
            base = pl.multiple_of(cidx * t, t)
            rows = pl.ds(base, t)
            a_c = a_s[rows, :]
            cum = jnp.dot(tri[d], a_c, preferred_element_type=F32,
                          precision=lax.Precision.HIGHEST)
            cum_t = cum.T
            edge = cum[t - 1:t, :] if d == 0 else cum[0:1, :]
            e_out = jnp.exp(cum)
            d_in = jnp.exp(edge - cum)
            c_dec = jnp.exp(edge)
            hcol = d * SSD_HEADS
            xs_c = xs_s[rows, :]
            bc = bc_s[rows, :]
            ypieces = []
            for g in range(SSD_GROUPS):
                gs = slice(g * gw, (g + 1) * gw)
                col0 = hcol + g * 2 * pairs_per_group
                dt_rep = _lane_pairs(dtsp_s[rows, :], col0, pairs_per_group, t)
                eo_rep = _lane_pairs(e_out, col0, pairs_per_group, t)
                din_rep = _lane_pairs(d_in, col0, pairs_per_group, t)
                cd_rep = _lane_pairs(c_dec, col0, pairs_per_group, 1)
                xd = xs_c[:, gs] * dt_rep
                xdd = (xd * din_rep).astype(BF16)
                b_g = bc[:, g * SSD_STATE:(g + 1) * SSD_STATE]
                c_g = bc[:, (SSD_GROUPS + g) * SSD_STATE:(SSD_GROUPS + g + 1) * SSD_STATE]
                b_bf = b_g.astype(BF16)
                c_bf = c_g.astype(BF16)
                gmat = _dot_nt(c_bf, b_bf)
                st = state_s[d, :, gs]
                y_off = _dot(c_bf, st.astype(BF16)) * eo_rep
                state_s[d, :, gs] = st * cd_rep + _dot(b_g.T.astype(BF16), xdd)
                for kk in range(pairs_per_group):
                    hh = col0 + 2 * kk
                    lmats = []
                    for hcur in (hh, hh + 1):
                        col = jnp.broadcast_to(cum[:, hcur:hcur + 1], (t, t))
                        row = jnp.broadcast_to(cum_t[hcur:hcur + 1, :], (t, t))
                        lmat = jnp.exp(jnp.where(keep[d], col - row, -jnp.inf))
                        lmats.append((gmat * lmat).astype(BF16))
                    lhs = jnp.concatenate(lmats, axis=1)
                    xdp = xd[:, kk * LANES:(kk + 1) * LANES]
                    rhs = jnp.concatenate(
                        [jnp.where(lo_half, xdp, 0.0).astype(BF16),
                         jnp.where(lo_half, 0.0, xdp).astype(BF16)], axis=0)
                    ypieces.append(_dot(lhs, rhs) + y_off[:, kk * LANES:(kk + 1) * LANES])
            yacc_s[rows, :] += jnp.concatenate(ypieces, axis=1)
        return carry

    lax.fori_loop(0, nc, chunk_step, 0)

    def finish(c, carry):
        base = pl.multiple_of(c * t, t)
        rows = pl.ds(base, t)
        y = yacc_s[rows, :] * _silu(z_ref[rows, :])
        y_ref[rows, :] = _rms(y, nw_ref[...]).astype(y_ref.dtype)
        return carry

    lax.fori_loop(0, nc, finish, 0)

    if has_state_out:
        for d in range(2):
            for k in range(SSD_INNER // LANES):
                ks = slice(k * LANES, (k + 1) * LANES)
                st_ref[d, ks, :] = state_s[d, :, ks].T


def _ssd(group, z, xbc, dt, cw, cb, dtb, alog, dskip, nw, h0, y_prev, layer_name):
    prompt = group == "prompt"
    length = SEQ if prompt else DEC_SEQ
    nb = BATCH if prompt else DEC_BATCH
    off = 0 if prompt else N_PROMPT_TOK // DEC_SEQ
    in_specs = _seq_specs(group, (SSD_INNER, SSD_CONV_DIM, DT_PAD))
    in_specs += [_const_spec(a.shape) for a in (cw, cb, dtb, alog, dskip, nw)]
    args = [z, xbc, dt, cw, cb, dtb, alog, dskip, nw]
    y_spec = pl.BlockSpec((length, SSD_INNER), lambda b: (b + off, 0))
    y_shape = jax.ShapeDtypeStruct((N_TOK, SSD_INNER), BF16)
    if prompt:
        out_specs = [y_spec, pl.BlockSpec((None, 2, SSD_INNER, SSD_STATE), lambda b: (b, 0, 0, 0))]
        out_shape = [y_shape, jax.ShapeDtypeStruct((nb, 2, SSD_INNER, SSD_STATE), F32)]
        aliases = {}
    else:
        in_specs += [pl.BlockSpec((None, 2, SSD_INNER, SSD_STATE), lambda b: (b, 0, 0, 0)),
                     pl.BlockSpec(memory_space=pl.ANY)]
        args += [h0, y_prev]
        out_specs = [y_spec]
        out_shape = [y_shape]
        aliases = {len(args) - 1: 0}
    halo = SUBLANES
    scratch = [
        pltpu.VMEM((length + 2 * halo, SSD_CONV_DIM), F32),
        pltpu.VMEM((length, SSD_INNER), F32),
        pltpu.VMEM((length, 2 * SSD_GROUPS * SSD_STATE), F32),
        pltpu.VMEM((length, DT_PAD), F32),
        pltpu.VMEM((length, DT_PAD), F32),
        pltpu.VMEM((length, SSD_INNER), F32),
        pltpu.VMEM((2, SSD_STATE, SSD_INNER), F32),
    ]
    return pl.pallas_call(
        functools.partial(_ssd_kernel, length=length, has_h0=not prompt, has_state_out=prompt),
        grid=(nb,),
        in_specs=in_specs,
        out_specs=out_specs,
        out_shape=out_shape,
        scratch_shapes=scratch,
        input_output_aliases=aliases,
        compiler_params=_cparams(1),
        name=f"ssd_{group}_{layer_name}",
    )(*args)


def _lru_kernel(*refs, length, has_h0, has_state_out):
    it = iter(refs)
    xl_ref, gl_ref = next(it), next(it)
    cw_ref, cb_ref, wa_ref, wx_ref, ba_ref, bx_ref, lam_ref = (next(it) for _ in range(7))
    h0_ref = next(it) if has_h0 else None
    if not has_state_out:
        next(it)
    o_ref = next(it)
    st_ref = next(it) if has_state_out else None
    pad_s, xc_s, a_s, u_s, h_s = (next(it) for _ in range(5))

    t = 128
    nc = length // t
    halo = SUBLANES
    n_tiles = LRU_WIDTH // MXU_DIM

    _fill_padded(pad_s, xl_ref[...], length, halo)

    def prep(c, carry):
        base = pl.multiple_of(c * t, t)
        xc_s[pl.ds(base, t), :] = _short_conv_chunk(pad_s, cw_ref, cb_ref, base, t, halo, LRU_CONV, 2)
        return carry

    lax.fori_loop(0, nc, prep, 0)

    row8 = lax.broadcasted_iota(jnp.int32, (SUBLANES, LRU_WIDTH), 0)
    n_groups = length // SUBLANES

    for d in range(2):
        sp = _softplus(-lam_ref[d:d + 1, :])

        def gates(c, carry, d=d, sp=sp):
            base = pl.multiple_of(c * t, t)
            rows = pl.ds(base, t)
            xc = xc_s[rows, :]
            xb = xc.astype(BF16)
            ra, ri = [], []
            for j in range(n_tiles):
                js = slice(j * MXU_DIM, (j + 1) * MXU_DIM)
                ra.append(_dot(xb[:, js], wa_ref[d, j]))
                ri.append(_dot(xb[:, js], wx_ref[d, j]))
            r = _sigmoid(jnp.concatenate(ra, axis=1) + ba_ref[d:d + 1, :])
            gi = _sigmoid(jnp.concatenate(ri, axis=1) + bx_ref[d:d + 1, :])
            log_a = (-LRU_C) * r * sp
            a = jnp.exp(log_a)
            a_s[rows, :] = a
            u_s[rows, :] = jnp.sqrt(-jnp.tanh(log_a) * (a * a + 1.0)) * gi * xc
            return carry

        lax.fori_loop(0, nc, gates, 0)

        if has_h0:
            carry0 = jnp.broadcast_to(h0_ref[d:d + 1, :], (SUBLANES, LRU_WIDTH))
        else:
            carry0 = jnp.zeros((SUBLANES, LRU_WIDTH), F32)

        def scan(gi_, carry, d=d):
            g = gi_ if d == 0 else n_groups - 1 - gi_
            base = pl.multiple_of(g * SUBLANES, SUBLANES)
            rows = pl.ds(base, SUBLANES)
            av = a_s[rows, :]
            uv = u_s[rows, :]
            for k in (1, 2, 4):
                if d == 0:
                    shift, valid = k, row8 >= k
                else:
                    shift, valid = SUBLANES - k, row8 < SUBLANES - k
                a_sh = pltpu.roll(av, shift, 0)
                u_sh = pltpu.roll(uv, shift, 0)
                uv = jnp.where(valid, av * u_sh + uv, uv)
                av = jnp.where(valid, av * a_sh, av)
            h = av * carry + uv
            if d == 0:
                h_s[rows, :] = h
                edge = h[SUBLANES - 1:SUBLANES, :]
            else:
                h_s[rows, :] += h
                edge = h[0:1, :]
            return jnp.broadcast_to(edge, (SUBLANES, LRU_WIDTH))

        final = lax.fori_loop(0, n_groups, scan, carry0, unroll=4)
        if has_state_out:
            st_ref[d:d + 1, :] = final[0:1, :]

    def finish(c, carry):
        base = pl.multiple_of(c * t, t)
        rows = pl.ds(base, t)
        o_ref[rows, :] = (_gelu_tanh(gl_ref[rows, :]) * h_s[rows, :]).astype(o_ref.dtype)
        return carry

    lax.fori_loop(0, nc, finish, 0)


def _lru(group, xl, gl, cw, cb, wa, wx, ba, bx, lam, h0, o_prev, layer_name):
    prompt = group == "prompt"
    length = SEQ if prompt else DEC_SEQ
    nb = BATCH if prompt else DEC_BATCH
    off = 0 if prompt else N_PROMPT_TOK // DEC_SEQ
    in_specs = _seq_specs(group, (LRU_WIDTH, LRU_WIDTH))
    in_specs += [_const_spec(a.shape) for a in (cw, cb, wa, wx, ba, bx, lam)]
    args = [xl, gl, cw, cb, wa, wx, ba, bx, lam]
    o_spec = pl.BlockSpec((length, LRU_WIDTH), lambda b: (b + off, 0))
    o_shape = jax.ShapeDtypeStruct((N_TOK, LRU_WIDTH), BF16)
    if prompt:
        out_specs = [o_spec, pl.BlockSpec((None, 2, LRU_WIDTH), lambda b: (b, 0, 0))]
        out_shape = [o_shape, jax.ShapeDtypeStruct((nb, 2, LRU_WIDTH), F32)]
        aliases = {}
    else:
        in_specs += [pl.BlockSpec((None, 2, LRU_WIDTH), lambda b: (b, 0, 0)),
                     pl.BlockSpec(memory_space=pl.ANY)]
        args += [h0, o_prev]
        out_specs = [o_spec]
        out_shape = [o_shape]
        aliases = {len(args) - 1: 0}
    halo = SUBLANES
    scratch = [pltpu.VMEM((length + 2 * halo, LRU_WIDTH), F32)]
    scratch += [pltpu.VMEM((length, LRU_WIDTH), F32) for _ in range(4)]
    return pl.pallas_call(
        functools.partial(_lru_kernel, length=length, has_h0=not prompt, has_state_out=prompt),
        grid=(nb,),
        in_specs=in_specs,
        out_specs=out_specs,
        out_shape=out_shape,
        scratch_shapes=scratch,
        input_output_aliases=aliases,
        compiler_params=_cparams(1),
        name=f"lru_{group}_{layer_name}",
    )(*args)


def _confconv_kernel(*refs, length, aliased):
    it = iter(refs)
    a_ref, g_ref, w_ref, b_ref, lng_ref, lnb_ref = (next(it) for _ in range(6))
    if aliased:
        next(it)
    o_ref = next(it)
    pad_s, acc_s = next(it), next(it)

    t = 128
    nc = length // t
    halo = 2 * SUBLANES
    left = (CONF_K - 1) // 2
    cblk = MXU_DIM
    n = t + 2 * halo

    _fill_padded(pad_s, a_ref[...] * _sigmoid(g_ref[...]), length, halo)

    def step(c, carry):
        base = pl.multiple_of(c * t, t)
        rows = pl.ds(base, t)
        for cb0 in range(0, CONV_WIDTH, cblk):
            cs = slice(cb0, cb0 + cblk)
            win = pad_s[pl.ds(base, n), cs]
            acc = jnp.broadcast_to(b_ref[:, cs], (t, cblk))
            for s in range(SUBLANES):
                shifted = win if s == 0 else pltpu.roll(win, n - s, 0)
                for m in range(n // SUBLANES):
                    j = SUBLANES * m + s - halo + left
                    if 0 <= j < CONF_K and SUBLANES * m + t <= n:
                        acc = acc + shifted[SUBLANES * m:SUBLANES * m + t] * w_ref[j:j + 1, cs]
            acc_s[:, cs] = acc
        cv = acc_s[...]
        mu = jnp.mean(cv, axis=-1, keepdims=True)
        xc = cv - mu
        var = jnp.mean(xc * xc, axis=-1, keepdims=True)
        y = (xc * lax.rsqrt(var + EPS)) * lng_ref[...] + lnb_ref[...]
        o_ref[rows, :] = _silu(y).astype(o_ref.dtype)
        return carry

    lax.fori_loop(0, nc, step, 0)


def _confconv(group, ga, gg, w, b, lng, lnb, o_prev, layer_name):
    prompt = group == "prompt"
    length = SEQ if prompt else DEC_SEQ
    nb = BATCH if prompt else DEC_BATCH
    off = 0 if prompt else N_PROMPT_TOK // DEC_SEQ
    in_specs = _seq_specs(group, (CONV_WIDTH, CONV_WIDTH))
    in_specs += [_const_spec(a.shape) for a in (w, b, lng, lnb)]
    args = [ga, gg, w, b, lng, lnb]
    aliases = {}
    if not prompt:
        in_specs.append(pl.BlockSpec(memory_space=pl.ANY))
        args.append(o_prev)
        aliases = {len(args) - 1: 0}
    halo = 2 * SUBLANES
    return pl.pallas_call(
        functools.partial(_confconv_kernel, length=length, aliased=not prompt),
        grid=(nb,),
        in_specs=in_specs,
        out_specs=pl.BlockSpec((length, CONV_WIDTH), lambda b: (b + off, 0)),
        out_shape=jax.ShapeDtypeStruct((N_TOK, CONV_WIDTH), BF16),
        scratch_shapes=[pltpu.VMEM((length + 2 * halo, CONV_WIDTH), F32),
                        pltpu.VMEM((128, CONV_WIDTH), F32)],
        input_output_aliases=aliases,
        compiler_params=_cparams(1),
        name=f"confconv_{group}_{layer_name}",
    )(*args)


def _rope(x, cos_t, sin_t):
    lane = lax.broadcasted_iota(jnp.int32, x.shape, 1)
    quarter = ROPE_AXIS_DIM // 2
    partner = jnp.where((lane & quarter) == 0,
                        pltpu.roll(x, HEAD_DIM - quarter, 1), pltpu.roll(x, quarter, 1))
    return x * cos_t + partner * sin_t


def _attn_kernel(*refs, length, n_ctx, use_rope, aliased):
    it = iter(refs)
    q_ref, k_ref, v_ref, qg_ref, kg_ref = (next(it) for _ in range(5))
    if n_ctx:
        ck_ref, cv_ref = next(it), next(it)
    if use_rope:
        cos_ref, sin_ref = next(it), next(it)
    if aliased:
        next(it)
    o_ref = next(it)
    kn_ref = None if aliased else next(it)
    kall_s, vall_s = next(it), next(it)

    qb = 128
    nq = length // qb
    rep = N_HEADS // N_KV_HEADS
    scale = HEAD_DIM ** -0.5

    for g in range(N_KV_HEADS):
        gs = slice(g * HEAD_DIM, (g + 1) * HEAD_DIM)
        kn = _rms(k_ref[:, gs], kg_ref[...])
        if kn_ref is not None:
            kn_ref[:, gs] = kn
        if use_rope:
            kn = _rope(kn, cos_ref[...], sin_ref[...])
        if n_ctx:
            kall_s[g, 0:n_ctx, :] = ck_ref[:, gs].astype(BF16)
            vall_s[g, 0:n_ctx, :] = cv_ref[:, gs].astype(BF16)
        kall_s[g, n_ctx:n_ctx + length, :] = kn.astype(BF16)
        vall_s[g, n_ctx:n_ctx + length, :] = v_ref[:, gs].astype(BF16)

    def qblock(i, carry):
        base = pl.multiple_of(i * qb, qb)
        rows = pl.ds(base, qb)
        for g in range(N_KV_HEADS):
            qs = []
            for r in range(rep):
                hs = slice((g * rep + r) * HEAD_DIM, (g * rep + r + 1) * HEAD_DIM)
                qn = _rms(q_ref[rows, hs], qg_ref[...])
                if use_rope:
                    qn = _rope(qn, cos_ref[rows, :], sin_ref[rows, :])
                qs.append(qn.astype(BF16))
            qst = jnp.concatenate(qs, axis=0)
            s = _dot_nt(qst, kall_s[g]) * scale
            m = jnp.max(s, axis=-1, keepdims=True)
            p = jnp.exp(s - m)
            p = p / jnp.sum(p, axis=-1, keepdims=True)
            o = _dot(p.astype(BF16), vall_s[g])
            for r in range(rep):
                hs = slice((g * rep + r) * HEAD_DIM, (g * rep + r + 1) * HEAD_DIM)
                o_ref[rows, hs] = o[r * qb:(r + 1) * qb].astype(o_ref.dtype)
        return carry

    lax.fori_loop(0, nq, qblock, 0)


def _attn(group, q, k, v, qg, kg, ck, cv, cos_t, sin_t, o_prev, layer_name):
    prompt = group == "prompt"
    length = SEQ if prompt else DEC_SEQ
    nb = BATCH if prompt else DEC_BATCH
    off = 0 if prompt else N_PROMPT_TOK // DEC_SEQ
    kvw = N_KV_HEADS * HEAD_DIM
    n_ctx = 0 if prompt else PAST_LEN
    in_specs = _seq_specs(group, (D_MODEL, kvw, kvw))
    in_specs += [_const_spec(qg.shape), _const_spec(kg.shape)]
    args = [q, k, v, qg, kg]
    o_spec = pl.BlockSpec((length, D_MODEL), lambda b: (b + off, 0))
    o_shape = jax.ShapeDtypeStruct((N_TOK, D_MODEL), BF16)
    if prompt:
        out_specs = [o_spec, pl.BlockSpec((length, kvw), lambda b: (b, 0))]
        out_shape = [o_shape, jax.ShapeDtypeStruct((N_PROMPT_TOK, kvw), F32)]
        aliases = {}
    else:
        in_specs += [pl.BlockSpec((None, n_ctx, kvw), lambda b: (b, 0, 0)),
                     pl.BlockSpec((None, n_ctx, kvw), lambda b: (b, 0, 0)),
                     _const_spec(cos_t.shape), _const_spec(sin_t.shape),
                     pl.BlockSpec(memory_space=pl.ANY)]
        args += [ck, cv, cos_t, sin_t, o_prev]
        out_specs = [o_spec]
        out_shape = [o_shape]
        aliases = {len(args) - 1: 0}
    return pl.pallas_call(
        functools.partial(_attn_kernel, length=length, n_ctx=n_ctx, use_rope=not prompt,
                          aliased=not prompt),
        grid=(nb,),
        in_specs=in_specs,
        out_specs=out_specs,
        out_shape=out_shape,
        scratch_shapes=[pltpu.VMEM((N_KV_HEADS, n_ctx + length, HEAD_DIM), BF16),
                        pltpu.VMEM((N_KV_HEADS, n_ctx + length, HEAD_DIM), BF16)],
        input_output_aliases=aliases,
        compiler_params=_cparams(1),
        name=f"attn_{group}_{layer_name}",
    )(*args)


def _rope_tables(rows):
    row_pos = jnp.repeat(jnp.arange(rows, dtype=F32), GRID_W)
    col_pos = jnp.tile(jnp.arange(GRID_W, dtype=F32), rows)
    inv_freq = jnp.power(ROPE_THETA, -jnp.arange(0, ROPE_AXIS_DIM, 2, dtype=F32) / ROPE_AXIS_DIM)
    ang_r = row_pos[:, None] * inv_freq
    ang_c = col_pos[:, None] * inv_freq
    cos_t = jnp.concatenate([jnp.cos(ang_r), jnp.cos(ang_r), jnp.cos(ang_c), jnp.cos(ang_c)], axis=1)
    sin_t = jnp.concatenate([-jnp.sin(ang_r), jnp.sin(ang_r), -jnp.sin(ang_c), jnp.sin(ang_c)], axis=1)
    return cos_t, sin_t


def _block_diag_tiles(w):
    per = MXU_DIM // LRU_BW
    n_tiles = LRU_BLOCKS // per
    w = w.reshape(2, n_tiles, per, LRU_BW, LRU_BW)
    eye = jnp.eye(per, dtype=w.dtype)
    tiles = jnp.einsum('dtpio,pq->dtpiqo', w, eye)
    return tiles.reshape(2, n_tiles, MXU_DIM, MXU_DIM).astype(BF16)


def _row(v):
    return v.reshape(1, -1)


def kernel(x_prompt, x_sample, state_ssd, state_lru, cache_k, cache_v, c, c_ctx,
           w_mod, b_mod, norm_g, w_in_ssm, ssd_conv_w, ssd_conv_b, ssd_a_log, ssd_dt_bias,
           ssd_d, ssd_norm_w, lru_conv_w, lru_conv_b, lru_wa, lru_ba, lru_wx, lru_bx,
           lru_lambda, w_out_ssm, w_in_ca, conf_dw_w, conf_dw_b, conf_ln_g, conf_ln_b,
           q_norm_g, k_norm_g, w_out_ca, ffn_w_in, ffn_conv_w, ffn_conv_b, ffn_w_out):
    x = jnp.concatenate([x_prompt.reshape(N_PROMPT_TOK, D_MODEL),
                         x_sample.reshape(N_SAMPLE_TOK, D_MODEL)], axis=0)
    cvec = jnp.concatenate(
        [c_ctx[None], c, jnp.zeros((N_MOD_ROWS - 1 - DEC_BATCH, D_MODEL), F32)], axis=0)
    mods = _modulation_all(cvec, w_mod, b_mod)
    cos_t, sin_t = _rope_tables(DEC_SEQ // GRID_W)
    kvw = N_KV_HEADS * HEAD_DIM

    ssd_states, lru_states, k_out, v_out = [], [], [], []
    for layer in range(DEPTH):
        i = layer // 2
        m = mods[layer]
        name = f"l{layer}"
        g0, g1, g2, g3 = (_row(norm_g[layer, j]) for j in range(4))
        if layer % 2 == 0:
            w_in = w_in_ssm[i].astype(BF16)
            o1 = SSD_INNER
            o2 = o1 + SSD_CONV_DIM
            o3 = o2 + 2 * SSD_HEADS
            o4 = o3 + LRU_WIDTH
            w_dt = jnp.pad(w_in[:, o2:o3], ((0, 0), (0, DT_PAD - 2 * SSD_HEADS)))
            z, xbc, dt, xl, gl = _inproj(
                x, g0, m[0], m[1],
                [w_in[:, :o1], w_in[:, o1:o2], w_dt, w_in[:, o3:o4], w_in[:, o4:]],
                f"inproj_ssm_{name}")
            pad32 = DT_PAD - 2 * SSD_HEADS
            dtb = jnp.pad(ssd_dt_bias[i].reshape(1, -1), ((0, 0), (0, pad32)))
            alog = jnp.pad(ssd_a_log[i].reshape(1, -1), ((0, 0), (0, pad32)))
            dskip = _row(jnp.repeat(ssd_d[i], SSD_HEADDIM))
            ssd_args = (ssd_conv_w[i], _row(ssd_conv_b[i]), dtb, alog, dskip, _row(ssd_norm_w[i]))
            y, s_p = _ssd("prompt", z, xbc, dt, *ssd_args, None, None, name)
            (y,) = _ssd("sample", z, xbc, dt, *ssd_args,
                        state_ssd[:, i].reshape(DEC_BATCH, 2, SSD_INNER, SSD_STATE), y, name)
            lru_args = (lru_conv_w[i], _row(lru_conv_b[i]), _block_diag_tiles(lru_wa[i]),
                        _block_diag_tiles(lru_wx[i]), lru_ba[i], lru_bx[i], lru_lambda[i])
            yl, l_p = _lru("prompt", xl, gl, *lru_args, None, None, name)
            (yl,) = _lru("sample", xl, gl, *lru_args, state_lru[:, i], yl, name)
            ssd_states.append(s_p.reshape(BATCH, 2, SSD_HEADS, SSD_HEADDIM, SSD_STATE))
            lru_states.append(l_p)
            w_out = w_out_ssm[i].astype(BF16)
            x = _outproj(y, yl, w_out[:SSD_INNER], w_out[SSD_INNER:], x, g1, m[2],
                         f"outproj_ssm_{name}")
        else:
            w_in = w_in_ca[i].astype(BF16)
            o1 = CONV_WIDTH
            o2 = 2 * CONV_WIDTH
            o3 = o2 + N_HEADS * HEAD_DIM
            o4 = o3 + kvw
            ga, gg, q, k, v = _inproj(
                x, g0, m[0], m[1],
                [w_in[:, :o1], w_in[:, o1:o2], w_in[:, o2:o3], w_in[:, o3:o4], w_in[:, o4:]],
                f"inproj_ca_{name}")
            conv_args = (conf_dw_w[i], _row(conf_dw_b[i]), _row(conf_ln_g[i]), _row(conf_ln_b[i]))
            cvo = _confconv("prompt", ga, gg, *conv_args, None, name)
            cvo = _confconv("sample", ga, gg, *conv_args, cvo, name)
            qg, kg = _row(q_norm_g[i]), _row(k_norm_g[i])
            o, kn = _attn("prompt", q, k, v, qg, kg, None, None, None, None, None, name)
            (o,) = _attn("sample", q, k, v, qg, kg,
                         cache_k[:, i].reshape(DEC_BATCH, PAST_LEN, kvw),
                         cache_v[:, i].reshape(DEC_BATCH, PAST_LEN, kvw),
                         cos_t, sin_t, o, name)
            k_out.append(kn.reshape(BATCH, SEQ, N_KV_HEADS, HEAD_DIM))
            v_out.append(v[:N_PROMPT_TOK].reshape(BATCH, SEQ, N_KV_HEADS, HEAD_DIM))
            w_out = w_out_ca[i].astype(BF16)
            x = _outproj(cvo, o, w_out[:CONV_WIDTH], w_out[CONV_WIDTH:], x, g1, m[2],
                         f"outproj_ca_{name}")
        w_ff = ffn_w_in[layer].astype(BF16)
        x = _ffn(x, g2, m[3], m[4], w_ff[:, :D_FF], w_ff[:, D_FF:], ffn_conv_w[layer],
                 _row(ffn_conv_b[layer]), ffn_w_out[layer].astype(BF16), g3, m[5],
                 f"ffn_{name}")

    xp = x[:N_PROMPT_TOK].reshape(BATCH, SEQ, D_MODEL)
    xs = x[N_PROMPT_TOK:].reshape(DEC_BATCH, DEC_SEQ, D_MODEL)
    return (xp, xs, jnp.stack(ssd_states, axis=1), jnp.stack(lru_states, axis=1),
            jnp.stack(k_out, axis=1), jnp.stack(v_out, axis=1))
```

```python
import functools
import math

import jax
import jax.numpy as jnp
from jax import lax
from jax.experimental import pallas as pl
from jax.experimental.pallas import tpu as pltpu

F32 = jnp.float32
BF16 = jnp.bfloat16

D_MODEL = 1024
BATCH = 16
SEQ = 256
DEPTH = 4
DEC_BATCH = 4
DEC_SEQ = 1024
PAST_LEN = 512
GRID_W = 64
EPS = 1e-6
SSD_HEADDIM = 64
SSD_INNER = D_MODEL
SSD_HEADS = SSD_INNER // SSD_HEADDIM
SSD_GROUPS = 2
SSD_STATE = 128
SSD_CONV = 4
SSD_CHUNK = 128
SSD_CONV_DIM = SSD_INNER + 2 * SSD_GROUPS * SSD_STATE
LRU_WIDTH = D_MODEL
LRU_BW = 64
LRU_BLOCKS = LRU_WIDTH // LRU_BW
LRU_CONV = 4
LRU_C = 8.0
CONV_WIDTH = D_MODEL
CONF_K = 31
HEAD_DIM = 128
N_HEADS = D_MODEL // HEAD_DIM
N_KV_HEADS = 2
ROPE_THETA = 10000.0
ROPE_AXIS_DIM = HEAD_DIM // 2
D_FF = 2816
FFN_CONV = 3

N_PROMPT_TOK = BATCH * SEQ
N_SAMPLE_TOK = DEC_BATCH * DEC_SEQ
N_TOK = N_PROMPT_TOK + N_SAMPLE_TOK
N_MOD_ROWS = 8
LANES = 128
SUBLANES = 8
MXU_DIM = 256
DT_PAD = LANES
VMEM_LIMIT = 58 * 1024 * 1024

TM_LINEAR = 512
TM_FFN = 1024


def _cparams(n_axes):
    return pltpu.CompilerParams(
        dimension_semantics=("arbitrary",) * n_axes,
        vmem_limit_bytes=VMEM_LIMIT)


def _const_spec(shape):
    nd = len(shape)
    return pl.BlockSpec(shape, lambda *_: (0,) * nd, pipeline_mode=pl.Buffered(1))


def _mod_row(i, tm):
    start = i * tm
    return jnp.where(start < N_PROMPT_TOK, 0, 1 + (start - N_PROMPT_TOK) // DEC_SEQ)


def _mod_spec(tm):
    return pl.BlockSpec((None, 1, D_MODEL), lambda i: (_mod_row(i, tm), 0, 0))


def _sigmoid(x):
    return jax.nn.sigmoid(x)


def _silu(x):
    return x * _sigmoid(x)


def _softplus(x):
    return jnp.maximum(x, 0.0) + jnp.log1p(jnp.exp(-jnp.abs(x)))


def _gelu_tanh(x):
    return 0.5 * x * (1.0 + jnp.tanh(math.sqrt(2.0 / math.pi) * (x + 0.044715 * (x * x * x))))


def _rms(x, g):
    ms = jnp.mean(x * x, axis=-1, keepdims=True)
    return (x * lax.rsqrt(ms + EPS)) * g


def _dot(a, b):
    return jnp.dot(a, b, preferred_element_type=F32)


def _dot_nt(a, b):
    return lax.dot_general(a, b, (((1,), (1,)), ((), ())), preferred_element_type=F32)


def _mod_kernel(c_ref, w_ref, b_ref, o_ref):
    c = c_ref[...]
    s = _silu(c).astype(BF16)
    o_ref[...] = _dot(s, w_ref[...].astype(BF16)) + b_ref[...]


def _modulation_all(cvec, w_mod, b_mod):
    tn = 1536
    n_out = 6 * D_MODEL
    out = pl.pallas_call(
        _mod_kernel,
        grid=(DEPTH, n_out // tn),
        in_specs=[
            pl.BlockSpec((N_MOD_ROWS, D_MODEL), lambda l, j: (0, 0)),
            pl.BlockSpec((None, D_MODEL, tn), lambda l, j: (l, 0, j)),
            pl.BlockSpec((None, 1, tn), lambda l, j: (l, 0, j)),
        ],
        out_specs=pl.BlockSpec((None, N_MOD_ROWS, tn), lambda l, j: (l, 0, j)),
        out_shape=jax.ShapeDtypeStruct((DEPTH, N_MOD_ROWS, n_out), F32),
        compiler_params=_cparams(2),
        name="modulation",
    )(cvec, w_mod, b_mod.reshape(DEPTH, 1, n_out))
    out = out.reshape(DEPTH, N_MOD_ROWS, 6, 1, D_MODEL)
    return jnp.transpose(out, (0, 2, 1, 3, 4))


def _inproj_kernel(x_ref, g_ref, shift_ref, scale_ref, w_ref, *rest, widths, chunk):
    o_refs = rest[:len(widths)]
    h_ref = rest[len(widths)]
    h = _rms(x_ref[...], g_ref[...]) * (1.0 + scale_ref[...]) + shift_ref[...]
    h_ref[...] = h.astype(BF16)
    off = 0
    for o_ref, n in zip(o_refs, widths):
        for c0 in range(0, n, chunk):
            c1 = min(c0 + chunk, n)
            o_ref[:, c0:c1] = _dot(h_ref[...], w_ref[:, off + c0:off + c1])
        off += n


def _inproj(x, g, shift, scale, w, widths, name):
    tm = TM_LINEAR
    assert sum(widths) == w.shape[1] and all(n % LANES == 0 for n in widths)
    row_spec = pl.BlockSpec((tm, D_MODEL), lambda i: (i, 0))
    return pl.pallas_call(
        functools.partial(_inproj_kernel, widths=tuple(widths), chunk=512),
        grid=(N_TOK // tm,),
        in_specs=[row_spec, _const_spec((1, D_MODEL)), _mod_spec(tm), _mod_spec(tm),
                  _const_spec(w.shape)],
        out_specs=[pl.BlockSpec((tm, n), lambda i: (i, 0)) for n in widths],
        out_shape=[jax.ShapeDtypeStruct((N_TOK, n), F32) for n in widths],
        scratch_shapes=[pltpu.VMEM((tm, D_MODEL), BF16)],
        compiler_params=_cparams(1),
        name=name,
    )(x, g, shift, scale, w)


def _outproj_kernel(a_ref, b_ref, w_ref, x_ref, g_ref, gate_ref, o_ref):
    ka = a_ref.shape[1]
    acc = _dot(a_ref[...], w_ref[0:ka, :]) + _dot(b_ref[...], w_ref[ka:, :])
    o_ref[...] = x_ref[...] + gate_ref[...] * _rms(acc, g_ref[...])


def _outproj(a, b, w, x, g, gate, name):
    tm = TM_LINEAR
    return pl.pallas_call(
        _outproj_kernel,
        grid=(N_TOK // tm,),
        in_specs=[
            pl.BlockSpec((tm, a.shape[1]), lambda i: (i, 0)),
            pl.BlockSpec((tm, b.shape[1]), lambda i: (i, 0)),
            _const_spec(w.shape),
            pl.BlockSpec((tm, D_MODEL), lambda i: (i, 0)),
            _const_spec((1, D_MODEL)),
            _mod_spec(tm),
        ],
        out_specs=pl.BlockSpec((tm, D_MODEL), lambda i: (i, 0)),
        out_shape=jax.ShapeDtypeStruct((N_TOK, D_MODEL), F32),
        compiler_params=_cparams(1),
        name=name,
    )(a, b, w, x, g, gate)


def _ffn_kernel(x_ref, g2_ref, shift_ref, scale_ref, wi_ref, cw_ref, cb_ref,
                wo_ref, g3_ref, gate_ref, o_ref, h_ref, act_ref, *, chunk, row_blk):
    tm = x_ref.shape[0]
    i = pl.program_id(0)
    h = _rms(x_ref[...], g2_ref[...]) * (1.0 + scale_ref[...]) + shift_ref[...]
    h_ref[...] = h.astype(BF16)
    lseq = jnp.where(i * tm < N_PROMPT_TOK, SEQ, DEC_SEQ)
    pos = lax.broadcasted_iota(jnp.int32, (tm, 1), 0) & (lseq - 1)
    first = pos == 0
    last = pos == lseq - 1
    for c0 in range(0, D_FF, chunk):
        c1 = min(c0 + chunk, D_FF)
        gt = _dot(h_ref[...], wi_ref[:, c0:c1])
        vl = _dot(h_ref[...], wi_ref[:, D_FF + c0:D_FF + c1])
        g_prev = jnp.where(first, 0.0, pltpu.roll(gt, 1, 0))
        g_next = jnp.where(last, 0.0, pltpu.roll(gt, tm - 1, 0))
        conv = (g_prev * cw_ref[0:1, c0:c1] + gt * cw_ref[1:2, c0:c1]
                + g_next * cw_ref[2:3, c0:c1] + cb_ref[:, c0:c1])
        act_ref[:, c0:c1] = (_silu(conv) * vl).astype(BF16)
    for r0 in range(0, tm, row_blk):
        rs = slice(r0, r0 + row_blk)
        acc = _dot(act_ref[rs, :], wo_ref[...])
        o_ref[rs, :] = x_ref[rs, :] + gate_ref[...] * _rms(acc, g3_ref[...])


def _ffn(x, g2, shift, scale, wi, cw, cb, wo, g3, gate, name):
    tm = TM_FFN
    row_spec = pl.BlockSpec((tm, D_MODEL), lambda i: (i, 0))
    return pl.pallas_call(
        functools.partial(_ffn_kernel, chunk=2 * MXU_DIM, row_blk=MXU_DIM),
        grid=(N_TOK // tm,),
        in_specs=[
            row_spec, _const_spec((1, D_MODEL)), _mod_spec(tm), _mod_spec(tm),
            _const_spec(wi.shape), _const_spec(cw.shape),
            _const_spec(cb.shape), _const_spec(wo.shape), _const_spec((1, D_MODEL)),
            _mod_spec(tm),
        ],
        out_specs=row_spec,
        out_shape=jax.ShapeDtypeStruct((N_TOK, D_MODEL), F32),
        scratch_shapes=[pltpu.VMEM((tm, D_MODEL), BF16), pltpu.VMEM((tm, D_FF), BF16)],
        compiler_params=_cparams(1),
        name=name,
    )(x, g2, shift, scale, wi, cw, cb, wo, g3, gate)


def _seq_specs(group, widths):
    if group == "prompt":
        return [pl.BlockSpec((SEQ, w), lambda b: (b, 0)) for w in widths]
    off = N_PROMPT_TOK // DEC_SEQ
    return [pl.BlockSpec((DEC_SEQ, w), lambda b: (b + off, 0)) for w in widths]


def _short_conv_chunk(pad_ref, cw_ref, cb_ref, base, t, halo, taps, left):
    win = pad_ref[pl.ds(base, t + 2 * halo), :]
    n = t + 2 * halo
    acc = cb_ref[...]
    for j in range(taps):
        s = (left - j) % n
        rolled = win if s == 0 else pltpu.roll(win, s, 0)
        acc = acc + rolled[halo:halo + t] * cw_ref[j:j + 1, :]
    return acc


def _fill_padded(pad_ref, src, length, halo):
    width = pad_ref.shape[1]
    pad_ref[0:halo, :] = jnp.zeros((halo, width), F32)
    pad_ref[halo + length:2 * halo + length, :] = jnp.zeros((halo, width), F32)
    pad_ref[halo:halo + length, :] = src


def _lane_pairs(m, first_col, n_pairs, rows):
    lane = lax.broadcasted_iota(jnp.int32, (rows, LANES), 1)
    lo_half = lane < SSD_HEADDIM
    pieces = []
    for k in range(n_pairs):
        c = first_col + 2 * k
        lo = jnp.broadcast_to(m[:, c:c + 1], (rows, LANES))
        hi = jnp.broadcast_to(m[:, c + 1:c + 2], (rows, LANES))
        pieces.append(jnp.where(lo_half, lo, hi))
    return jnp.concatenate(pieces, axis=1)


def _ssd_kernel(*refs, length, has_h0, has_state_out):
    it = iter(refs)
    z_ref, xbc_ref, dt_ref = next(it), next(it), next(it)
    cw_ref, cb_ref, dtb_ref, alog_ref, dskip_ref, nw_ref = (next(it) for _ in range(6))
    h0_ref = next(it) if has_h0 else None
    if not has_state_out:
        next(it)
    y_ref = next(it)
    st_ref = next(it) if has_state_out else None
    pad_s, xs_s, bc_s, cum_s, cum_t_s, dt_t_s, w_t_s, cd_s, yacc_s, state_s = (
        next(it) for _ in range(10))

    t = SSD_CHUNK
    nc = length // t
    halo = SUBLANES
    gw = SSD_INNER // SSD_GROUPS
    pairs_per_group = SSD_HEADS // SSD_GROUPS // 2

    _fill_padded(pad_s, xbc_ref[...], length, halo)
    a_row = -jnp.exp(alog_ref[...])

    ri = lax.broadcasted_iota(jnp.int32, (t, t), 0)
    ci = lax.broadcasted_iota(jnp.int32, (t, t), 1)
    keep = (ci <= ri, ci >= ri)
    tril = keep[0].astype(F32)
    lane = lax.broadcasted_iota(jnp.int32, (t, LANES), 1)
    lo_half = lane < SSD_HEADDIM
    fwd_cols = lane < SSD_HEADS
    fwd_rows = ri < SSD_HEADS

    def prep(c, carry):
        base = pl.multiple_of(c * t, t)
        rows = pl.ds(base, t)
        conv = _silu(_short_conv_chunk(pad_s, cw_ref, cb_ref, base, t, halo, SSD_CONV, 2))
        xs = conv[:, :SSD_INNER]
        xs_s[rows, :] = xs
        bc_s[rows, :] = conv[:, SSD_INNER:]
        yacc_s[rows, :] = xs * dskip_ref[...]
        dtsp = _softplus(dt_ref[rows, :] + dtb_ref[...])
        a_c = dtsp * a_row
        pre = jnp.dot(tril, a_c, preferred_element_type=F32, precision=lax.Precision.HIGHEST)
        suf = pre[t - 1:t, :] - pre + a_c
        cum = jnp.where(fwd_cols, pre, suf)
        cum_s[rows, :] = cum
        cum_t = cum.T
        dt_t = dtsp.T
        edge_col = jnp.where(fwd_rows[:, 0:1], cum_t[:, t - 1:t], cum_t[:, 0:1])
        cum_t_s[c] = cum_t
        dt_t_s[c] = dt_t
        w_t_s[c] = dt_t * jnp.exp(edge_col - cum_t)
        edge_row = jnp.where(fwd_cols[0:1, :], cum[t - 1:t, :], cum[0:1, :])
        cd_s[c] = jnp.broadcast_to(jnp.exp(edge_row), (SUBLANES, LANES))
        return carry

    lax.fori_loop(0, nc, prep, 0)

    for d in range(2):
        if has_h0:
            for k in range(SSD_INNER // LANES):
                ks = slice(k * LANES, (k + 1) * LANES)
                state_s[d, :, ks] = h0_ref[d, ks, :].T
        else:
            state_s[d] = jnp.zeros((SSD_STATE, SSD_INNER), F32)

    def block_diag(m):
        return jnp.concatenate([jnp.where(lo_half, m, 0.0).astype(BF16),
                                jnp.where(lo_half, 0.0, m).astype(BF16)], axis=0)

    def chunk_step(c, carry):
        for d in range(2):
            cidx = c if d == 0 else nc - 1 - c
            base = pl.multiple_of(cidx * t, t)
            rows = pl.ds(base, t)
            cum = cum_s[rows, :]
            bc = bc_s[rows, :]
            cd = cd_s[cidx][0:1, :]
            for g in range(SSD_GROUPS):
                col0 = d * SSD_HEADS + g * 2 * pairs_per_group
                b_g = bc[:, g * SSD_STATE:(g + 1) * SSD_STATE]
                c_g = bc[:, (SSD_GROUPS + g) * SSD_STATE:(SSD_GROUPS + g + 1) * SSD_STATE]
                gmat = _dot_nt(c_g.astype(BF16), b_g.astype(BF16))
                b_t = b_g.T
                cd_rep = _lane_pairs(cd, col0, pairs_per_group, 1)
                for kk in range(pairs_per_group):
                    lanes = slice(g * gw + kk * LANES, g * gw + (kk + 1) * LANES)
                    rhs_x = block_diag(xs_s[rows, lanes])
                    st = state_s[d, :, lanes]
                    gl, ce, bw = [], [], []
                    for hcur in (col0 + 2 * kk, col0 + 2 * kk + 1):
                        hrow = pl.ds(hcur, 1)
                        colb = jnp.broadcast_to(cum[:, hcur:hcur + 1], (t, t))
                        rowb = jnp.broadcast_to(cum_t_s[cidx, hrow, :], (t, t))
                        dtrow = jnp.broadcast_to(dt_t_s[cidx, hrow, :], (t, t))
                        wrow = jnp.broadcast_to(w_t_s[cidx, hrow, :], (t, t))
                        lmat = jnp.exp(jnp.where(keep[d], colb - rowb, -jnp.inf))
                        gl.append((gmat * lmat * dtrow).astype(BF16))
                        ce.append((c_g * jnp.exp(colb)).astype(BF16))
                        bw.append((b_t * wrow).astype(BF16))
                    y_p = _dot(jnp.concatenate(gl + ce, axis=1),
                               jnp.concatenate([rhs_x, block_diag(st)], axis=0))
                    state_s[d, :, lanes] = (st * cd_rep[:, kk * LANES:(kk + 1) * LANES]
                                            + _dot(jnp.concatenate(bw, axis=1), rhs_x))
                    yacc_s[rows, lanes] += y_p
        return carry

    lax.fori_loop(0, nc, chunk_step, 0)

    def finish(c, carry):
        base = pl.multiple_of(c * t, t)
        rows = pl.ds(base, t)
        y = yacc_s[rows, :] * _silu(z_ref[rows, :])
        y_ref[rows, :] = _rms(y, nw_ref[...]).astype(y_ref.dtype)
        return carry

    lax.fori_loop(0, nc, finish, 0)

    if has_state_out:
        for d in range(2):
            for k in range(SSD_INNER // LANES):
                ks = slice(k * LANES, (k + 1) * LANES)
                st_ref[d, ks, :] = state_s[d, :, ks].T


def _ssd(group, z, xbc, dt, cw, cb, dtb, alog, dskip, nw, h0, y_prev, layer_name):
    prompt = group == "prompt"
    length = SEQ if prompt else DEC_SEQ
    nb = BATCH if prompt else DEC_BATCH
    off = 0 if prompt else N_PROMPT_TOK // DEC_SEQ
    in_specs = _seq_specs(group, (SSD_INNER, SSD_CONV_DIM, DT_PAD))
    in_specs += [_const_spec(a.shape) for a in (cw, cb, dtb, alog, dskip, nw)]
    args = [z, xbc, dt, cw, cb, dtb, alog, dskip, nw]
    y_spec = pl.BlockSpec((length, SSD_INNER), lambda b: (b + off, 0))
    y_shape = jax.ShapeDtypeStruct((N_TOK, SSD_INNER), BF16)
    if prompt:
        out_specs = [y_spec, pl.BlockSpec((None, 2, SSD_INNER, SSD_STATE), lambda b: (b, 0, 0, 0))]
        out_shape = [y_shape, jax.ShapeDtypeStruct((nb, 2, SSD_INNER, SSD_STATE), F32)]
        aliases = {}
    else:
        in_specs += [pl.BlockSpec((None, 2, SSD_INNER, SSD_STATE), lambda b: (b, 0, 0, 0)),
                     pl.BlockSpec(memory_space=pl.ANY)]
        args += [h0, y_prev]
        out_specs = [y_spec]
        out_shape = [y_shape]
        aliases = {len(args) - 1: 0}
    halo = SUBLANES
    nc = length // SSD_CHUNK
    scratch = [
        pltpu.VMEM((length + 2 * halo, SSD_CONV_DIM), F32),
        pltpu.VMEM((length, SSD_INNER), F32),
        pltpu.VMEM((length, 2 * SSD_GROUPS * SSD_STATE), F32),
        pltpu.VMEM((length, DT_PAD), F32),
        pltpu.VMEM((nc, DT_PAD, SSD_CHUNK), F32),
        pltpu.VMEM((nc, DT_PAD, SSD_CHUNK), F32),
        pltpu.VMEM((nc, DT_PAD, SSD_CHUNK), F32),
        pltpu.VMEM((nc, SUBLANES, DT_PAD), F32),
        pltpu.VMEM((length, SSD_INNER), F32),
        pltpu.VMEM((2, SSD_STATE, SSD_INNER), F32),
    ]
    return pl.pallas_call(
        functools.partial(_ssd_kernel, length=length, has_h0=not prompt, has_state_out=prompt),
        grid=(nb,),
        in_specs=in_specs,
        out_specs=out_specs,
        out_shape=out_shape,
        scratch_shapes=scratch,
        input_output_aliases=aliases,
        compiler_params=_cparams(1),
        name=f"ssd_{group}_{layer_name}",
    )(*args)


def _lru_kernel(*refs, length, has_h0, has_state_out):
    it = iter(refs)
    xl_ref, gl_ref = next(it), next(it)
    cw_ref, cb_ref, wa_ref, wx_ref, ba_ref, bx_ref, lam_ref = (next(it) for _ in range(7))
    h0_ref = next(it) if has_h0 else None
    if not has_state_out:
        next(it)
    o_ref = next(it)
    st_ref = next(it) if has_state_out else None
    pad_s, xc_s, a_s, u_s, h_s = (next(it) for _ in range(5))

    t = 128
    nc = length // t
    halo = SUBLANES
    n_tiles = LRU_WIDTH // MXU_DIM

    _fill_padded(pad_s, xl_ref[...], length, halo)

    def prep(c, carry):
        base = pl.multiple_of(c * t, t)
        xc_s[pl.ds(base, t), :] = _short_conv_chunk(pad_s, cw_ref, cb_ref, base, t, halo, LRU_CONV, 2)
        return carry

    lax.fori_loop(0, nc, prep, 0)

    row8 = lax.broadcasted_iota(jnp.int32, (SUBLANES, LRU_WIDTH), 0)
    n_groups = length // SUBLANES

    for d in range(2):
        sp = _softplus(-lam_ref[d:d + 1, :])

        def gates(c, carry, d=d, sp=sp):
            base = pl.multiple_of(c * t, t)
            rows = pl.ds(base, t)
            xc = xc_s[rows, :]
            xb = xc.astype(BF16)
            ra, ri = [], []
            for j in range(n_tiles):
                js = slice(j * MXU_DIM, (j + 1) * MXU_DIM)
                ra.append(_dot(xb[:, js], wa_ref[d, j]))
                ri.append(_dot(xb[:, js], wx_ref[d, j]))
            r = _sigmoid(jnp.concatenate(ra, axis=1) + ba_ref[d:d + 1, :])
            gi = _sigmoid(jnp.concatenate(ri, axis=1) + bx_ref[d:d + 1, :])
            log_a = (-LRU_C) * r * sp
            a = jnp.exp(log_a)
            a_s[rows, :] = a
            u_s[rows, :] = jnp.sqrt(-jnp.tanh(log_a) * (a * a + 1.0)) * gi * xc
            return carry

        lax.fori_loop(0, nc, gates, 0)

        if has_h0:
            carry0 = jnp.broadcast_to(h0_ref[d:d + 1, :], (SUBLANES, LRU_WIDTH))
        else:
            carry0 = jnp.zeros((SUBLANES, LRU_WIDTH), F32)

        def scan(gi_, carry, d=d):
            g = gi_ if d == 0 else n_groups - 1 - gi_
            base = pl.multiple_of(g * SUBLANES, SUBLANES)
            rows = pl.ds(base, SUBLANES)
            av = a_s[rows, :]
            uv = u_s[rows, :]
            for k in (1, 2, 4):
                if d == 0:
                    shift, valid = k, row8 >= k
                else:
                    shift, valid = SUBLANES - k, row8 < SUBLANES - k
                a_sh = pltpu.roll(av, shift, 0)
                u_sh = pltpu.roll(uv, shift, 0)
                uv = jnp.where(valid, av * u_sh + uv, uv)
                av = jnp.where(valid, av * a_sh, av)
            h = av * carry + uv
            if d == 0:
                h_s[rows, :] = h
                edge = h[SUBLANES - 1:SUBLANES, :]
            else:
                h_s[rows, :] += h
                edge = h[0:1, :]
            return jnp.broadcast_to(edge, (SUBLANES, LRU_WIDTH))

        final = lax.fori_loop(0, n_groups, scan, carry0, unroll=4)
        if has_state_out:
            st_ref[d:d + 1, :] = final[0:1, :]

    def finish(c, carry):
        base = pl.multiple_of(c * t, t)
        rows = pl.ds(base, t)
        o_ref[rows, :] = (_gelu_tanh(gl_ref[rows, :]) * h_s[rows, :]).astype(o_ref.dtype)
        return carry

    lax.fori_loop(0, nc, finish, 0)


def _lru(group, xl, gl, cw, cb, wa, wx, ba, bx, lam, h0, o_prev, layer_name):
    prompt = group == "prompt"
    length = SEQ if prompt else DEC_SEQ
    nb = BATCH if prompt else DEC_BATCH
    off = 0 if prompt else N_PROMPT_TOK // DEC_SEQ
    in_specs = _seq_specs(group, (LRU_WIDTH, LRU_WIDTH))
    in_specs += [_const_spec(a.shape) for a in (cw, cb, wa, wx, ba, bx, lam)]
    args = [xl, gl, cw, cb, wa, wx, ba, bx, lam]
    o_spec = pl.BlockSpec((length, LRU_WIDTH), lambda b: (b + off, 0))
    o_shape = jax.ShapeDtypeStruct((N_TOK, LRU_WIDTH), BF16)
    if prompt:
        out_specs = [o_spec, pl.BlockSpec((None, 2, LRU_WIDTH), lambda b: (b, 0, 0))]
        out_shape = [o_shape, jax.ShapeDtypeStruct((nb, 2, LRU_WIDTH), F32)]
        aliases = {}
    else:
        in_specs += [pl.BlockSpec((None, 2, LRU_WIDTH), lambda b: (b, 0, 0)),
                     pl.BlockSpec(memory_space=pl.ANY)]
        args += [h0, o_prev]
        out_specs = [o_spec]
        out_shape = [o_shape]
        aliases = {len(args) - 1: 0}
    halo = SUBLANES
    scratch = [pltpu.VMEM((length + 2 * halo, LRU_WIDTH), F32)]
    scratch += [pltpu.VMEM((length, LRU_WIDTH), F32) for _ in range(4)]
    return pl.pallas_call(
        functools.partial(_lru_kernel, length=length, has_h0=not prompt, has_state_out=prompt),
        grid=(nb,),
        in_specs=in_specs,
        out_specs=out_specs,
        out_shape=out_shape,
        scratch_shapes=scratch,
        input_output_aliases=aliases,
        compiler_params=_cparams(1),
        name=f"lru_{group}_{layer_name}",
    )(*args)


def _confconv_kernel(*refs, length, aliased):
    it = iter(refs)
    a_ref, g_ref, w_ref, b_ref, lng_ref, lnb_ref = (next(it) for _ in range(6))
    if aliased:
        next(it)
    o_ref = next(it)
    pad_s, acc_s = next(it), next(it)

    t = 128
    nc = length // t
    halo = 2 * SUBLANES
    left = (CONF_K - 1) // 2
    cblk = MXU_DIM
    n = t + 2 * halo

    _fill_padded(pad_s, a_ref[...] * _sigmoid(g_ref[...]), length, halo)

    def step(c, carry):
        base = pl.multiple_of(c * t, t)
        rows = pl.ds(base, t)
        for cb0 in range(0, CONV_WIDTH, cblk):
            cs = slice(cb0, cb0 + cblk)
            win = pad_s[pl.ds(base, n), cs]
            acc = jnp.broadcast_to(b_ref[:, cs], (t, cblk))
            for s in range(SUBLANES):
                shifted = win if s == 0 else pltpu.roll(win, n - s, 0)
                for m in range(n // SUBLANES):
                    j = SUBLANES * m + s - halo + left
                    if 0 <= j < CONF_K and SUBLANES * m + t <= n:
                        acc = acc + shifted[SUBLANES * m:SUBLANES * m + t] * w_ref[j:j + 1, cs]
            acc_s[:, cs] = acc
        cv = acc_s[...]
        mu = jnp.mean(cv, axis=-1, keepdims=True)
        xc = cv - mu
        var = jnp.mean(xc * xc, axis=-1, keepdims=True)
        y = (xc * lax.rsqrt(var + EPS)) * lng_ref[...] + lnb_ref[...]
        o_ref[rows, :] = _silu(y).astype(o_ref.dtype)
        return carry

    lax.fori_loop(0, nc, step, 0)


def _confconv(group, ga, gg, w, b, lng, lnb, o_prev, layer_name):
    prompt = group == "prompt"
    length = SEQ if prompt else DEC_SEQ
    nb = BATCH if prompt else DEC_BATCH
    off = 0 if prompt else N_PROMPT_TOK // DEC_SEQ
    in_specs = _seq_specs(group, (CONV_WIDTH, CONV_WIDTH))
    in_specs += [_const_spec(a.shape) for a in (w, b, lng, lnb)]
    args = [ga, gg, w, b, lng, lnb]
    aliases = {}
    if not prompt:
        in_specs.append(pl.BlockSpec(memory_space=pl.ANY))
        args.append(o_prev)
        aliases = {len(args) - 1: 0}
    halo = 2 * SUBLANES
    return pl.pallas_call(
        functools.partial(_confconv_kernel, length=length, aliased=not prompt),
        grid=(nb,),
        in_specs=in_specs,
        out_specs=pl.BlockSpec((length, CONV_WIDTH), lambda b: (b + off, 0)),
        out_shape=jax.ShapeDtypeStruct((N_TOK, CONV_WIDTH), BF16),
        scratch_shapes=[pltpu.VMEM((length + 2 * halo, CONV_WIDTH), F32),
                        pltpu.VMEM((128, CONV_WIDTH), F32)],
        input_output_aliases=aliases,
        compiler_params=_cparams(1),
        name=f"confconv_{group}_{layer_name}",
    )(*args)


def _rope(x, cos_t, sin_t):
    lane = lax.broadcasted_iota(jnp.int32, x.shape, 1)
    quarter = ROPE_AXIS_DIM // 2
    partner = jnp.where((lane & quarter) == 0,
                        pltpu.roll(x, HEAD_DIM - quarter, 1), pltpu.roll(x, quarter, 1))
    return x * cos_t + partner * sin_t


def _attn_kernel(*refs, length, n_ctx, use_rope, aliased):
    it = iter(refs)
    q_ref, k_ref, v_ref, qg_ref, kg_ref = (next(it) for _ in range(5))
    if n_ctx:
        ck_ref, cv_ref = next(it), next(it)
    if use_rope:
        cos_ref, sin_ref = next(it), next(it)
    if aliased:
        next(it)
    o_ref = next(it)
    kn_ref = None if aliased else next(it)
    kall_s, vall_s = next(it), next(it)

    qb = 128
    nq = length // qb
    rep = N_HEADS // N_KV_HEADS
    scale = HEAD_DIM ** -0.5

    for g in range(N_KV_HEADS):
        gs = slice(g * HEAD_DIM, (g + 1) * HEAD_DIM)
        kn = _rms(k_ref[:, gs], kg_ref[...])
        if kn_ref is not None:
            kn_ref[:, gs] = kn
        if use_rope:
            kn = _rope(kn, cos_ref[...], sin_ref[...])
        if n_ctx:
            kall_s[g, 0:n_ctx, :] = ck_ref[:, gs].astype(BF16)
            vall_s[g, 0:n_ctx, :] = cv_ref[:, gs].astype(BF16)
        kall_s[g, n_ctx:n_ctx + length, :] = kn.astype(BF16)
        vall_s[g, n_ctx:n_ctx + length, :] = v_ref[:, gs].astype(BF16)

    def qblock(i, carry):
        base = pl.multiple_of(i * qb, qb)
        rows = pl.ds(base, qb)
        for g in range(N_KV_HEADS):
            qs = []
            for r in range(rep):
                hs = slice((g * rep + r) * HEAD_DIM, (g * rep + r + 1) * HEAD_DIM)
                qn = _rms(q_ref[rows, hs], qg_ref[...])
                if use_rope:
                    qn = _rope(qn, cos_ref[rows, :], sin_ref[rows, :])
                qs.append(qn.astype(BF16))
            qst = jnp.concatenate(qs, axis=0)
            s = _dot_nt(qst, kall_s[g])
            m = jnp.max(s, axis=-1, keepdims=True)
            p = jnp.exp((s - m) * scale)
            o = _dot(p.astype(BF16), vall_s[g]) / jnp.sum(p, axis=-1, keepdims=True)
            for r in range(rep):
                hs = slice((g * rep + r) * HEAD_DIM, (g * rep + r + 1) * HEAD_DIM)
                o_ref[rows, hs] = o[r * qb:(r + 1) * qb].astype(o_ref.dtype)
        return carry

    lax.fori_loop(0, nq, qblock, 0)


def _attn(group, q, k, v, qg, kg, ck, cv, cos_t, sin_t, o_prev, layer_name):
    prompt = group == "prompt"
    length = SEQ if prompt else DEC_SEQ
    nb = BATCH if prompt else DEC_BATCH
    off = 0 if prompt else N_PROMPT_TOK // DEC_SEQ
    kvw = N_KV_HEADS * HEAD_DIM
    n_ctx = 0 if prompt else PAST_LEN
    in_specs = _seq_specs(group, (D_MODEL, kvw, kvw))
    in_specs += [_const_spec(qg.shape), _const_spec(kg.shape)]
    args = [q, k, v, qg, kg]
    o_spec = pl.BlockSpec((length, D_MODEL), lambda b: (b + off, 0))
    o_shape = jax.ShapeDtypeStruct((N_TOK, D_MODEL), BF16)
    if prompt:
        out_specs = [o_spec, pl.BlockSpec((length, kvw), lambda b: (b, 0))]
        out_shape = [o_shape, jax.ShapeDtypeStruct((N_PROMPT_TOK, kvw), F32)]
        aliases = {}
    else:
        in_specs += [pl.BlockSpec((None, n_ctx, kvw), lambda b: (b, 0, 0)),
                     pl.BlockSpec((None, n_ctx, kvw), lambda b: (b, 0, 0)),
                     _const_spec(cos_t.shape), _const_spec(sin_t.shape),
                     pl.BlockSpec(memory_space=pl.ANY)]
        args += [ck, cv, cos_t, sin_t, o_prev]
        out_specs = [o_spec]
        out_shape = [o_shape]
        aliases = {len(args) - 1: 0}
    return pl.pallas_call(
        functools.partial(_attn_kernel, length=length, n_ctx=n_ctx, use_rope=not prompt,
                          aliased=not prompt),
        grid=(nb,),
        in_specs=in_specs,
        out_specs=out_specs,
        out_shape=out_shape,
        scratch_shapes=[pltpu.VMEM((N_KV_HEADS, n_ctx + length, HEAD_DIM), BF16),
                        pltpu.VMEM((N_KV_HEADS, n_ctx + length, HEAD_DIM), BF16)],
        input_output_aliases=aliases,
        compiler_params=_cparams(1),
        name=f"attn_{group}_{layer_name}",
    )(*args)


def _rope_tables(rows):
    row_pos = jnp.repeat(jnp.arange(rows, dtype=F32), GRID_W)
    col_pos = jnp.tile(jnp.arange(GRID_W, dtype=F32), rows)
    inv_freq = jnp.power(ROPE_THETA, -jnp.arange(0, ROPE_AXIS_DIM, 2, dtype=F32) / ROPE_AXIS_DIM)
    ang_r = row_pos[:, None] * inv_freq
    ang_c = col_pos[:, None] * inv_freq
    cos_t = jnp.concatenate([jnp.cos(ang_r), jnp.cos(ang_r), jnp.cos(ang_c), jnp.cos(ang_c)], axis=1)
    sin_t = jnp.concatenate([-jnp.sin(ang_r), jnp.sin(ang_r), -jnp.sin(ang_c), jnp.sin(ang_c)], axis=1)
    return cos_t, sin_t


def _block_diag_tiles(w):
    per = MXU_DIM // LRU_BW
    n_tiles = LRU_BLOCKS // per
    w = w.reshape(2, n_tiles, per, LRU_BW, LRU_BW)
    eye = jnp.eye(per, dtype=w.dtype)
    tiles = jnp.einsum('dtpio,pq->dtpiqo', w, eye)
    return tiles.reshape(2, n_tiles, MXU_DIM, MXU_DIM).astype(BF16)


def _row(v):
    return v.reshape(1, -1)


def kernel(x_prompt, x_sample, state_ssd, state_lru, cache_k, cache_v, c, c_ctx,
           w_mod, b_mod, norm_g, w_in_ssm, ssd_conv_w, ssd_conv_b, ssd_a_log, ssd_dt_bias,
           ssd_d, ssd_norm_w, lru_conv_w, lru_conv_b, lru_wa, lru_ba, lru_wx, lru_bx,
           lru_lambda, w_out_ssm, w_in_ca, conf_dw_w, conf_dw_b, conf_ln_g, conf_ln_b,
           q_norm_g, k_norm_g, w_out_ca, ffn_w_in, ffn_conv_w, ffn_conv_b, ffn_w_out):
    x = jnp.concatenate([x_prompt.reshape(N_PROMPT_TOK, D_MODEL),
                         x_sample.reshape(N_SAMPLE_TOK, D_MODEL)], axis=0)
    cvec = jnp.concatenate(
        [c_ctx[None], c, jnp.zeros((N_MOD_ROWS - 1 - DEC_BATCH, D_MODEL), F32)], axis=0)
    mods = _modulation_all(cvec, w_mod, b_mod)
    cos_t, sin_t = _rope_tables(DEC_SEQ // GRID_W)
    kvw = N_KV_HEADS * HEAD_DIM

    ssd_states, lru_states, k_out, v_out = [], [], [], []
    for layer in range(DEPTH):
        i = layer // 2
        m = mods[layer]
        name = f"l{layer}"
        g0, g1, g2, g3 = (_row(norm_g[layer, j]) for j in range(4))
        if layer % 2 == 0:
            o2 = SSD_INNER + SSD_CONV_DIM
            o3 = o2 + 2 * SSD_HEADS
            pad32 = DT_PAD - 2 * SSD_HEADS
            w_in = jnp.concatenate(
                [w_in_ssm[i][:, :o2], w_in_ssm[i][:, o3:], w_in_ssm[i][:, o2:o3],
                 jnp.zeros((D_MODEL, pad32), F32)], axis=1).astype(BF16)
            z, xbc, xl, gl, dt = _inproj(
                x, g0, m[0], m[1], w_in,
                (SSD_INNER, SSD_CONV_DIM, LRU_WIDTH, LRU_WIDTH, DT_PAD), f"inproj_ssm_{name}")
            dtb = jnp.pad(ssd_dt_bias[i].reshape(1, -1), ((0, 0), (0, pad32)))
            alog = jnp.pad(ssd_a_log[i].reshape(1, -1), ((0, 0), (0, pad32)))
            dskip = _row(jnp.repeat(ssd_d[i], SSD_HEADDIM))
            ssd_args = (ssd_conv_w[i], _row(ssd_conv_b[i]), dtb, alog, dskip, _row(ssd_norm_w[i]))
            y, s_p = _ssd("prompt", z, xbc, dt, *ssd_args, None, None, name)
            (y,) = _ssd("sample", z, xbc, dt, *ssd_args,
                        state_ssd[:, i].reshape(DEC_BATCH, 2, SSD_INNER, SSD_STATE), y, name)
            lru_args = (lru_conv_w[i], _row(lru_conv_b[i]), _block_diag_tiles(lru_wa[i]),
                        _block_diag_tiles(lru_wx[i]), lru_ba[i], lru_bx[i], lru_lambda[i])
            yl, l_p = _lru("prompt", xl, gl, *lru_args, None, None, name)
            (yl,) = _lru("sample", xl, gl, *lru_args, state_lru[:, i], yl, name)
            ssd_states.append(s_p.reshape(BATCH, 2, SSD_HEADS, SSD_HEADDIM, SSD_STATE))
            lru_states.append(l_p)
            x = _outproj(y, yl, w_out_ssm[i].astype(BF16), x, g1, m[2], f"outproj_ssm_{name}")
        else:
            ga, gg, q, k, v = _inproj(
                x, g0, m[0], m[1], w_in_ca[i].astype(BF16),
                (CONV_WIDTH, CONV_WIDTH, N_HEADS * HEAD_DIM, kvw, kvw), f"inproj_ca_{name}")
            conv_args = (conf_dw_w[i], _row(conf_dw_b[i]), _row(conf_ln_g[i]), _row(conf_ln_b[i]))
            cvo = _confconv("prompt", ga, gg, *conv_args, None, name)
            cvo = _confconv("sample", ga, gg, *conv_args, cvo, name)
            qg, kg = _row(q_norm_g[i]), _row(k_norm_g[i])
            o, kn = _attn("prompt", q, k, v, qg, kg, None, None, None, None, None, name)
            (o,) = _attn("sample", q, k, v, qg, kg,
                         cache_k[:, i].reshape(DEC_BATCH, PAST_LEN, kvw),
                         cache_v[:, i].reshape(DEC_BATCH, PAST_LEN, kvw),
                         cos_t, sin_t, o, name)
            k_out.append(kn.reshape(BATCH, SEQ, N_KV_HEADS, HEAD_DIM))
            v_out.append(v[:N_PROMPT_TOK].reshape(BATCH, SEQ, N_KV_HEADS, HEAD_DIM))
            x = _outproj(cvo, o, w_out_ca[i].astype(BF16), x, g1, m[2], f"outproj_ca_{name}")
        x = _ffn(x, g2, m[3], m[4], ffn_w_in[layer].astype(BF16), ffn_conv_w[layer],
                 _row(ffn_conv_b[layer]), ffn_w_out[layer].astype(BF16), g3, m[5],
                 f"ffn_{name}")

    xp = x[:N_PROMPT_TOK].reshape(BATCH, SEQ, D_MODEL)
    xs = x[N_PROMPT_TOK:].reshape(DEC_BATCH, DEC_SEQ, D_MODEL)
    return (xp, xs, jnp.stack(ssd_states, axis=1), jnp.stack(lru_states, axis=1),
            jnp.stack(k_out, axis=1), jnp.stack(v_out, axis=1))
```

```python
import functools
import math

import jax
import jax.numpy as jnp
from jax import lax
from jax.experimental import pallas as pl
from jax.experimental.pallas import tpu as pltpu

F32 = jnp.float32
BF16 = jnp.bfloat16

D_MODEL = 1024
BATCH = 16
SEQ = 256
DEPTH = 4
N_SSM_LAYERS = (DEPTH + 1) // 2
N_ATTN_LAYERS = DEPTH // 2
DEC_BATCH = 4
DEC_SEQ = 1024
PAST_LEN = 512
GRID_W = 64
EPS = 1e-6
SSD_HEADDIM = 64
SSD_INNER = D_MODEL
SSD_HEADS = SSD_INNER // SSD_HEADDIM
SSD_GROUPS = 2
SSD_STATE = 128
SSD_CONV = 4
SSD_CHUNK = 128
SSD_CONV_DIM = SSD_INNER + 2 * SSD_GROUPS * SSD_STATE
LRU_WIDTH = D_MODEL
LRU_BW = 64
LRU_BLOCKS = LRU_WIDTH // LRU_BW
LRU_CONV = 4
LRU_C = 8.0
CONV_WIDTH = D_MODEL
CONF_K = 31
HEAD_DIM = 128
N_HEADS = D_MODEL // HEAD_DIM
N_KV_HEADS = 2
ROPE_THETA = 10000.0
ROPE_AXIS_DIM = HEAD_DIM // 2
D_FF = 2816
FFN_CONV = 3

N_PROMPT_TOK = BATCH * SEQ
N_SAMPLE_TOK = DEC_BATCH * DEC_SEQ
N_TOK = N_PROMPT_TOK + N_SAMPLE_TOK
N_MOD_ROWS = 8
LANES = 128
SUBLANES = 8
MXU_DIM = 256
DT_PAD = LANES
SSM_O2 = SSD_INNER + SSD_CONV_DIM
SSM_O3 = SSM_O2 + 2 * SSD_HEADS
VMEM_LIMIT = 58 * 1024 * 1024

TM_LINEAR = 512
TM_FFN = 1024


def _cparams(n_axes):
    return pltpu.CompilerParams(
        dimension_semantics=("arbitrary",) * n_axes,
        vmem_limit_bytes=VMEM_LIMIT)


def _const_spec(shape):
    nd = len(shape)
    return pl.BlockSpec(shape, lambda *_: (0,) * nd, pipeline_mode=pl.Buffered(1))


def _layer_spec(shape, layer):
    nd = len(shape) - 1
    return pl.BlockSpec((None,) + tuple(shape[1:]), lambda *_: (layer,) + (0,) * nd,
                        pipeline_mode=pl.Buffered(1))


def _mod_row(i, tm):
    start = i * tm
    return jnp.where(start < N_PROMPT_TOK, 0, 1 + (start - N_PROMPT_TOK) // DEC_SEQ)


def _mod_spec(tm):
    return pl.BlockSpec((None, 1, D_MODEL), lambda i: (_mod_row(i, tm), 0, 0))


def _sigmoid(x):
    return jax.nn.sigmoid(x)


def _silu(x):
    return x * _sigmoid(x)


def _softplus(x):
    return jnp.maximum(x, 0.0) + jnp.log1p(jnp.exp(-jnp.abs(x)))


def _gelu_tanh(x):
    return 0.5 * x * (1.0 + jnp.tanh(math.sqrt(2.0 / math.pi) * (x + 0.044715 * (x * x * x))))


def _rms(x, g):
    ms = jnp.mean(x * x, axis=-1, keepdims=True)
    return (x * lax.rsqrt(ms + EPS)) * g


def _dot(a, b):
    return jnp.dot(a, b, preferred_element_type=F32)


def _dot_nt(a, b):
    return lax.dot_general(a, b, (((1,), (1,)), ((), ())), preferred_element_type=F32)


def _mod_kernel(c_ref, w_ref, b_ref, o_ref):
    c = c_ref[...]
    s = _silu(c).astype(BF16)
    o_ref[...] = _dot(s, w_ref[...].astype(BF16)) + b_ref[...]


def _modulation_all(cvec, w_mod, b_mod):
    tn = 1536
    n_out = 6 * D_MODEL
    out = pl.pallas_call(
        _mod_kernel,
        grid=(DEPTH, n_out // tn),
        in_specs=[
            pl.BlockSpec((N_MOD_ROWS, D_MODEL), lambda l, j: (0, 0)),
            pl.BlockSpec((None, D_MODEL, tn), lambda l, j: (l, 0, j)),
            pl.BlockSpec((None, 1, tn), lambda l, j: (l, 0, j)),
        ],
        out_specs=pl.BlockSpec((None, N_MOD_ROWS, tn), lambda l, j: (l, 0, j)),
        out_shape=jax.ShapeDtypeStruct((DEPTH, N_MOD_ROWS, n_out), F32),
        compiler_params=_cparams(2),
        name="modulation",
    )(cvec, w_mod, b_mod.reshape(DEPTH, 1, n_out))
    out = out.reshape(DEPTH, N_MOD_ROWS, 6, 1, D_MODEL)
    return jnp.transpose(out, (0, 2, 1, 3, 4))


def _inproj_kernel(x_ref, g_ref, shift_ref, scale_ref, w_ref, *rest, widths, chunk):
    o_refs = rest[:len(widths)]
    h_ref = rest[len(widths)]
    h = _rms(x_ref[...], g_ref[...]) * (1.0 + scale_ref[...]) + shift_ref[...]
    h_ref[...] = h.astype(BF16)
    off = 0
    for o_ref, n in zip(o_refs, widths):
        for c0 in range(0, n, chunk):
            c1 = min(c0 + chunk, n)
            o_ref[:, c0:c1] = _dot(h_ref[...], w_ref[:, off + c0:off + c1])
        off += n


def _inproj(x, g, shift, scale, w, layer, widths, name):
    tm = TM_LINEAR
    assert sum(widths) == w.shape[2] and all(n % LANES == 0 for n in widths)
    row_spec = pl.BlockSpec((tm, D_MODEL), lambda i: (i, 0))
    return pl.pallas_call(
        functools.partial(_inproj_kernel, widths=tuple(widths), chunk=512),
        grid=(N_TOK // tm,),
        in_specs=[row_spec, _const_spec((1, D_MODEL)), _mod_spec(tm), _mod_spec(tm),
                  _layer_spec(w.shape, layer)],
        out_specs=[pl.BlockSpec((tm, n), lambda i: (i, 0)) for n in widths],
        out_shape=[jax.ShapeDtypeStruct((N_TOK, n), F32) for n in widths],
        scratch_shapes=[pltpu.VMEM((tm, D_MODEL), BF16)],
        compiler_params=_cparams(1),
        name=name,
    )(x, g, shift, scale, w)


def _outproj_kernel(a_ref, b_ref, w_ref, x_ref, g_ref, gate_ref, o_ref):
    ka = a_ref.shape[1]
    acc = _dot(a_ref[...], w_ref[0:ka, :]) + _dot(b_ref[...], w_ref[ka:, :])
    o_ref[...] = x_ref[...] + gate_ref[...] * _rms(acc, g_ref[...])


def _outproj(a, b, w, layer, x, g, gate, name):
    tm = TM_LINEAR
    return pl.pallas_call(
        _outproj_kernel,
        grid=(N_TOK // tm,),
        in_specs=[
            pl.BlockSpec((tm, a.shape[1]), lambda i: (i, 0)),
            pl.BlockSpec((tm, b.shape[1]), lambda i: (i, 0)),
            _layer_spec(w.shape, layer),
            pl.BlockSpec((tm, D_MODEL), lambda i: (i, 0)),
            _const_spec((1, D_MODEL)),
            _mod_spec(tm),
        ],
        out_specs=pl.BlockSpec((tm, D_MODEL), lambda i: (i, 0)),
        out_shape=jax.ShapeDtypeStruct((N_TOK, D_MODEL), F32),
        compiler_params=_cparams(1),
        name=name,
    )(a, b, w, x, g, gate)


def _ffn_kernel(x_ref, g2_ref, shift_ref, scale_ref, wi_ref, cw_ref, cb_ref,
                wo_ref, g3_ref, gate_ref, o_ref, h_ref, act_ref, *, chunk, row_blk):
    tm = x_ref.shape[0]
    i = pl.program_id(0)
    h = _rms(x_ref[...], g2_ref[...]) * (1.0 + scale_ref[...]) + shift_ref[...]
    h_ref[...] = h.astype(BF16)
    lseq = jnp.where(i * tm < N_PROMPT_TOK, SEQ, DEC_SEQ)
    pos = lax.broadcasted_iota(jnp.int32, (tm, 1), 0) & (lseq - 1)
    first = pos == 0
    last = pos == lseq - 1
    for c0 in range(0, D_FF, chunk):
        c1 = min(c0 + chunk, D_FF)
        gt = _dot(h_ref[...], wi_ref[:, c0:c1])
        vl = _dot(h_ref[...], wi_ref[:, D_FF + c0:D_FF + c1])
        g_prev = jnp.where(first, 0.0, pltpu.roll(gt, 1, 0))
        g_next = jnp.where(last, 0.0, pltpu.roll(gt, tm - 1, 0))
        conv = (g_prev * cw_ref[0:1, c0:c1] + gt * cw_ref[1:2, c0:c1]
                + g_next * cw_ref[2:3, c0:c1] + cb_ref[:, c0:c1])
        act_ref[:, c0:c1] = (_silu(conv) * vl).astype(BF16)
    for r0 in range(0, tm, row_blk):
        rs = slice(r0, r0 + row_blk)
        acc = _dot(act_ref[rs, :], wo_ref[...])
        o_ref[rs, :] = x_ref[rs, :] + gate_ref[...] * _rms(acc, g3_ref[...])


def _ffn(x, g2, shift, scale, wi, cw, cb, wo, layer, g3, gate, name):
    tm = TM_FFN
    row_spec = pl.BlockSpec((tm, D_MODEL), lambda i: (i, 0))
    return pl.pallas_call(
        functools.partial(_ffn_kernel, chunk=2 * MXU_DIM, row_blk=MXU_DIM),
        grid=(N_TOK // tm,),
        in_specs=[
            row_spec, _const_spec((1, D_MODEL)), _mod_spec(tm), _mod_spec(tm),
            _layer_spec(wi.shape, layer), _const_spec(cw.shape),
            _const_spec(cb.shape), _layer_spec(wo.shape, layer), _const_spec((1, D_MODEL)),
            _mod_spec(tm),
        ],
        out_specs=row_spec,
        out_shape=jax.ShapeDtypeStruct((N_TOK, D_MODEL), F32),
        scratch_shapes=[pltpu.VMEM((tm, D_MODEL), BF16), pltpu.VMEM((tm, D_FF), BF16)],
        compiler_params=_cparams(1),
        name=name,
    )(x, g2, shift, scale, wi, cw, cb, wo, g3, gate)


def _seq_specs(group, widths):
    if group == "prompt":
        return [pl.BlockSpec((SEQ, w), lambda b: (b, 0)) for w in widths]
    off = N_PROMPT_TOK // DEC_SEQ
    return [pl.BlockSpec((DEC_SEQ, w), lambda b: (b + off, 0)) for w in widths]


def _short_conv_chunk(pad_ref, cw_ref, cb_ref, base, t, halo, taps, left, cols):
    win = pad_ref[pl.ds(base, t + 2 * halo), cols]
    n = t + 2 * halo
    acc = cb_ref[:, cols]
    for j in range(taps):
        s = (left - j) % n
        rolled = win if s == 0 else pltpu.roll(win, s, 0)
        acc = acc + rolled[halo:halo + t] * cw_ref[j:j + 1, cols]
    return acc


def _fill_padded(pad_ref, src, length, halo):
    width = pad_ref.shape[1]
    pad_ref[0:halo, :] = jnp.zeros((halo, width), F32)
    pad_ref[halo + length:2 * halo + length, :] = jnp.zeros((halo, width), F32)
    pad_ref[halo:halo + length, :] = src


def _lane_pairs(m, first_col, n_pairs, rows):
    lane = lax.broadcasted_iota(jnp.int32, (rows, LANES), 1)
    lo_half = lane < SSD_HEADDIM
    pieces = []
    for k in range(n_pairs):
        c = first_col + 2 * k
        lo = jnp.broadcast_to(m[:, c:c + 1], (rows, LANES))
        hi = jnp.broadcast_to(m[:, c + 1:c + 2], (rows, LANES))
        pieces.append(jnp.where(lo_half, lo, hi))
    return jnp.concatenate(pieces, axis=1)


def _ssd_kernel(*refs, length, has_h0, has_state_out, n_alias):
    it = iter(refs)
    z_ref, xbc_ref, dt_ref = next(it), next(it), next(it)
    cw_ref, cb_ref, dtb_ref, alog_ref, dskip_ref, nw_ref = (next(it) for _ in range(6))
    h0_ref = next(it) if has_h0 else None
    for _ in range(n_alias):
        next(it)
    y_ref = next(it)
    st_ref = next(it) if has_state_out else None
    pad_s, xs_s, bc_s, cum_s, cum_t_s, dt_t_s, w_t_s, cd_s, yacc_s, state_s = (
        next(it) for _ in range(10))

    t = SSD_CHUNK
    nc = length // t
    halo = SUBLANES
    gw = SSD_INNER // SSD_GROUPS
    pairs_per_group = SSD_HEADS // SSD_GROUPS // 2

    _fill_padded(pad_s, xbc_ref[...], length, halo)
    a_row = -jnp.exp(alog_ref[...])

    ri = lax.broadcasted_iota(jnp.int32, (t, t), 0)
    ci = lax.broadcasted_iota(jnp.int32, (t, t), 1)
    keep = (ci <= ri, ci >= ri)
    tril = keep[0].astype(F32)
    lane = lax.broadcasted_iota(jnp.int32, (t, LANES), 1)
    lo_half = lane < SSD_HEADDIM
    fwd_cols = lane < SSD_HEADS
    fwd_rows = ri < SSD_HEADS

    def prep(c, carry):
        base = pl.multiple_of(c * t, t)
        rows = pl.ds(base, t)
        for c0 in range(0, SSD_CONV_DIM, MXU_DIM):
            cols = slice(c0, c0 + MXU_DIM)
            conv = _silu(_short_conv_chunk(pad_s, cw_ref, cb_ref, base, t, halo, SSD_CONV, 2, cols))
            if c0 < SSD_INNER:
                xs_s[rows, cols] = conv
                yacc_s[rows, cols] = conv * dskip_ref[:, cols]
            else:
                bc_s[rows, c0 - SSD_INNER:c0 - SSD_INNER + MXU_DIM] = conv
        dtsp = _softplus(dt_ref[rows, :] + dtb_ref[...])
        a_c = dtsp * a_row
        pre = jnp.dot(tril, a_c, preferred_element_type=F32, precision=lax.Precision.HIGHEST)
        suf = pre[t - 1:t, :] - pre + a_c
        cum = jnp.where(fwd_cols, pre, suf)
        cum_s[rows, :] = cum
        cum_t = cum.T
        dt_t = dtsp.T
        edge_col = jnp.where(fwd_rows[:, 0:1], cum_t[:, t - 1:t], cum_t[:, 0:1])
        cum_t_s[c] = cum_t
        dt_t_s[c] = dt_t
        w_t_s[c] = dt_t * jnp.exp(edge_col - cum_t)
        edge_row = jnp.where(fwd_cols[0:1, :], cum[t - 1:t, :], cum[0:1, :])
        cd_s[c] = jnp.broadcast_to(jnp.exp(edge_row), (SUBLANES, LANES))
        return carry

    lax.fori_loop(0, nc, prep, 0)

    for d in range(2):
        if has_h0:
            for k in range(SSD_INNER // LANES):
                ks = slice(k * LANES, (k + 1) * LANES)
                state_s[d, :, ks] = h0_ref[d, ks, :].T
        else:
            state_s[d] = jnp.zeros((SSD_STATE, SSD_INNER), F32)

    def block_diag(m):
        return jnp.concatenate([jnp.where(lo_half, m, 0.0).astype(BF16),
                                jnp.where(lo_half, 0.0, m).astype(BF16)], axis=0)

    def chunk_step(c, carry):
        for d in range(2):
            cidx = c if d == 0 else nc - 1 - c
            base = pl.multiple_of(cidx * t, t)
            rows = pl.ds(base, t)
            cum = cum_s[rows, :]
            bc = bc_s[rows, :]
            cd = cd_s[cidx][0:1, :]
            for g in range(SSD_GROUPS):
                col0 = d * SSD_HEADS + g * 2 * pairs_per_group
                b_g = bc[:, g * SSD_STATE:(g + 1) * SSD_STATE]
                c_g = bc[:, (SSD_GROUPS + g) * SSD_STATE:(SSD_GROUPS + g + 1) * SSD_STATE]
                gmat = _dot_nt(c_g.astype(BF16), b_g.astype(BF16))
                b_t = b_g.T
                cd_rep = _lane_pairs(cd, col0, pairs_per_group, 1)
                for kk in range(pairs_per_group):
                    lanes = slice(g * gw + kk * LANES, g * gw + (kk + 1) * LANES)
                    rhs_x = block_diag(xs_s[rows, lanes])
                    st = state_s[d, :, lanes]
                    gl, ce, bw = [], [], []
                    for hcur in (col0 + 2 * kk, col0 + 2 * kk + 1):
                        hrow = pl.ds(hcur, 1)
                        colb = jnp.broadcast_to(cum[:, hcur:hcur + 1], (t, t))
                        rowb = jnp.broadcast_to(cum_t_s[cidx, hrow, :], (t, t))
                        dtrow = jnp.broadcast_to(dt_t_s[cidx, hrow, :], (t, t))
                        wrow = jnp.broadcast_to(w_t_s[cidx, hrow, :], (t, t))
                        lmat = jnp.exp(jnp.where(keep[d], colb - rowb, -jnp.inf))
                        gl.append((gmat * lmat * dtrow).astype(BF16))
                        ce.append((c_g * jnp.exp(colb)).astype(BF16))
                        bw.append((b_t * wrow).astype(BF16))
                    y_p = _dot(jnp.concatenate(gl + ce, axis=1),
                               jnp.concatenate([rhs_x, block_diag(st)], axis=0))
                    state_s[d, :, lanes] = (st * cd_rep[:, kk * LANES:(kk + 1) * LANES]
                                            + _dot(jnp.concatenate(bw, axis=1), rhs_x))
                    yacc_s[rows, lanes] += y_p
        return carry

    lax.fori_loop(0, nc, chunk_step, 0)

    def finish(c, carry):
        base = pl.multiple_of(c * t, t)
        rows = pl.ds(base, t)
        y = yacc_s[rows, :] * _silu(z_ref[rows, :])
        y_ref[rows, :] = _rms(y, nw_ref[...]).astype(y_ref.dtype)
        return carry

    lax.fori_loop(0, nc, finish, 0)

    if has_state_out:
        for d in range(2):
            for k in range(SSD_INNER // LANES):
                ks = slice(k * LANES, (k + 1) * LANES)
                st_ref[d, ks, :] = state_s[d, :, ks].T


def _ssd(group, z, xbc, dt, cw, cb, dtb, alog, dskip, nw, i, h0_all, y_prev, st_prev, layer_name):
    prompt = group == "prompt"
    length = SEQ if prompt else DEC_SEQ
    nb = BATCH if prompt else DEC_BATCH
    off = 0 if prompt else N_PROMPT_TOK // DEC_SEQ
    in_specs = _seq_specs(group, (SSD_INNER, SSD_CONV_DIM, DT_PAD))
    in_specs += [_const_spec(a.shape) for a in (cw, cb, dtb, alog, dskip, nw)]
    args = [z, xbc, dt, cw, cb, dtb, alog, dskip, nw]
    y_spec = pl.BlockSpec((length, SSD_INNER), lambda b: (b + off, 0))
    y_shape = jax.ShapeDtypeStruct((N_TOK, SSD_INNER), BF16)
    st_block = (None, None, 2, SSD_INNER, SSD_STATE)
    aliases = {}
    if prompt:
        out_specs = [y_spec, pl.BlockSpec(st_block, lambda b: (b, i, 0, 0, 0))]
        out_shape = [y_shape, jax.ShapeDtypeStruct((nb, N_SSM_LAYERS, 2, SSD_INNER, SSD_STATE), F32)]
        if st_prev is not None:
            in_specs.append(pl.BlockSpec(memory_space=pl.ANY))
            args.append(st_prev)
            aliases = {len(args) - 1: 1}
    else:
        in_specs += [pl.BlockSpec(st_block, lambda b: (b, i, 0, 0, 0)),
                     pl.BlockSpec(memory_space=pl.ANY)]
        args += [h0_all, y_prev]
        out_specs = [y_spec]
        out_shape = [y_shape]
        aliases = {len(args) - 1: 0}
    halo = SUBLANES
    nc = length // SSD_CHUNK
    scratch = [
        pltpu.VMEM((length + 2 * halo, SSD_CONV_DIM), F32),
        pltpu.VMEM((length, SSD_INNER), F32),
        pltpu.VMEM((length, 2 * SSD_GROUPS * SSD_STATE), F32),
        pltpu.VMEM((length, DT_PAD), F32),
        pltpu.VMEM((nc, DT_PAD, SSD_CHUNK), F32),
        pltpu.VMEM((nc, DT_PAD, SSD_CHUNK), F32),
        pltpu.VMEM((nc, DT_PAD, SSD_CHUNK), F32),
        pltpu.VMEM((nc, SUBLANES, DT_PAD), F32),
        pltpu.VMEM((length, SSD_INNER), F32),
        pltpu.VMEM((2, SSD_STATE, SSD_INNER), F32),
    ]
    return pl.pallas_call(
        functools.partial(_ssd_kernel, length=length, has_h0=not prompt, has_state_out=prompt,
                          n_alias=len(aliases)),
        grid=(nb,),
        in_specs=in_specs,
        out_specs=out_specs,
        out_shape=out_shape,
        scratch_shapes=scratch,
        input_output_aliases=aliases,
        compiler_params=_cparams(1),
        name=f"ssd_{group}_{layer_name}",
    )(*args)


def _lru_kernel(*refs, length, has_h0, has_state_out, n_alias):
    it = iter(refs)
    xl_ref, gl_ref = next(it), next(it)
    cw_ref, cb_ref, wa_ref, wx_ref, ba_ref, bx_ref, lam_ref = (next(it) for _ in range(7))
    h0_ref = next(it) if has_h0 else None
    for _ in range(n_alias):
        next(it)
    o_ref = next(it)
    st_ref = next(it) if has_state_out else None
    pad_s, xc_s, a_s, u_s, h_s = (next(it) for _ in range(5))

    t = 128
    nc = length // t
    halo = SUBLANES
    n_tiles = LRU_WIDTH // MXU_DIM

    _fill_padded(pad_s, xl_ref[...], length, halo)

    def prep(c, carry):
        base = pl.multiple_of(c * t, t)
        for c0 in range(0, LRU_WIDTH, MXU_DIM):
            cols = slice(c0, c0 + MXU_DIM)
            xc_s[pl.ds(base, t), cols] = _short_conv_chunk(
                pad_s, cw_ref, cb_ref, base, t, halo, LRU_CONV, 2, cols)
        return carry

    lax.fori_loop(0, nc, prep, 0)

    row8 = lax.broadcasted_iota(jnp.int32, (SUBLANES, LRU_WIDTH), 0)
    n_groups = length // SUBLANES

    for d in range(2):
        sp = _softplus(-lam_ref[d:d + 1, :])

        def gates(c, carry, d=d, sp=sp):
            base = pl.multiple_of(c * t, t)
            rows = pl.ds(base, t)
            xc = xc_s[rows, :]
            xb = xc.astype(BF16)
            ra, ri = [], []
            for j in range(n_tiles):
                js = slice(j * MXU_DIM, (j + 1) * MXU_DIM)
                ra.append(_dot(xb[:, js], wa_ref[d, j]))
                ri.append(_dot(xb[:, js], wx_ref[d, j]))
            r = _sigmoid(jnp.concatenate(ra, axis=1) + ba_ref[d:d + 1, :])
            gi = _sigmoid(jnp.concatenate(ri, axis=1) + bx_ref[d:d + 1, :])
            log_a = (-LRU_C) * r * sp
            a = jnp.exp(log_a)
            a_s[rows, :] = a
            u_s[rows, :] = jnp.sqrt(-jnp.tanh(log_a) * (a * a + 1.0)) * gi * xc
            return carry

        lax.fori_loop(0, nc, gates, 0)

        if has_h0:
            carry0 = jnp.broadcast_to(h0_ref[d:d + 1, :], (SUBLANES, LRU_WIDTH))
        else:
            carry0 = jnp.zeros((SUBLANES, LRU_WIDTH), F32)

        def scan(gi_, carry, d=d):
            g = gi_ if d == 0 else n_groups - 1 - gi_
            base = pl.multiple_of(g * SUBLANES, SUBLANES)
            rows = pl.ds(base, SUBLANES)
            av = a_s[rows, :]
            uv = u_s[rows, :]
            for k in (1, 2, 4):
                if d == 0:
                    shift, valid = k, row8 >= k
                else:
                    shift, valid = SUBLANES - k, row8 < SUBLANES - k
                a_sh = pltpu.roll(av, shift, 0)
                u_sh = pltpu.roll(uv, shift, 0)
                uv = jnp.where(valid, av * u_sh + uv, uv)
                av = jnp.where(valid, av * a_sh, av)
            h = av * carry + uv
            if d == 0:
                h_s[rows, :] = h
                edge = h[SUBLANES - 1:SUBLANES, :]
            else:
                h_s[rows, :] += h
                edge = h[0:1, :]
            return jnp.broadcast_to(edge, (SUBLANES, LRU_WIDTH))

        final = lax.fori_loop(0, n_groups, scan, carry0, unroll=4)
        if has_state_out:
            st_ref[d:d + 1, :] = final[0:1, :]

    def finish(c, carry):
        base = pl.multiple_of(c * t, t)
        rows = pl.ds(base, t)
        o_ref[rows, :] = (_gelu_tanh(gl_ref[rows, :]) * h_s[rows, :]).astype(o_ref.dtype)
        return carry

    lax.fori_loop(0, nc, finish, 0)


def _lru(group, xl, gl, cw, cb, wa, wx, ba, bx, lam, i, h0_all, o_prev, st_prev, layer_name):
    prompt = group == "prompt"
    length = SEQ if prompt else DEC_SEQ
    nb = BATCH if prompt else DEC_BATCH
    off = 0 if prompt else N_PROMPT_TOK // DEC_SEQ
    in_specs = _seq_specs(group, (LRU_WIDTH, LRU_WIDTH))
    in_specs += [_const_spec(a.shape) for a in (cw, cb, wa, wx, ba, bx, lam)]
    args = [xl, gl, cw, cb, wa, wx, ba, bx, lam]
    o_spec = pl.BlockSpec((length, LRU_WIDTH), lambda b: (b + off, 0))
    o_shape = jax.ShapeDtypeStruct((N_TOK, LRU_WIDTH), BF16)
    st_block = (None, None, 2, LRU_WIDTH)
    aliases = {}
    if prompt:
        out_specs = [o_spec, pl.BlockSpec(st_block, lambda b: (b, i, 0, 0))]
        out_shape = [o_shape, jax.ShapeDtypeStruct((nb, N_SSM_LAYERS, 2, LRU_WIDTH), F32)]
        if st_prev is not None:
            in_specs.append(pl.BlockSpec(memory_space=pl.ANY))
            args.append(st_prev)
            aliases = {len(args) - 1: 1}
    else:
        in_specs += [pl.BlockSpec(st_block, lambda b: (b, i, 0, 0)),
                     pl.BlockSpec(memory_space=pl.ANY)]
        args += [h0_all, o_prev]
        out_specs = [o_spec]
        out_shape = [o_shape]
        aliases = {len(args) - 1: 0}
    halo = SUBLANES
    scratch = [pltpu.VMEM((length + 2 * halo, LRU_WIDTH), F32)]
    scratch += [pltpu.VMEM((length, LRU_WIDTH), F32) for _ in range(4)]
    return pl.pallas_call(
        functools.partial(_lru_kernel, length=length, has_h0=not prompt, has_state_out=prompt,
                          n_alias=len(aliases)),
        grid=(nb,),
        in_specs=in_specs,
        out_specs=out_specs,
        out_shape=out_shape,
        scratch_shapes=scratch,
        input_output_aliases=aliases,
        compiler_params=_cparams(1),
        name=f"lru_{group}_{layer_name}",
    )(*args)


def _confconv_kernel(*refs, length, aliased):
    it = iter(refs)
    a_ref, g_ref, w_ref, b_ref, lng_ref, lnb_ref = (next(it) for _ in range(6))
    if aliased:
        next(it)
    o_ref = next(it)
    pad_s, acc_s = next(it), next(it)

    t = 128
    nc = length // t
    halo = 2 * SUBLANES
    left = (CONF_K - 1) // 2
    cblk = MXU_DIM
    n = t + 2 * halo

    _fill_padded(pad_s, a_ref[...] * _sigmoid(g_ref[...]), length, halo)

    def step(c, carry):
        base = pl.multiple_of(c * t, t)
        rows = pl.ds(base, t)
        for cb0 in range(0, CONV_WIDTH, cblk):
            cs = slice(cb0, cb0 + cblk)
            win = pad_s[pl.ds(base, n), cs]
            acc = jnp.broadcast_to(b_ref[:, cs], (t, cblk))
            for s in range(SUBLANES):
                shifted = win if s == 0 else pltpu.roll(win, n - s, 0)
                for m in range(n // SUBLANES):
                    j = SUBLANES * m + s - halo + left
                    if 0 <= j < CONF_K and SUBLANES * m + t <= n:
                        acc = acc + shifted[SUBLANES * m:SUBLANES * m + t] * w_ref[j:j + 1, cs]
            acc_s[:, cs] = acc
        cv = acc_s[...]
        mu = jnp.mean(cv, axis=-1, keepdims=True)
        xc = cv - mu
        var = jnp.mean(xc * xc, axis=-1, keepdims=True)
        y = (xc * lax.rsqrt(var + EPS)) * lng_ref[...] + lnb_ref[...]
        o_ref[rows, :] = _silu(y).astype(o_ref.dtype)
        return carry

    lax.fori_loop(0, nc, step, 0)


def _confconv(group, ga, gg, w, b, lng, lnb, o_prev, layer_name):
    prompt = group == "prompt"
    length = SEQ if prompt else DEC_SEQ
    nb = BATCH if prompt else DEC_BATCH
    off = 0 if prompt else N_PROMPT_TOK // DEC_SEQ
    in_specs = _seq_specs(group, (CONV_WIDTH, CONV_WIDTH))
    in_specs += [_const_spec(a.shape) for a in (w, b, lng, lnb)]
    args = [ga, gg, w, b, lng, lnb]
    aliases = {}
    if not prompt:
        in_specs.append(pl.BlockSpec(memory_space=pl.ANY))
        args.append(o_prev)
        aliases = {len(args) - 1: 0}
    halo = 2 * SUBLANES
    return pl.pallas_call(
        functools.partial(_confconv_kernel, length=length, aliased=not prompt),
        grid=(nb,),
        in_specs=in_specs,
        out_specs=pl.BlockSpec((length, CONV_WIDTH), lambda b: (b + off, 0)),
        out_shape=jax.ShapeDtypeStruct((N_TOK, CONV_WIDTH), BF16),
        scratch_shapes=[pltpu.VMEM((length + 2 * halo, CONV_WIDTH), F32),
                        pltpu.VMEM((128, CONV_WIDTH), F32)],
        input_output_aliases=aliases,
        compiler_params=_cparams(1),
        name=f"confconv_{group}_{layer_name}",
    )(*args)


def _rope(x, cos_t, sin_t):
    lane = lax.broadcasted_iota(jnp.int32, x.shape, 1)
    quarter = ROPE_AXIS_DIM // 2
    partner = jnp.where((lane & quarter) == 0,
                        pltpu.roll(x, HEAD_DIM - quarter, 1), pltpu.roll(x, quarter, 1))
    return x * cos_t + partner * sin_t


def _attn_kernel(*refs, length, n_ctx, use_rope, has_cache_out, n_alias, qb):
    it = iter(refs)
    q_ref, k_ref, v_ref, qg_ref, kg_ref = (next(it) for _ in range(5))
    if n_ctx:
        ck_ref, cv_ref = next(it), next(it)
    if use_rope:
        cos_ref, sin_ref = next(it), next(it)
    for _ in range(n_alias):
        next(it)
    o_ref = next(it)
    kn_ref, vc_ref = (next(it), next(it)) if has_cache_out else (None, None)
    kall_s, vall_s, s_s = next(it), next(it), next(it)

    nq = length // qb
    rep = N_HEADS // N_KV_HEADS
    scale = HEAD_DIM ** -0.5

    for g in range(N_KV_HEADS):
        gs = slice(g * HEAD_DIM, (g + 1) * HEAD_DIM)
        kn = _rms(k_ref[:, gs], kg_ref[...])
        if kn_ref is not None:
            kn_ref[:, gs] = kn
            vc_ref[:, gs] = v_ref[:, gs]
        if use_rope:
            kn = _rope(kn, cos_ref[...], sin_ref[...])
        if n_ctx:
            kall_s[g, 0:n_ctx, :] = ck_ref[:, gs].astype(BF16)
            vall_s[g, 0:n_ctx, :] = cv_ref[:, gs].astype(BF16)
        kall_s[g, n_ctx:n_ctx + length, :] = kn.astype(BF16)
        vall_s[g, n_ctx:n_ctx + length, :] = v_ref[:, gs].astype(BF16)

    def scores(i, slot):
        rows = pl.ds(pl.multiple_of(i * qb, qb), qb)
        for g in range(N_KV_HEADS):
            qs = []
            for r in range(rep):
                hs = slice((g * rep + r) * HEAD_DIM, (g * rep + r + 1) * HEAD_DIM)
                qn = _rms(q_ref[rows, hs], qg_ref[...])
                if use_rope:
                    qn = _rope(qn, cos_ref[rows, :], sin_ref[rows, :])
                qs.append(qn.astype(BF16))
            s_s[slot, g] = _dot_nt(jnp.concatenate(qs, axis=0), kall_s[g])

    def outputs(i, slot):
        rows = pl.ds(pl.multiple_of(i * qb, qb), qb)
        for g in range(N_KV_HEADS):
            s = s_s[slot, g]
            m = jnp.max(s, axis=-1, keepdims=True)
            p = jnp.exp((s - m) * scale)
            o = _dot(p.astype(BF16), vall_s[g]) / jnp.sum(p, axis=-1, keepdims=True)
            for r in range(rep):
                hs = slice((g * rep + r) * HEAD_DIM, (g * rep + r + 1) * HEAD_DIM)
                o_ref[rows, hs] = o[r * qb:(r + 1) * qb].astype(o_ref.dtype)

    scores(0, 0)

    def pair(j, carry):
        i = 2 * j
        scores(i + 1, 1)
        outputs(i, 0)
        scores(jnp.minimum(i + 2, nq - 1), 0)
        outputs(i + 1, 1)
        return carry

    lax.fori_loop(0, nq // 2, pair, 0)


def _attn(group, q, k, v, qg, kg, i, ck_all, cv_all, cos_t, sin_t, o_prev, kc_prev, vc_prev,
          layer_name):
    prompt = group == "prompt"
    length = SEQ if prompt else DEC_SEQ
    nb = BATCH if prompt else DEC_BATCH
    off = 0 if prompt else N_PROMPT_TOK // DEC_SEQ
    kvw = N_KV_HEADS * HEAD_DIM
    n_ctx = 0 if prompt else PAST_LEN
    qb = 128
    rep = N_HEADS // N_KV_HEADS
    in_specs = _seq_specs(group, (D_MODEL, kvw, kvw))
    in_specs += [_const_spec(qg.shape), _const_spec(kg.shape)]
    args = [q, k, v, qg, kg]
    o_spec = pl.BlockSpec((length, D_MODEL), lambda b: (b + off, 0))
    o_shape = jax.ShapeDtypeStruct((N_TOK, D_MODEL), BF16)
    aliases = {}
    if prompt:
        c_spec = pl.BlockSpec((None, None, length, kvw), lambda b: (b, i, 0, 0))
        c_shape = jax.ShapeDtypeStruct((nb, N_ATTN_LAYERS, length, kvw), F32)
        out_specs = [o_spec, c_spec, c_spec]
        out_shape = [o_shape, c_shape, c_shape]
        if kc_prev is not None:
            in_specs += [pl.BlockSpec(memory_space=pl.ANY)] * 2
            args += [kc_prev, vc_prev]
            aliases = {len(args) - 2: 1, len(args) - 1: 2}
    else:
        ctx_spec = pl.BlockSpec((None, None, n_ctx, kvw), lambda b: (b, i, 0, 0))
        in_specs += [ctx_spec, ctx_spec, _const_spec(cos_t.shape), _const_spec(sin_t.shape),
                     pl.BlockSpec(memory_space=pl.ANY)]
        args += [ck_all, cv_all, cos_t, sin_t, o_prev]
        out_specs = [o_spec]
        out_shape = [o_shape]
        aliases = {len(args) - 1: 0}
    return pl.pallas_call(
        functools.partial(_attn_kernel, length=length, n_ctx=n_ctx, use_rope=not prompt,
                          has_cache_out=prompt, n_alias=len(aliases), qb=qb),
        grid=(nb,),
        in_specs=in_specs,
        out_specs=out_specs,
        out_shape=out_shape,
        scratch_shapes=[pltpu.VMEM((N_KV_HEADS, n_ctx + length, HEAD_DIM), BF16),
                        pltpu.VMEM((N_KV_HEADS, n_ctx + length, HEAD_DIM), BF16),
                        pltpu.VMEM((2, N_KV_HEADS, rep * qb, n_ctx + length), F32)],
        input_output_aliases=aliases,
        compiler_params=_cparams(1),
        name=f"attn_{group}_{layer_name}",
    )(*args)


def _rope_tables(rows):
    row_pos = jnp.repeat(jnp.arange(rows, dtype=F32), GRID_W)
    col_pos = jnp.tile(jnp.arange(GRID_W, dtype=F32), rows)
    inv_freq = jnp.power(ROPE_THETA, -jnp.arange(0, ROPE_AXIS_DIM, 2, dtype=F32) / ROPE_AXIS_DIM)
    ang_r = row_pos[:, None] * inv_freq
    ang_c = col_pos[:, None] * inv_freq
    cos_t = jnp.concatenate([jnp.cos(ang_r), jnp.cos(ang_r), jnp.cos(ang_c), jnp.cos(ang_c)], axis=1)
    sin_t = jnp.concatenate([-jnp.sin(ang_r), jnp.sin(ang_r), -jnp.sin(ang_c), jnp.sin(ang_c)], axis=1)
    return cos_t, sin_t


def _block_diag_tiles(w):
    per = MXU_DIM // LRU_BW
    n_tiles = LRU_BLOCKS // per
    w = w.reshape(2, n_tiles, per, LRU_BW, LRU_BW)
    eye = jnp.eye(per, dtype=w.dtype)
    tiles = jnp.einsum('dtpio,pq->dtpiqo', w, eye)
    return tiles.reshape(2, n_tiles, MXU_DIM, MXU_DIM).astype(BF16)


def _row(v):
    return v.reshape(1, -1)


def kernel(x_prompt, x_sample, state_ssd, state_lru, cache_k, cache_v, c, c_ctx,
           w_mod, b_mod, norm_g, w_in_ssm, ssd_conv_w, ssd_conv_b, ssd_a_log, ssd_dt_bias,
           ssd_d, ssd_norm_w, lru_conv_w, lru_conv_b, lru_wa, lru_ba, lru_wx, lru_bx,
           lru_lambda, w_out_ssm, w_in_ca, conf_dw_w, conf_dw_b, conf_ln_g, conf_ln_b,
           q_norm_g, k_norm_g, w_out_ca, ffn_w_in, ffn_conv_w, ffn_conv_b, ffn_w_out):
    x = jnp.concatenate([x_prompt.reshape(N_PROMPT_TOK, D_MODEL),
                         x_sample.reshape(N_SAMPLE_TOK, D_MODEL)], axis=0)
    cvec = jnp.concatenate(
        [c_ctx[None], c, jnp.zeros((N_MOD_ROWS - 1 - DEC_BATCH, D_MODEL), F32)], axis=0)
    mods = _modulation_all(cvec, w_mod, b_mod)
    cos_t, sin_t = _rope_tables(DEC_SEQ // GRID_W)
    kvw = N_KV_HEADS * HEAD_DIM

    w_ssm_in = jnp.concatenate(
        [w_in_ssm[..., :SSM_O2], w_in_ssm[..., SSM_O3:], w_in_ssm[..., SSM_O2:SSM_O3],
         jnp.zeros((N_SSM_LAYERS, D_MODEL, DT_PAD - 2 * SSD_HEADS), F32)], axis=-1).astype(BF16)
    w_ssm_out = w_out_ssm.astype(BF16)
    w_ca_in = w_in_ca.astype(BF16)
    w_ca_out = w_out_ca.astype(BF16)
    w_ffn_in = ffn_w_in.astype(BF16)
    w_ffn_out = ffn_w_out.astype(BF16)
    ssd_h0 = state_ssd.reshape(DEC_BATCH, N_SSM_LAYERS, 2, SSD_INNER, SSD_STATE)
    ck_all = cache_k.reshape(DEC_BATCH, N_ATTN_LAYERS, PAST_LEN, kvw)
    cv_all = cache_v.reshape(DEC_BATCH, N_ATTN_LAYERS, PAST_LEN, kvw)
    pad32 = DT_PAD - 2 * SSD_HEADS

    ssd_st = lru_st = kc = vc = None
    for layer in range(DEPTH):
        i = layer // 2
        m = mods[layer]
        name = f"l{layer}"
        g0, g1, g2, g3 = (_row(norm_g[layer, j]) for j in range(4))
        if layer % 2 == 0:
            z, xbc, xl, gl, dt = _inproj(
                x, g0, m[0], m[1], w_ssm_in, i,
                (SSD_INNER, SSD_CONV_DIM, LRU_WIDTH, LRU_WIDTH, DT_PAD), f"inproj_ssm_{name}")
            dtb = jnp.pad(ssd_dt_bias[i].reshape(1, -1), ((0, 0), (0, pad32)))
            alog = jnp.pad(ssd_a_log[i].reshape(1, -1), ((0, 0), (0, pad32)))
            dskip = _row(jnp.repeat(ssd_d[i], SSD_HEADDIM))
            ssd_args = (ssd_conv_w[i], _row(ssd_conv_b[i]), dtb, alog, dskip, _row(ssd_norm_w[i]))
            y, ssd_st = _ssd("prompt", z, xbc, dt, *ssd_args, i, None, None, ssd_st, name)
            (y,) = _ssd("sample", z, xbc, dt, *ssd_args, i, ssd_h0, y, None, name)
            lru_args = (lru_conv_w[i], _row(lru_conv_b[i]), _block_diag_tiles(lru_wa[i]),
                        _block_diag_tiles(lru_wx[i]), lru_ba[i], lru_bx[i], lru_lambda[i])
            yl, lru_st = _lru("prompt", xl, gl, *lru_args, i, None, None, lru_st, name)
            (yl,) = _lru("sample", xl, gl, *lru_args, i, state_lru, yl, None, name)
            x = _outproj(y, yl, w_ssm_out, i, x, g1, m[2], f"outproj_ssm_{name}")
        else:
            ga, gg, q, k, v = _inproj(
                x, g0, m[0], m[1], w_ca_in, i,
                (CONV_WIDTH, CONV_WIDTH, N_HEADS * HEAD_DIM, kvw, kvw), f"inproj_ca_{name}")
            conv_args = (conf_dw_w[i], _row(conf_dw_b[i]), _row(conf_ln_g[i]), _row(conf_ln_b[i]))
            cvo = _confconv("prompt", ga, gg, *conv_args, None, name)
            cvo = _confconv("sample", ga, gg, *conv_args, cvo, name)
            qg, kg = _row(q_norm_g[i]), _row(k_norm_g[i])
            o, kc, vc = _attn("prompt", q, k, v, qg, kg, i, None, None, None, None, None, kc, vc,
                              name)
            (o,) = _attn("sample", q, k, v, qg, kg, i, ck_all, cv_all, cos_t, sin_t, o, None, None,
                         name)
            x = _outproj(cvo, o, w_ca_out, i, x, g1, m[2], f"outproj_ca_{name}")
        x = _ffn(x, g2, m[3], m[4], w_ffn_in, ffn_conv_w[layer], _row(ffn_conv_b[layer]),
                 w_ffn_out, layer, g3, m[5], f"ffn_{name}")

    xp = x[:N_PROMPT_TOK].reshape(BATCH, SEQ, D_MODEL)
    xs = x[N_PROMPT_TOK:].reshape(DEC_BATCH, DEC_SEQ, D_MODEL)
    return (xp, xs,
            ssd_st.reshape(BATCH, N_SSM_LAYERS, 2, SSD_HEADS, SSD_HEADDIM, SSD_STATE),
            lru_st,
            kc.reshape(BATCH, N_ATTN_LAYERS, SEQ, N_KV_HEADS, HEAD_DIM),
            vc.reshape(BATCH, N_ATTN_LAYERS, SEQ, N_KV_HEADS, HEAD_DIM))
```

```python
import functools
import math

import jax
import jax.numpy as jnp
from jax import lax
from jax.experimental import pallas as pl
from jax.experimental.pallas import tpu as pltpu

F32 = jnp.float32
BF16 = jnp.bfloat16

D_MODEL = 1024
BATCH = 16
SEQ = 256
DEPTH = 4
N_SSM_LAYERS = (DEPTH + 1) // 2
N_ATTN_LAYERS = DEPTH // 2
DEC_BATCH = 4
DEC_SEQ = 1024
PAST_LEN = 512
GRID_W = 64
EPS = 1e-6
SSD_HEADDIM = 64
SSD_INNER = D_MODEL
SSD_HEADS = SSD_INNER // SSD_HEADDIM
SSD_GROUPS = 2
SSD_STATE = 128
SSD_CONV = 4
SSD_CHUNK = 128
SSD_CONV_DIM = SSD_INNER + 2 * SSD_GROUPS * SSD_STATE
LRU_WIDTH = D_MODEL
LRU_BW = 64
LRU_BLOCKS = LRU_WIDTH // LRU_BW
LRU_CONV = 4
LRU_C = 8.0
CONV_WIDTH = D_MODEL
CONF_K = 31
HEAD_DIM = 128
N_HEADS = D_MODEL // HEAD_DIM
N_KV_HEADS = 2
ROPE_THETA = 10000.0
ROPE_AXIS_DIM = HEAD_DIM // 2
D_FF = 2816
FFN_CONV = 3

N_PROMPT_TOK = BATCH * SEQ
N_SAMPLE_TOK = DEC_BATCH * DEC_SEQ
N_TOK = N_PROMPT_TOK + N_SAMPLE_TOK
N_MOD_ROWS = 8
LANES = 128
SUBLANES = 8
MXU_DIM = 256
DT_PAD = LANES
SSM_O2 = SSD_INNER + SSD_CONV_DIM
DT_COL0 = DT_PAD - 2 * SSD_HEADS
LOG2E = 1.0 / math.log(2.0)
VMEM_LIMIT = 58 * 1024 * 1024

TM_LINEAR = 512
TM_FFN = 1024


def _cparams(n_axes):
    return pltpu.CompilerParams(
        dimension_semantics=("arbitrary",) * n_axes,
        vmem_limit_bytes=VMEM_LIMIT)


def _const_spec(shape):
    nd = len(shape)
    return pl.BlockSpec(shape, lambda *_: (0,) * nd, pipeline_mode=pl.Buffered(1))


def _layer_spec(shape, layer):
    nd = len(shape) - 1
    return pl.BlockSpec((None,) + tuple(shape[1:]), lambda *_: (layer,) + (0,) * nd,
                        pipeline_mode=pl.Buffered(1))


def _mod_row(i, tm):
    start = i * tm
    return jnp.where(start < N_PROMPT_TOK, 0, 1 + (start - N_PROMPT_TOK) // DEC_SEQ)


def _mod_spec(tm):
    return pl.BlockSpec((None, 1, D_MODEL), lambda i: (_mod_row(i, tm), 0, 0))


def _x_specs(xs, tm):
    if len(xs) == 1:
        return [pl.BlockSpec((tm, D_MODEL), lambda i: (i, 0))]
    n_p = N_PROMPT_TOK // tm
    return [pl.BlockSpec((tm, D_MODEL), lambda i: (jnp.minimum(i, n_p - 1), 0)),
            pl.BlockSpec((tm, D_MODEL), lambda i: (jnp.maximum(i - n_p, 0), 0))]


def _x_tile(x_refs, tm):
    if len(x_refs) == 1:
        return x_refs[0][...]
    is_prompt = pl.program_id(0) * tm < N_PROMPT_TOK
    return jnp.where(is_prompt, x_refs[0][...], x_refs[1][...])


def _sigmoid(x):
    return jax.nn.sigmoid(x)


def _sigmoid_tanh(x):
    return 0.5 * jnp.tanh(0.5 * x) + 0.5


def _silu(x):
    return x * _sigmoid(x)


def _softplus(x):
    return jnp.maximum(x, 0.0) + jnp.log1p(jnp.exp(-jnp.abs(x)))


def _gelu_tanh(x):
    return 0.5 * x * (1.0 + jnp.tanh(math.sqrt(2.0 / math.pi) * (x + 0.044715 * (x * x * x))))


def _rms(x, g):
    ms = jnp.mean(x * x, axis=-1, keepdims=True)
    return (x * lax.rsqrt(ms + EPS)) * g


def _dot(a, b):
    return jnp.dot(a, b, preferred_element_type=F32)


def _dot_nt(a, b):
    return lax.dot_general(a, b, (((1,), (1,)), ((), ())), preferred_element_type=F32)


def _mod_kernel(c_ref, w_ref, b_ref, o_ref):
    c = c_ref[...]
    s = _silu(c).astype(BF16)
    o_ref[...] = _dot(s, w_ref[...].astype(BF16)) + b_ref[...]


def _modulation_all(cvec, w_mod, b_mod):
    tn = 1536
    n_out = 6 * D_MODEL
    out = pl.pallas_call(
        _mod_kernel,
        grid=(DEPTH, n_out // tn),
        in_specs=[
            pl.BlockSpec((N_MOD_ROWS, D_MODEL), lambda l, j: (0, 0)),
            pl.BlockSpec((None, D_MODEL, tn), lambda l, j: (l, 0, j)),
            pl.BlockSpec((None, 1, tn), lambda l, j: (l, 0, j)),
        ],
        out_specs=pl.BlockSpec((None, N_MOD_ROWS, tn), lambda l, j: (l, 0, j)),
        out_shape=jax.ShapeDtypeStruct((DEPTH, N_MOD_ROWS, n_out), F32),
        compiler_params=_cparams(2),
        name="modulation",
    )(cvec, w_mod, b_mod.reshape(DEPTH, 1, n_out))
    out = out.reshape(DEPTH, N_MOD_ROWS, 6, 1, D_MODEL)
    return jnp.transpose(out, (0, 2, 1, 3, 4))


def _inproj_kernel(*refs, n_x, widths, chunk, rot):
    x_refs = refs[:n_x]
    g_ref, shift_ref, scale_ref, w_ref = refs[n_x:n_x + 4]
    rest = refs[n_x + 4:]
    o_refs = rest[:len(widths)]
    h_ref = rest[len(widths)]
    if rot is not None:
        rot_start, rot_by = rot
        wr_ref = rest[len(widths) + 1]
        rot_w = w_ref.shape[1] - rot_start

        @pl.when(pl.program_id(0) == 0)
        def _():
            wr_ref[...] = pltpu.roll(w_ref[:, rot_start:], rot_w - rot_by, 1)

    x = _x_tile(x_refs, x_refs[0].shape[0])
    h = _rms(x, g_ref[...]) * (1.0 + scale_ref[...]) + shift_ref[...]
    h_ref[...] = h.astype(BF16)
    off = 0
    for o_ref, n in zip(o_refs, widths):
        for c0 in range(0, n, chunk):
            c1 = min(c0 + chunk, n)
            if rot is not None and off >= rot_start:
                w_blk = wr_ref[:, off - rot_start + c0:off - rot_start + c1]
            else:
                w_blk = w_ref[:, off + c0:off + c1]
            o_ref[:, c0:c1] = _dot(h_ref[...], w_blk)
        off += n


def _inproj(xs, g, shift, scale, w, layer, widths, name, rot=None):
    tm = TM_LINEAR
    assert sum(widths) == w.shape[2] and all(n % LANES == 0 for n in widths)
    scratch = [pltpu.VMEM((tm, D_MODEL), BF16)]
    if rot is not None:
        scratch.append(pltpu.VMEM((D_MODEL, w.shape[2] - rot[0]), BF16))
    return pl.pallas_call(
        functools.partial(_inproj_kernel, n_x=len(xs), widths=tuple(widths), chunk=512, rot=rot),
        grid=(N_TOK // tm,),
        in_specs=_x_specs(xs, tm) + [_const_spec((1, D_MODEL)), _mod_spec(tm), _mod_spec(tm),
                                     _layer_spec(w.shape, layer)],
        out_specs=[pl.BlockSpec((tm, n), lambda i: (i, 0)) for n in widths],
        out_shape=[jax.ShapeDtypeStruct((N_TOK, n), F32) for n in widths],
        scratch_shapes=scratch,
        compiler_params=_cparams(1),
        name=name,
    )(*xs, g, shift, scale, w)


def _outproj_kernel(a_ref, b_ref, w_ref, *rest, n_x):
    x_refs = rest[:n_x]
    g_ref, gate_ref, o_ref = rest[n_x:]
    ka = a_ref.shape[1]
    acc = _dot(a_ref[...], w_ref[0:ka, :]) + _dot(b_ref[...], w_ref[ka:, :])
    o_ref[...] = _x_tile(x_refs, o_ref.shape[0]) + gate_ref[...] * _rms(acc, g_ref[...])


def _outproj(a, b, w, layer, xs, g, gate, name):
    tm = TM_LINEAR
    return pl.pallas_call(
        functools.partial(_outproj_kernel, n_x=len(xs)),
        grid=(N_TOK // tm,),
        in_specs=[
            pl.BlockSpec((tm, a.shape[1]), lambda i: (i, 0)),
            pl.BlockSpec((tm, b.shape[1]), lambda i: (i, 0)),
            _layer_spec(w.shape, layer),
        ] + _x_specs(xs, tm) + [_const_spec((1, D_MODEL)), _mod_spec(tm)],
        out_specs=pl.BlockSpec((tm, D_MODEL), lambda i: (i, 0)),
        out_shape=jax.ShapeDtypeStruct((N_TOK, D_MODEL), F32),
        compiler_params=_cparams(1),
        name=name,
    )(a, b, w, *xs, g, gate)


def _ffn_kernel(x_ref, g2_ref, shift_ref, scale_ref, wi_ref, cw_ref, cb_ref,
                wo_ref, g3_ref, gate_ref, o_ref, h_ref, act_ref, *, chunk, row_blk):
    tm = x_ref.shape[0]
    i = pl.program_id(0)
    h = _rms(x_ref[...], g2_ref[...]) * (1.0 + scale_ref[...]) + shift_ref[...]
    h_ref[...] = h.astype(BF16)
    lseq = jnp.where(i * tm < N_PROMPT_TOK, SEQ, DEC_SEQ)
    pos = lax.broadcasted_iota(jnp.int32, (tm, 1), 0) & (lseq - 1)
    first = pos == 0
    last = pos == lseq - 1
    for c0 in range(0, D_FF, chunk):
        c1 = min(c0 + chunk, D_FF)
        gt = _dot(h_ref[...], wi_ref[:, c0:c1])
        vl = _dot(h_ref[...], wi_ref[:, D_FF + c0:D_FF + c1])
        g_prev = jnp.where(first, 0.0, pltpu.roll(gt, 1, 0))
        g_next = jnp.where(last, 0.0, pltpu.roll(gt, tm - 1, 0))
        conv = (g_prev * cw_ref[0:1, c0:c1] + gt * cw_ref[1:2, c0:c1]
                + g_next * cw_ref[2:3, c0:c1] + cb_ref[:, c0:c1])
        act_ref[:, c0:c1] = (_silu(conv) * vl).astype(BF16)
    for r0 in range(0, tm, row_blk):
        rs = slice(r0, r0 + row_blk)
        acc = _dot(act_ref[rs, :], wo_ref[...])
        o_ref[rs, :] = x_ref[rs, :] + gate_ref[...] * _rms(acc, g3_ref[...])


def _ffn(x, g2, shift, scale, wi, cw, cb, wo, layer, g3, gate, name):
    tm = TM_FFN
    row_spec = pl.BlockSpec((tm, D_MODEL), lambda i: (i, 0))
    return pl.pallas_call(
        functools.partial(_ffn_kernel, chunk=2 * MXU_DIM, row_blk=MXU_DIM),
        grid=(N_TOK // tm,),
        in_specs=[
            row_spec, _const_spec((1, D_MODEL)), _mod_spec(tm), _mod_spec(tm),
            _layer_spec(wi.shape, layer), _const_spec(cw.shape),
            _const_spec(cb.shape), _layer_spec(wo.shape, layer), _const_spec((1, D_MODEL)),
            _mod_spec(tm),
        ],
        out_specs=row_spec,
        out_shape=jax.ShapeDtypeStruct((N_TOK, D_MODEL), F32),
        scratch_shapes=[pltpu.VMEM((tm, D_MODEL), BF16), pltpu.VMEM((tm, D_FF), BF16)],
        compiler_params=_cparams(1),
        name=name,
    )(x, g2, shift, scale, wi, cw, cb, wo, g3, gate)


def _seq_specs(group, widths):
    if group == "prompt":
        return [pl.BlockSpec((SEQ, w), lambda b: (b, 0)) for w in widths]
    off = N_PROMPT_TOK // DEC_SEQ
    return [pl.BlockSpec((DEC_SEQ, w), lambda b: (b + off, 0)) for w in widths]


def _short_conv_chunk(pad_ref, cw_ref, cb_ref, base, t, halo, taps, left, cols):
    win = pad_ref[pl.ds(base, t + 2 * halo), cols]
    n = t + 2 * halo
    acc = cb_ref[:, cols]
    for j in range(taps):
        s = (left - j) % n
        rolled = win if s == 0 else pltpu.roll(win, s, 0)
        acc = acc + rolled[halo:halo + t] * cw_ref[j:j + 1, cols]
    return acc


def _fill_padded(pad_ref, src, length, halo):
    width = pad_ref.shape[1]
    pad_ref[0:halo, :] = jnp.zeros((halo, width), F32)
    pad_ref[halo + length:2 * halo + length, :] = jnp.zeros((halo, width), F32)
    pad_ref[halo:halo + length, :] = src


def _lane_pairs(m, first_col, n_pairs, rows):
    lane = lax.broadcasted_iota(jnp.int32, (rows, LANES), 1)
    lo_half = lane < SSD_HEADDIM
    pieces = []
    for k in range(n_pairs):
        c = first_col + 2 * k
        lo = jnp.broadcast_to(m[:, c:c + 1], (rows, LANES))
        hi = jnp.broadcast_to(m[:, c + 1:c + 2], (rows, LANES))
        pieces.append(jnp.where(lo_half, lo, hi))
    return jnp.concatenate(pieces, axis=1)


def _ssd_kernel(*refs, length, has_h0, has_state_out, n_alias):
    it = iter(refs)
    z_ref, xbc_ref, dt_ref = next(it), next(it), next(it)
    cw_ref, cb_ref, dtb_ref, alog_ref, dskip_ref, nw_ref = (next(it) for _ in range(6))
    h0_ref = next(it) if has_h0 else None
    for _ in range(n_alias):
        next(it)
    y_ref = next(it)
    st_ref = next(it) if has_state_out else None
    pad_s, xs_s, bc_s, cum_s, row_t_s, w_t_s, cd_s, yacc_s, state_s = (
        next(it) for _ in range(9))

    t = SSD_CHUNK
    nc = length // t
    halo = SUBLANES
    gw = SSD_INNER // SSD_GROUPS
    pairs_per_group = SSD_HEADS // SSD_GROUPS // 2

    _fill_padded(pad_s, xbc_ref[...], length, halo)
    a_row = -jnp.exp(alog_ref[...])

    ri = lax.broadcasted_iota(jnp.int32, (t, t), 0)
    ci = lax.broadcasted_iota(jnp.int32, (t, t), 1)
    keep = (ci <= ri, ci >= ri)
    tril = keep[0].astype(F32)
    lane = lax.broadcasted_iota(jnp.int32, (t, LANES), 1)
    lo_half = lane < SSD_HEADDIM
    fwd_cols = lane < DT_COL0 + SSD_HEADS
    fwd_rows = ri < DT_COL0 + SSD_HEADS

    def prep(c, carry):
        base = pl.multiple_of(c * t, t)
        rows = pl.ds(base, t)
        for c0 in range(0, SSD_CONV_DIM, MXU_DIM):
            cols = slice(c0, c0 + MXU_DIM)
            conv = _silu(_short_conv_chunk(pad_s, cw_ref, cb_ref, base, t, halo, SSD_CONV, 2, cols))
            if c0 < SSD_INNER:
                xs_s[rows, cols] = conv
                yacc_s[rows, cols] = conv * dskip_ref[:, cols]
            else:
                bc_s[rows, c0 - SSD_INNER:c0 - SSD_INNER + MXU_DIM] = conv
        dtsp = _softplus(dt_ref[rows, :] + dtb_ref[...])
        a_c = dtsp * a_row
        pre = jnp.dot(tril, a_c, preferred_element_type=F32, precision=lax.Precision.HIGHEST)
        suf = pre[t - 1:t, :] - pre + a_c
        cum = jnp.where(fwd_cols, pre, suf) * LOG2E
        cum_s[rows, :] = cum
        cum_t = cum.T
        dt_t = dtsp.T
        edge_col = jnp.where(fwd_rows[:, 0:1], cum_t[:, t - 1:t], cum_t[:, 0:1])
        row_t_s[c] = cum_t - jnp.log(dt_t) * LOG2E
        w_t_s[c] = dt_t * jnp.exp2(edge_col - cum_t)
        edge_row = jnp.where(fwd_cols[0:1, :], cum[t - 1:t, :], cum[0:1, :])
        cd_s[c] = jnp.broadcast_to(jnp.exp2(edge_row), (SUBLANES, LANES))
        return carry

    lax.fori_loop(0, nc, prep, 0)

    for d in range(2):
        if has_h0:
            for k in range(SSD_INNER // LANES):
                ks = slice(k * LANES, (k + 1) * LANES)
                state_s[d, :, ks] = h0_ref[d, ks, :].T
        else:
            state_s[d] = jnp.zeros((SSD_STATE, SSD_INNER), F32)

    def block_diag(m):
        return jnp.concatenate([jnp.where(lo_half, m, 0.0).astype(BF16),
                                jnp.where(lo_half, 0.0, m).astype(BF16)], axis=0)

    def chunk_step(c, carry):
        for d in range(2):
            cidx = c if d == 0 else nc - 1 - c
            base = pl.multiple_of(cidx * t, t)
            rows = pl.ds(base, t)
            cum = cum_s[rows, :]
            bc = bc_s[rows, :]
            cd = cd_s[cidx][0:1, :]
            for g in range(SSD_GROUPS):
                col0 = DT_COL0 + d * SSD_HEADS + g * 2 * pairs_per_group
                b_g = bc[:, g * SSD_STATE:(g + 1) * SSD_STATE]
                c_g = bc[:, (SSD_GROUPS + g) * SSD_STATE:(SSD_GROUPS + g + 1) * SSD_STATE]
                gmat = _dot_nt(c_g.astype(BF16), b_g.astype(BF16))
                b_t = b_g.T
                cd_rep = _lane_pairs(cd, col0, pairs_per_group, 1)
                for kk in range(pairs_per_group):
                    lanes = slice(g * gw + kk * LANES, g * gw + (kk + 1) * LANES)
                    rhs_x = block_diag(xs_s[rows, lanes])
                    st = state_s[d, :, lanes]
                    gl, ce, bw = [], [], []
                    for hcur in (col0 + 2 * kk, col0 + 2 * kk + 1):
                        hrow = pl.ds(hcur, 1)
                        colb = jnp.broadcast_to(cum[:, hcur:hcur + 1], (t, t))
                        rowb = jnp.broadcast_to(row_t_s[cidx, hrow, :], (t, t))
                        wrow = jnp.broadcast_to(w_t_s[cidx, hrow, :], (t, t))
                        lmat = jnp.exp2(jnp.where(keep[d], colb - rowb, -jnp.inf))
                        gl.append((gmat * lmat).astype(BF16))
                        ce.append((c_g * jnp.exp2(colb)).astype(BF16))
                        bw.append((b_t * wrow).astype(BF16))
                    y_p = _dot(jnp.concatenate(gl + ce, axis=1),
                               jnp.concatenate([rhs_x, block_diag(st)], axis=0))
                    state_s[d, :, lanes] = (st * cd_rep[:, kk * LANES:(kk + 1) * LANES]
                                            + _dot(jnp.concatenate(bw, axis=1), rhs_x))
                    yacc_s[rows, lanes] += y_p
        return carry

    lax.fori_loop(0, nc, chunk_step, 0)

    def finish(c, carry):
        base = pl.multiple_of(c * t, t)
        rows = pl.ds(base, t)
        y = yacc_s[rows, :] * _silu(z_ref[rows, :])
        y_ref[rows, :] = _rms(y, nw_ref[...]).astype(y_ref.dtype)
        return carry

    lax.fori_loop(0, nc, finish, 0)

    if has_state_out:
        for d in range(2):
            for k in range(SSD_INNER // LANES):
                ks = slice(k * LANES, (k + 1) * LANES)
                st_ref[d, ks, :] = state_s[d, :, ks].T


def _ssd(group, z, xbc, dt, cw, cb, dtb, alog, dskip, nw, i, h0_all, y_prev, st_prev, layer_name):
    prompt = group == "prompt"
    length = SEQ if prompt else DEC_SEQ
    nb = BATCH if prompt else DEC_BATCH
    off = 0 if prompt else N_PROMPT_TOK // DEC_SEQ
    in_specs = _seq_specs(group, (SSD_INNER, SSD_CONV_DIM, DT_PAD))
    in_specs += [_const_spec(a.shape) for a in (cw, cb, dtb, alog, dskip, nw)]
    args = [z, xbc, dt, cw, cb, dtb, alog, dskip, nw]
    y_spec = pl.BlockSpec((length, SSD_INNER), lambda b: (b + off, 0))
    y_shape = jax.ShapeDtypeStruct((N_TOK, SSD_INNER), BF16)
    st_block = (None, None, 2, SSD_INNER, SSD_STATE)
    aliases = {}
    if prompt:
        out_specs = [y_spec, pl.BlockSpec(st_block, lambda b: (b, i, 0, 0, 0))]
        out_shape = [y_shape, jax.ShapeDtypeStruct((nb, N_SSM_LAYERS, 2, SSD_INNER, SSD_STATE), F32)]
        if st_prev is not None:
            in_specs.append(pl.BlockSpec(memory_space=pl.ANY))
            args.append(st_prev)
            aliases = {len(args) - 1: 1}
    else:
        in_specs += [pl.BlockSpec(st_block, lambda b: (b, i, 0, 0, 0)),
                     pl.BlockSpec(memory_space=pl.ANY)]
        args += [h0_all, y_prev]
        out_specs = [y_spec]
        out_shape = [y_shape]
        aliases = {len(args) - 1: 0}
    halo = SUBLANES
    nc = length // SSD_CHUNK
    scratch = [
        pltpu.VMEM((length + 2 * halo, SSD_CONV_DIM), F32),
        pltpu.VMEM((length, SSD_INNER), F32),
        pltpu.VMEM((length, 2 * SSD_GROUPS * SSD_STATE), F32),
        pltpu.VMEM((length, DT_PAD), F32),
        pltpu.VMEM((nc, DT_PAD, SSD_CHUNK), F32),
        pltpu.VMEM((nc, DT_PAD, SSD_CHUNK), F32),
        pltpu.VMEM((nc, SUBLANES, DT_PAD), F32),
        pltpu.VMEM((length, SSD_INNER), F32),
        pltpu.VMEM((2, SSD_STATE, SSD_INNER), F32),
    ]
    return pl.pallas_call(
        functools.partial(_ssd_kernel, length=length, has_h0=not prompt, has_state_out=prompt,
                          n_alias=len(aliases)),
        grid=(nb,),
        in_specs=in_specs,
        out_specs=out_specs,
        out_shape=out_shape,
        scratch_shapes=scratch,
        input_output_aliases=aliases,
        compiler_params=_cparams(1),
        name=f"ssd_{group}_{layer_name}",
    )(*args)


def _lru_kernel(*refs, length, has_h0, has_state_out, n_alias):
    it = iter(refs)
    xl_ref, gl_ref = next(it), next(it)
    cw_ref, cb_ref, wa_ref, wx_ref, ba_ref, bx_ref, lam_ref = (next(it) for _ in range(7))
    h0_ref = next(it) if has_h0 else None
    for _ in range(n_alias):
        next(it)
    o_ref = next(it)
    st_ref = next(it) if has_state_out else None
    pad_s, xc_s, a_s, u_s, h_s = (next(it) for _ in range(5))

    t = 128
    nc = length // t
    halo = SUBLANES
    n_tiles = LRU_WIDTH // MXU_DIM

    _fill_padded(pad_s, xl_ref[...], length, halo)

    def prep(c, carry):
        base = pl.multiple_of(c * t, t)
        for c0 in range(0, LRU_WIDTH, MXU_DIM):
            cols = slice(c0, c0 + MXU_DIM)
            xc_s[pl.ds(base, t), cols] = _short_conv_chunk(
                pad_s, cw_ref, cb_ref, base, t, halo, LRU_CONV, 2, cols)
        return carry

    lax.fori_loop(0, nc, prep, 0)

    row8 = lax.broadcasted_iota(jnp.int32, (SUBLANES, LRU_WIDTH), 0)
    n_groups = length // SUBLANES

    for d in range(2):
        log_a_unit = (-LRU_C) * _softplus(-lam_ref[d:d + 1, :])

        def gates(c, carry, d=d, log_a_unit=log_a_unit):
            base = pl.multiple_of(c * t, t)
            rows = pl.ds(base, t)
            xc = xc_s[rows, :]
            xb = xc.astype(BF16)
            ra, ri = [], []
            for j in range(n_tiles):
                js = slice(j * MXU_DIM, (j + 1) * MXU_DIM)
                ra.append(_dot(xb[:, js], wa_ref[d, j]))
                ri.append(_dot(xb[:, js], wx_ref[d, j]))
            r = _sigmoid_tanh(jnp.concatenate(ra, axis=1) + ba_ref[d:d + 1, :])
            gi = _sigmoid_tanh(jnp.concatenate(ri, axis=1) + bx_ref[d:d + 1, :])
            log_a = r * log_a_unit
            a = jnp.exp2(r * (log_a_unit * LOG2E))
            gap = -jnp.tanh(log_a) * (a * a + 1.0)
            root = jnp.where(gap > 0.0, gap * lax.rsqrt(gap), 0.0)
            a_s[rows, :] = a
            u_s[rows, :] = root * gi * xc
            return carry

        lax.fori_loop(0, nc, gates, 0)

        if has_h0:
            carry0 = jnp.broadcast_to(h0_ref[d:d + 1, :], (SUBLANES, LRU_WIDTH))
        else:
            carry0 = jnp.zeros((SUBLANES, LRU_WIDTH), F32)

        def scan(gi_, carry, d=d):
            g = gi_ if d == 0 else n_groups - 1 - gi_
            base = pl.multiple_of(g * SUBLANES, SUBLANES)
            rows = pl.ds(base, SUBLANES)
            av = a_s[rows, :]
            uv = u_s[rows, :]
            for k in (1, 2, 4):
                if d == 0:
                    shift, valid = k, row8 >= k
                else:
                    shift, valid = SUBLANES - k, row8 < SUBLANES - k
                a_sh = pltpu.roll(av, shift, 0)
                u_sh = pltpu.roll(uv, shift, 0)
                uv = jnp.where(valid, av * u_sh + uv, uv)
                av = jnp.where(valid, av * a_sh, av)
            h = av * carry + uv
            if d == 0:
                h_s[rows, :] = h
                edge = h[SUBLANES - 1:SUBLANES, :]
            else:
                h_s[rows, :] += h
                edge = h[0:1, :]
            return jnp.broadcast_to(edge, (SUBLANES, LRU_WIDTH))

        final = lax.fori_loop(0, n_groups, scan, carry0, unroll=4)
        if has_state_out:
            st_ref[d:d + 1, :] = final[0:1, :]

    def finish(c, carry):
        base = pl.multiple_of(c * t, t)
        rows = pl.ds(base, t)
        o_ref[rows, :] = (_gelu_tanh(gl_ref[rows, :]) * h_s[rows, :]).astype(o_ref.dtype)
        return carry

    lax.fori_loop(0, nc, finish, 0)


def _lru(group, xl, gl, cw, cb, wa, wx, ba, bx, lam, i, h0_all, o_prev, st_prev, layer_name):
    prompt = group == "prompt"
    length = SEQ if prompt else DEC_SEQ
    nb = BATCH if prompt else DEC_BATCH
    off = 0 if prompt else N_PROMPT_TOK // DEC_SEQ
    in_specs = _seq_specs(group, (LRU_WIDTH, LRU_WIDTH))
    in_specs += [_const_spec(a.shape) for a in (cw, cb, wa, wx, ba, bx, lam)]
    args = [xl, gl, cw, cb, wa, wx, ba, bx, lam]
    o_spec = pl.BlockSpec((length, LRU_WIDTH), lambda b: (b + off, 0))
    o_shape = jax.ShapeDtypeStruct((N_TOK, LRU_WIDTH), BF16)
    st_block = (None, None, 2, LRU_WIDTH)
    aliases = {}
    if prompt:
        out_specs = [o_spec, pl.BlockSpec(st_block, lambda b: (b, i, 0, 0))]
        out_shape = [o_shape, jax.ShapeDtypeStruct((nb, N_SSM_LAYERS, 2, LRU_WIDTH), F32)]
        if st_prev is not None:
            in_specs.append(pl.BlockSpec(memory_space=pl.ANY))
            args.append(st_prev)
            aliases = {len(args) - 1: 1}
    else:
        in_specs += [pl.BlockSpec(st_block, lambda b: (b, i, 0, 0)),
                     pl.BlockSpec(memory_space=pl.ANY)]
        args += [h0_all, o_prev]
        out_specs = [o_spec]
        out_shape = [o_shape]
        aliases = {len(args) - 1: 0}
    halo = SUBLANES
    scratch = [pltpu.VMEM((length + 2 * halo, LRU_WIDTH), F32)]
    scratch += [pltpu.VMEM((length, LRU_WIDTH), F32) for _ in range(4)]
    return pl.pallas_call(
        functools.partial(_lru_kernel, length=length, has_h0=not prompt, has_state_out=prompt,
                          n_alias=len(aliases)),
        grid=(nb,),
        in_specs=in_specs,
        out_specs=out_specs,
        out_shape=out_shape,
        scratch_shapes=scratch,
        input_output_aliases=aliases,
        compiler_params=_cparams(1),
        name=f"lru_{group}_{layer_name}",
    )(*args)


def _confconv_kernel(*refs, length, aliased):
    it = iter(refs)
    a_ref, g_ref, w_ref, b_ref, lng_ref, lnb_ref = (next(it) for _ in range(6))
    if aliased:
        next(it)
    o_ref = next(it)
    pad_s, acc_s = next(it), next(it)

    t = 128
    nc = length // t
    halo = 2 * SUBLANES
    left = (CONF_K - 1) // 2
    cblk = MXU_DIM
    n = t + 2 * halo

    _fill_padded(pad_s, a_ref[...] * _sigmoid(g_ref[...]), length, halo)

    def step(c, carry):
        base = pl.multiple_of(c * t, t)
        rows = pl.ds(base, t)
        for cb0 in range(0, CONV_WIDTH, cblk):
            cs = slice(cb0, cb0 + cblk)
            win = pad_s[pl.ds(base, n), cs]
            acc = jnp.broadcast_to(b_ref[:, cs], (t, cblk))
            for s in range(SUBLANES):
                shifted = win if s == 0 else pltpu.roll(win, n - s, 0)
                for m in range(n // SUBLANES):
                    j = SUBLANES * m + s - halo + left
                    if 0 <= j < CONF_K and SUBLANES * m + t <= n:
                        acc = acc + shifted[SUBLANES * m:SUBLANES * m + t] * w_ref[j:j + 1, cs]
            acc_s[:, cs] = acc
        cv = acc_s[...]
        mu = jnp.mean(cv, axis=-1, keepdims=True)
        xc = cv - mu
        var = jnp.mean(xc * xc, axis=-1, keepdims=True)
        y = (xc * lax.rsqrt(var + EPS)) * lng_ref[...] + lnb_ref[...]
        o_ref[rows, :] = _silu(y).astype(o_ref.dtype)
        return carry

    lax.fori_loop(0, nc, step, 0)


def _confconv(group, ga, gg, w, b, lng, lnb, o_prev, layer_name):
    prompt = group == "prompt"
    length = SEQ if prompt else DEC_SEQ
    nb = BATCH if prompt else DEC_BATCH
    off = 0 if prompt else N_PROMPT_TOK // DEC_SEQ
    in_specs = _seq_specs(group, (CONV_WIDTH, CONV_WIDTH))
    in_specs += [_const_spec(a.shape) for a in (w, b, lng, lnb)]
    args = [ga, gg, w, b, lng, lnb]
    aliases = {}
    if not prompt:
        in_specs.append(pl.BlockSpec(memory_space=pl.ANY))
        args.append(o_prev)
        aliases = {len(args) - 1: 0}
    halo = 2 * SUBLANES
    return pl.pallas_call(
        functools.partial(_confconv_kernel, length=length, aliased=not prompt),
        grid=(nb,),
        in_specs=in_specs,
        out_specs=pl.BlockSpec((length, CONV_WIDTH), lambda b: (b + off, 0)),
        out_shape=jax.ShapeDtypeStruct((N_TOK, CONV_WIDTH), BF16),
        scratch_shapes=[pltpu.VMEM((length + 2 * halo, CONV_WIDTH), F32),
                        pltpu.VMEM((128, CONV_WIDTH), F32)],
        input_output_aliases=aliases,
        compiler_params=_cparams(1),
        name=f"confconv_{group}_{layer_name}",
    )(*args)


def _rope(x, cos_t, sin_t):
    lane = lax.broadcasted_iota(jnp.int32, x.shape, 1)
    quarter = ROPE_AXIS_DIM // 2
    partner = jnp.where((lane & quarter) == 0,
                        pltpu.roll(x, HEAD_DIM - quarter, 1), pltpu.roll(x, quarter, 1))
    return x * cos_t + partner * sin_t


def _attn_kernel(*refs, length, n_ctx, use_rope, has_cache_out, n_alias, qb):
    it = iter(refs)
    q_ref, k_ref, v_ref, qg_ref, kg_ref = (next(it) for _ in range(5))
    if n_ctx:
        ck_ref, cv_ref = next(it), next(it)
    if use_rope:
        cos_ref, sin_ref = next(it), next(it)
    for _ in range(n_alias):
        next(it)
    o_ref = next(it)
    kn_ref, vc_ref = (next(it), next(it)) if has_cache_out else (None, None)
    kall_s, vall_s, s_s = next(it), next(it), next(it)

    nq = length // qb
    rep = N_HEADS // N_KV_HEADS
    scale = HEAD_DIM ** -0.5

    for g in range(N_KV_HEADS):
        gs = slice(g * HEAD_DIM, (g + 1) * HEAD_DIM)
        kn = _rms(k_ref[:, gs], kg_ref[...])
        if kn_ref is not None:
            kn_ref[:, gs] = kn
            vc_ref[:, gs] = v_ref[:, gs]
        if use_rope:
            kn = _rope(kn, cos_ref[...], sin_ref[...])
        if n_ctx:
            kall_s[g, 0:n_ctx, :] = ck_ref[:, gs].astype(BF16)
            vall_s[g, 0:n_ctx, 0:HEAD_DIM] = cv_ref[:, gs].astype(BF16)
        kall_s[g, n_ctx:n_ctx + length, :] = kn.astype(BF16)
        vall_s[g, n_ctx:n_ctx + length, 0:HEAD_DIM] = v_ref[:, gs].astype(BF16)
        vall_s[g, :, HEAD_DIM:] = jnp.ones((n_ctx + length, HEAD_DIM), BF16)

    c_exp = scale * LOG2E

    def scores(i, slot):
        rows = pl.ds(pl.multiple_of(i * qb, qb), qb)
        for g in range(N_KV_HEADS):
            qs = []
            for r in range(rep):
                hs = slice((g * rep + r) * HEAD_DIM, (g * rep + r + 1) * HEAD_DIM)
                qn = _rms(q_ref[rows, hs], qg_ref[...])
                if use_rope:
                    qn = _rope(qn, cos_ref[rows, :], sin_ref[rows, :])
                qs.append(qn.astype(BF16))
            s_s[slot, g] = _dot_nt(jnp.concatenate(qs, axis=0), kall_s[g])

    def outputs(i, slot):
        rows = pl.ds(pl.multiple_of(i * qb, qb), qb)
        for g in range(N_KV_HEADS):
            s = s_s[slot, g]
            m = jnp.max(s, axis=-1, keepdims=True)
            p = jnp.exp2((s - m) * c_exp)
            ov = _dot(p.astype(BF16), vall_s[g])
            o = ov[:, :HEAD_DIM] / ov[:, HEAD_DIM:]
            for r in range(rep):
                hs = slice((g * rep + r) * HEAD_DIM, (g * rep + r + 1) * HEAD_DIM)
                o_ref[rows, hs] = o[r * qb:(r + 1) * qb].astype(o_ref.dtype)

    scores(0, 0)

    def pair(j, carry):
        i = 2 * j
        scores(i + 1, 1)
        outputs(i, 0)
        scores(jnp.minimum(i + 2, nq - 1), 0)
        outputs(i + 1, 1)
        return carry

    lax.fori_loop(0, nq // 2, pair, 0)


def _attn(group, q, k, v, qg, kg, i, ck_all, cv_all, cos_t, sin_t, o_prev, kc_prev, vc_prev,
          layer_name):
    prompt = group == "prompt"
    length = SEQ if prompt else DEC_SEQ
    nb = BATCH if prompt else DEC_BATCH
    off = 0 if prompt else N_PROMPT_TOK // DEC_SEQ
    kvw = N_KV_HEADS * HEAD_DIM
    n_ctx = 0 if prompt else PAST_LEN
    qb = 128
    rep = N_HEADS // N_KV_HEADS
    in_specs = _seq_specs(group, (D_MODEL, kvw, kvw))
    in_specs += [_const_spec(qg.shape), _const_spec(kg.shape)]
    args = [q, k, v, qg, kg]
    o_spec = pl.BlockSpec((length, D_MODEL), lambda b: (b + off, 0))
    o_shape = jax.ShapeDtypeStruct((N_TOK, D_MODEL), BF16)
    aliases = {}
    if prompt:
        c_spec = pl.BlockSpec((None, None, length, kvw), lambda b: (b, i, 0, 0))
        c_shape = jax.ShapeDtypeStruct((nb, N_ATTN_LAYERS, length, kvw), F32)
        out_specs = [o_spec, c_spec, c_spec]
        out_shape = [o_shape, c_shape, c_shape]
        if kc_prev is not None:
            in_specs += [pl.BlockSpec(memory_space=pl.ANY)] * 2
            args += [kc_prev, vc_prev]
            aliases = {len(args) - 2: 1, len(args) - 1: 2}
    else:
        ctx_spec = pl.BlockSpec((None, None, n_ctx, kvw), lambda b: (b, i, 0, 0))
        in_specs += [ctx_spec, ctx_spec, _const_spec(cos_t.shape), _const_spec(sin_t.shape),
                     pl.BlockSpec(memory_space=pl.ANY)]
        args += [ck_all, cv_all, cos_t, sin_t, o_prev]
        out_specs = [o_spec]
        out_shape = [o_shape]
        aliases = {len(args) - 1: 0}
    return pl.pallas_call(
        functools.partial(_attn_kernel, length=length, n_ctx=n_ctx, use_rope=not prompt,
                          has_cache_out=prompt, n_alias=len(aliases), qb=qb),
        grid=(nb,),
        in_specs=in_specs,
        out_specs=out_specs,
        out_shape=out_shape,
        scratch_shapes=[pltpu.VMEM((N_KV_HEADS, n_ctx + length, HEAD_DIM), BF16),
                        pltpu.VMEM((N_KV_HEADS, n_ctx + length, 2 * HEAD_DIM), BF16),
                        pltpu.VMEM((2, N_KV_HEADS, rep * qb, n_ctx + length), F32)],
        input_output_aliases=aliases,
        compiler_params=_cparams(1),
        name=f"attn_{group}_{layer_name}",
    )(*args)


def _rope_tables(rows):
    row_pos = jnp.repeat(jnp.arange(rows, dtype=F32), GRID_W)
    col_pos = jnp.tile(jnp.arange(GRID_W, dtype=F32), rows)
    inv_freq = jnp.power(ROPE_THETA, -jnp.arange(0, ROPE_AXIS_DIM, 2, dtype=F32) / ROPE_AXIS_DIM)
    ang_r = row_pos[:, None] * inv_freq
    ang_c = col_pos[:, None] * inv_freq
    cos_t = jnp.concatenate([jnp.cos(ang_r), jnp.cos(ang_r), jnp.cos(ang_c), jnp.cos(ang_c)], axis=1)
    sin_t = jnp.concatenate([-jnp.sin(ang_r), jnp.sin(ang_r), -jnp.sin(ang_c), jnp.sin(ang_c)], axis=1)
    return cos_t, sin_t


def _block_diag_tiles(w):
    per = MXU_DIM // LRU_BW
    n_tiles = LRU_BLOCKS // per
    w = w.reshape(2, n_tiles, per, LRU_BW, LRU_BW)
    eye = jnp.eye(per, dtype=w.dtype)
    tiles = jnp.einsum('dtpio,pq->dtpiqo', w, eye)
    return tiles.reshape(2, n_tiles, MXU_DIM, MXU_DIM).astype(BF16)


def _row(v):
    return v.reshape(1, -1)


def kernel(x_prompt, x_sample, state_ssd, state_lru, cache_k, cache_v, c, c_ctx,
           w_mod, b_mod, norm_g, w_in_ssm, ssd_conv_w, ssd_conv_b, ssd_a_log, ssd_dt_bias,
           ssd_d, ssd_norm_w, lru_conv_w, lru_conv_b, lru_wa, lru_ba, lru_wx, lru_bx,
           lru_lambda, w_out_ssm, w_in_ca, conf_dw_w, conf_dw_b, conf_ln_g, conf_ln_b,
           q_norm_g, k_norm_g, w_out_ca, ffn_w_in, ffn_conv_w, ffn_conv_b, ffn_w_out):
    x = (x_prompt.reshape(N_PROMPT_TOK, D_MODEL), x_sample.reshape(N_SAMPLE_TOK, D_MODEL))
    cvec = jnp.concatenate(
        [c_ctx[None], c, jnp.zeros((N_MOD_ROWS - 1 - DEC_BATCH, D_MODEL), F32)], axis=0)
    mods = _modulation_all(cvec, w_mod, b_mod)
    cos_t, sin_t = _rope_tables(DEC_SEQ // GRID_W)
    kvw = N_KV_HEADS * HEAD_DIM

    w_ssm_in = jnp.pad(w_in_ssm, ((0, 0), (0, 0), (0, DT_COL0))).astype(BF16)
    w_ssm_out = w_out_ssm.astype(BF16)
    w_ca_in = w_in_ca.astype(BF16)
    w_ca_out = w_out_ca.astype(BF16)
    w_ffn_in = ffn_w_in.astype(BF16)
    w_ffn_out = ffn_w_out.astype(BF16)
    ssd_h0 = state_ssd.reshape(DEC_BATCH, N_SSM_LAYERS, 2, SSD_INNER, SSD_STATE)
    ck_all = cache_k.reshape(DEC_BATCH, N_ATTN_LAYERS, PAST_LEN, kvw)
    cv_all = cache_v.reshape(DEC_BATCH, N_ATTN_LAYERS, PAST_LEN, kvw)

    ssd_st = lru_st = kc = vc = None
    for layer in range(DEPTH):
        i = layer // 2
        m = mods[layer]
        name = f"l{layer}"
        g0, g1, g2, g3 = (_row(norm_g[layer, j]) for j in range(4))
        if layer % 2 == 0:
            z, xbc, xl, gl, dt = _inproj(
                x, g0, m[0], m[1], w_ssm_in, i,
                (SSD_INNER, SSD_CONV_DIM, LRU_WIDTH, LRU_WIDTH, DT_PAD), f"inproj_ssm_{name}",
                rot=(SSM_O2, 2 * SSD_HEADS))
            dtb = jnp.pad(ssd_dt_bias[i].reshape(1, -1), ((0, 0), (DT_COL0, 0)))
            alog = jnp.pad(ssd_a_log[i].reshape(1, -1), ((0, 0), (DT_COL0, 0)))
            dskip = _row(jnp.repeat(ssd_d[i], SSD_HEADDIM))
            ssd_args = (ssd_conv_w[i], _row(ssd_conv_b[i]), dtb, alog, dskip, _row(ssd_norm_w[i]))
            y, ssd_st = _ssd("prompt", z, xbc, dt, *ssd_args, i, None, None, ssd_st, name)
            (y,) = _ssd("sample", z, xbc, dt, *ssd_args, i, ssd_h0, y, None, name)
            lru_args = (lru_conv_w[i], _row(lru_conv_b[i]), _block_diag_tiles(lru_wa[i]),
                        _block_diag_tiles(lru_wx[i]), lru_ba[i], lru_bx[i], lru_lambda[i])
            yl, lru_st = _lru("prompt", xl, gl, *lru_args, i, None, None, lru_st, name)
            (yl,) = _lru("sample", xl, gl, *lru_args, i, state_lru, yl, None, name)
            x = (_outproj(y, yl, w_ssm_out, i, x, g1, m[2], f"outproj_ssm_{name}"),)
        else:
            ga, gg, q, k, v = _inproj(
                x, g0, m[0], m[1], w_ca_in, i,
                (CONV_WIDTH, CONV_WIDTH, N_HEADS * HEAD_DIM, kvw, kvw), f"inproj_ca_{name}")
            conv_args = (conf_dw_w[i], _row(conf_dw_b[i]), _row(conf_ln_g[i]), _row(conf_ln_b[i]))
            cvo = _confconv("prompt", ga, gg, *conv_args, None, name)
            cvo = _confconv("sample", ga, gg, *conv_args, cvo, name)
            qg, kg = _row(q_norm_g[i]), _row(k_norm_g[i])
            o, kc, vc = _attn("prompt", q, k, v, qg, kg, i, None, None, None, None, None, kc, vc,
                              name)
            (o,) = _attn("sample", q, k, v, qg, kg, i, ck_all, cv_all, cos_t, sin_t, o, None, None,
                         name)
            x = (_outproj(cvo, o, w_ca_out, i, x, g1, m[2], f"outproj_ca_{name}"),)
        x = (_ffn(x[0], g2, m[3], m[4], w_ffn_in, ffn_conv_w[layer], _row(ffn_conv_b[layer]),
                  w_ffn_out, layer, g3, m[5], f"ffn_{name}"),)

    xp = x[0][:N_PROMPT_TOK].reshape(BATCH, SEQ, D_MODEL)
    xs = x[0][N_PROMPT_TOK:].reshape(DEC_BATCH, DEC_SEQ, D_MODEL)
    return (xp, xs,
            ssd_st.reshape(BATCH, N_SSM_LAYERS, 2, SSD_HEADS, SSD_HEADDIM, SSD_STATE),
            lru_st,
            kc.reshape(BATCH, N_ATTN_LAYERS, SEQ, N_KV_HEADS, HEAD_DIM),
            vc.reshape(BATCH, N_ATTN_LAYERS, SEQ, N_KV_HEADS, HEAD_DIM))
```

```python
import functools
import math

import jax
import jax.numpy as jnp
from jax import lax
from jax.experimental import pallas as pl
from jax.experimental.pallas import tpu as pltpu

F32 = jnp.float32
BF16 = jnp.bfloat16

D_MODEL = 1024
BATCH = 16
SEQ = 256
DEPTH = 4
N_SSM_LAYERS = (DEPTH + 1) // 2
N_ATTN_LAYERS = DEPTH // 2
DEC_BATCH = 4
DEC_SEQ = 1024
PAST_LEN = 512
GRID_W = 64
EPS = 1e-6
SSD_HEADDIM = 64
SSD_INNER = D_MODEL
SSD_HEADS = SSD_INNER // SSD_HEADDIM
SSD_GROUPS = 2
SSD_STATE = 128
SSD_CONV = 4
SSD_CHUNK = 128
SSD_CONV_DIM = SSD_INNER + 2 * SSD_GROUPS * SSD_STATE
LRU_WIDTH = D_MODEL
LRU_BW = 64
LRU_BLOCKS = LRU_WIDTH // LRU_BW
LRU_CONV = 4
LRU_C = 8.0
CONV_WIDTH = D_MODEL
CONF_K = 31
HEAD_DIM = 128
N_HEADS = D_MODEL // HEAD_DIM
N_KV_HEADS = 2
ROPE_THETA = 10000.0
ROPE_AXIS_DIM = HEAD_DIM // 2
D_FF = 2816
FFN_CONV = 3

N_PROMPT_TOK = BATCH * SEQ
N_SAMPLE_TOK = DEC_BATCH * DEC_SEQ
N_TOK = N_PROMPT_TOK + N_SAMPLE_TOK
N_MOD_ROWS = 8
LANES = 128
SUBLANES = 8
MXU_DIM = 256
DT_PAD = LANES
SSM_O2 = SSD_INNER + SSD_CONV_DIM
DT_COL0 = DT_PAD - 2 * SSD_HEADS
LOG2E = 1.0 / math.log(2.0)
VMEM_LIMIT = 58 * 1024 * 1024

TM_LINEAR = 512
TM_FFN = 1024


def _cparams(n_axes):
    return pltpu.CompilerParams(
        dimension_semantics=("arbitrary",) * n_axes,
        vmem_limit_bytes=VMEM_LIMIT)


def _const_spec(shape):
    nd = len(shape)
    return pl.BlockSpec(shape, lambda *_: (0,) * nd, pipeline_mode=pl.Buffered(1))


def _layer_spec(shape, layer):
    nd = len(shape) - 1
    return pl.BlockSpec((None,) + tuple(shape[1:]), lambda *_: (layer,) + (0,) * nd,
                        pipeline_mode=pl.Buffered(1))


def _mod_row(i, tm):
    start = i * tm
    return jnp.where(start < N_PROMPT_TOK, 0, 1 + (start - N_PROMPT_TOK) // DEC_SEQ)


def _mod_spec(tm):
    return pl.BlockSpec((None, 1, D_MODEL), lambda i: (_mod_row(i, tm), 0, 0))


def _x_specs(xs, tm):
    if len(xs) == 1:
        return [pl.BlockSpec((tm, D_MODEL), lambda i: (i, 0))]
    n_p = N_PROMPT_TOK // tm
    return [pl.BlockSpec((tm, D_MODEL), lambda i: (jnp.minimum(i, n_p - 1), 0)),
            pl.BlockSpec((tm, D_MODEL), lambda i: (jnp.maximum(i - n_p, 0), 0))]


def _x_tile(x_refs, tm):
    if len(x_refs) == 1:
        return x_refs[0][...]
    is_prompt = pl.program_id(0) * tm < N_PROMPT_TOK
    return jnp.where(is_prompt, x_refs[0][...], x_refs[1][...])


def _sigmoid(x):
    return jax.nn.sigmoid(x)


def _sigmoid_tanh(x):
    return 0.5 * jnp.tanh(0.5 * x) + 0.5


def _silu(x):
    return x * _sigmoid(x)


def _softplus(x):
    return jnp.maximum(x, 0.0) + jnp.log1p(jnp.exp(-jnp.abs(x)))


def _gelu_tanh(x):
    return 0.5 * x * (1.0 + jnp.tanh(math.sqrt(2.0 / math.pi) * (x + 0.044715 * (x * x * x))))


def _rms(x, g):
    ms = jnp.mean(x * x, axis=-1, keepdims=True)
    return (x * lax.rsqrt(ms + EPS)) * g


def _dot(a, b):
    return jnp.dot(a, b, preferred_element_type=F32)


def _dot_nt(a, b):
    return lax.dot_general(a, b, (((1,), (1,)), ((), ())), preferred_element_type=F32)


def _mod_kernel(c_ref, w_ref, b_ref, o_ref):
    c = c_ref[...]
    s = _silu(c).astype(BF16)
    o_ref[...] = _dot(s, w_ref[...].astype(BF16)) + b_ref[...]


def _modulation_all(cvec, w_mod, b_mod):
    tn = 1536
    n_out = 6 * D_MODEL
    out = pl.pallas_call(
        _mod_kernel,
        grid=(DEPTH, n_out // tn),
        in_specs=[
            pl.BlockSpec((N_MOD_ROWS, D_MODEL), lambda l, j: (0, 0)),
            pl.BlockSpec((None, D_MODEL, tn), lambda l, j: (l, 0, j)),
            pl.BlockSpec((None, 1, tn), lambda l, j: (l, 0, j)),
        ],
        out_specs=pl.BlockSpec((None, N_MOD_ROWS, tn), lambda l, j: (l, 0, j)),
        out_shape=jax.ShapeDtypeStruct((DEPTH, N_MOD_ROWS, n_out), F32),
        compiler_params=_cparams(2),
        name="modulation",
    )(cvec, w_mod, b_mod.reshape(DEPTH, 1, n_out))
    out = out.reshape(DEPTH, N_MOD_ROWS, 6, 1, D_MODEL)
    return jnp.transpose(out, (0, 2, 1, 3, 4))


def _inproj_kernel(*refs, n_x, widths, chunk, rot):
    x_refs = refs[:n_x]
    g_ref, shift_ref, scale_ref, w_ref = refs[n_x:n_x + 4]
    rest = refs[n_x + 4:]
    o_refs = rest[:len(widths)]
    h_ref = rest[len(widths)]
    if rot is not None:
        rot_start, rot_by = rot
        wr_ref = rest[len(widths) + 1]
        rot_w = w_ref.shape[1] - rot_start

        @pl.when(pl.program_id(0) == 0)
        def _():
            wr_ref[...] = pltpu.roll(w_ref[:, rot_start:], rot_w - rot_by, 1)
    elif w_ref.dtype != BF16:
        wb_ref = rest[len(widths) + 1]

        @pl.when(pl.program_id(0) == 0)
        def _():
            wb_ref[...] = w_ref[...].astype(BF16)
        w_ref = wb_ref

    x = _x_tile(x_refs, x_refs[0].shape[0])
    h = _rms(x, g_ref[...]) * (1.0 + scale_ref[...]) + shift_ref[...]
    h_ref[...] = h.astype(BF16)
    off = 0
    for o_ref, n in zip(o_refs, widths):
        for c0 in range(0, n, chunk):
            c1 = min(c0 + chunk, n)
            if rot is not None and off >= rot_start:
                w_blk = wr_ref[:, off - rot_start + c0:off - rot_start + c1]
            else:
                w_blk = w_ref[:, off + c0:off + c1]
            o_ref[:, c0:c1] = _dot(h_ref[...], w_blk)
        off += n


def _inproj(xs, g, shift, scale, w, layer, widths, name, rot=None):
    tm = TM_LINEAR
    assert sum(widths) == w.shape[2] and all(n % LANES == 0 for n in widths)
    scratch = [pltpu.VMEM((tm, D_MODEL), BF16)]
    if rot is not None:
        scratch.append(pltpu.VMEM((D_MODEL, w.shape[2] - rot[0]), BF16))
    elif w.dtype != BF16:
        scratch.append(pltpu.VMEM(w.shape[1:], BF16))
    return pl.pallas_call(
        functools.partial(_inproj_kernel, n_x=len(xs), widths=tuple(widths), chunk=512, rot=rot),
        grid=(N_TOK // tm,),
        in_specs=_x_specs(xs, tm) + [_const_spec((1, D_MODEL)), _mod_spec(tm), _mod_spec(tm),
                                     _layer_spec(w.shape, layer)],
        out_specs=[pl.BlockSpec((tm, n), lambda i: (i, 0)) for n in widths],
        out_shape=[jax.ShapeDtypeStruct((N_TOK, n), F32) for n in widths],
        scratch_shapes=scratch,
        compiler_params=_cparams(1),
        name=name,
    )(*xs, g, shift, scale, w)


def _outproj_kernel(a_ref, b_ref, w_ref, *rest, n_x):
    x_refs = rest[:n_x]
    g_ref, gate_ref, o_ref, wb_ref = rest[n_x:]

    @pl.when(pl.program_id(0) == 0)
    def _():
        wb_ref[...] = w_ref[...].astype(BF16)

    ka = a_ref.shape[1]
    acc = _dot(a_ref[...], wb_ref[0:ka, :]) + _dot(b_ref[...], wb_ref[ka:, :])
    o_ref[...] = _x_tile(x_refs, o_ref.shape[0]) + gate_ref[...] * _rms(acc, g_ref[...])


def _outproj(a, b, w, layer, xs, g, gate, name):
    tm = TM_LINEAR
    return pl.pallas_call(
        functools.partial(_outproj_kernel, n_x=len(xs)),
        grid=(N_TOK // tm,),
        in_specs=[
            pl.BlockSpec((tm, a.shape[1]), lambda i: (i, 0)),
            pl.BlockSpec((tm, b.shape[1]), lambda i: (i, 0)),
            _layer_spec(w.shape, layer),
        ] + _x_specs(xs, tm) + [_const_spec((1, D_MODEL)), _mod_spec(tm)],
        out_specs=pl.BlockSpec((tm, D_MODEL), lambda i: (i, 0)),
        out_shape=jax.ShapeDtypeStruct((N_TOK, D_MODEL), F32),
        scratch_shapes=[pltpu.VMEM(w.shape[1:], BF16)],
        compiler_params=_cparams(1),
        name=name,
    )(a, b, w, *xs, g, gate)


def _ffn_kernel(x_ref, g2_ref, shift_ref, scale_ref, wi_ref, cw_ref, cb_ref,
                wo_ref, g3_ref, gate_ref, o_ref, h_ref, act_ref, *, chunk, row_blk):
    tm = x_ref.shape[0]
    i = pl.program_id(0)
    h = _rms(x_ref[...], g2_ref[...]) * (1.0 + scale_ref[...]) + shift_ref[...]
    h_ref[...] = h.astype(BF16)
    lseq = jnp.where(i * tm < N_PROMPT_TOK, SEQ, DEC_SEQ)
    pos = lax.broadcasted_iota(jnp.int32, (tm, 1), 0) & (lseq - 1)
    first = pos == 0
    last = pos == lseq - 1
    for c0 in range(0, D_FF, chunk):
        c1 = min(c0 + chunk, D_FF)
        gt = _dot(h_ref[...], wi_ref[:, c0:c1])
        vl = _dot(h_ref[...], wi_ref[:, D_FF + c0:D_FF + c1])
        g_prev = jnp.where(first, 0.0, pltpu.roll(gt, 1, 0))
        g_next = jnp.where(last, 0.0, pltpu.roll(gt, tm - 1, 0))
        conv = (g_prev * cw_ref[0:1, c0:c1] + gt * cw_ref[1:2, c0:c1]
                + g_next * cw_ref[2:3, c0:c1] + cb_ref[:, c0:c1])
        act_ref[:, c0:c1] = (_silu(conv) * vl).astype(BF16)
    for r0 in range(0, tm, row_blk):
        rs = slice(r0, r0 + row_blk)
        acc = _dot(act_ref[rs, :], wo_ref[...])
        o_ref[rs, :] = x_ref[rs, :] + gate_ref[...] * _rms(acc, g3_ref[...])


def _ffn(x, g2, shift, scale, wi, cw, cb, wo, layer, g3, gate, name):
    tm = TM_FFN
    row_spec = pl.BlockSpec((tm, D_MODEL), lambda i: (i, 0))
    return pl.pallas_call(
        functools.partial(_ffn_kernel, chunk=2 * MXU_DIM, row_blk=MXU_DIM),
        grid=(N_TOK // tm,),
        in_specs=[
            row_spec, _const_spec((1, D_MODEL)), _mod_spec(tm), _mod_spec(tm),
            _layer_spec(wi.shape, layer), _const_spec(cw.shape),
            _const_spec(cb.shape), _layer_spec(wo.shape, layer), _const_spec((1, D_MODEL)),
            _mod_spec(tm),
        ],
        out_specs=row_spec,
        out_shape=jax.ShapeDtypeStruct((N_TOK, D_MODEL), F32),
        scratch_shapes=[pltpu.VMEM((tm, D_MODEL), BF16), pltpu.VMEM((tm, D_FF), BF16)],
        compiler_params=_cparams(1),
        name=name,
    )(x, g2, shift, scale, wi, cw, cb, wo, g3, gate)


def _seq_specs(group, widths):
    if group == "prompt":
        return [pl.BlockSpec((SEQ, w), lambda b: (b, 0)) for w in widths]
    off = N_PROMPT_TOK // DEC_SEQ
    return [pl.BlockSpec((DEC_SEQ, w), lambda b: (b + off, 0)) for w in widths]


def _short_conv_chunk(pad_ref, cw_ref, cb_ref, base, t, halo, taps, left, cols):
    win = pad_ref[pl.ds(base, t + 2 * halo), cols]
    n = t + 2 * halo
    acc = cb_ref[:, cols]
    for j in range(taps):
        s = (left - j) % n
        rolled = win if s == 0 else pltpu.roll(win, s, 0)
        acc = acc + rolled[halo:halo + t] * cw_ref[j:j + 1, cols]
    return acc


def _fill_padded(pad_ref, src, length, halo):
    width = pad_ref.shape[1]
    pad_ref[0:halo, :] = jnp.zeros((halo, width), F32)
    pad_ref[halo + length:2 * halo + length, :] = jnp.zeros((halo, width), F32)
    pad_ref[halo:halo + length, :] = src


def _lane_pairs(m, first_col, n_pairs, rows):
    lane = lax.broadcasted_iota(jnp.int32, (rows, LANES), 1)
    lo_half = lane < SSD_HEADDIM
    pieces = []
    for k in range(n_pairs):
        c = first_col + 2 * k
        lo = jnp.broadcast_to(m[:, c:c + 1], (rows, LANES))
        hi = jnp.broadcast_to(m[:, c + 1:c + 2], (rows, LANES))
        pieces.append(jnp.where(lo_half, lo, hi))
    return jnp.concatenate(pieces, axis=1)


def _ssd_kernel(*refs, length, has_h0, has_state_out, n_alias):
    it = iter(refs)
    z_ref, xbc_ref, dt_ref = next(it), next(it), next(it)
    cw_ref, cb_ref, dtb_ref, alog_ref, dskip_ref, nw_ref = (next(it) for _ in range(6))
    h0_ref = next(it) if has_h0 else None
    for _ in range(n_alias):
        next(it)
    y_ref = next(it)
    st_ref = next(it) if has_state_out else None
    pad_s, xs_s, bc_s, cum_s, row_t_s, w_t_s, cd_s, yacc_s, state_s = (
        next(it) for _ in range(9))

    t = SSD_CHUNK
    nc = length // t
    halo = SUBLANES
    gw = SSD_INNER // SSD_GROUPS
    pairs_per_group = SSD_HEADS // SSD_GROUPS // 2

    _fill_padded(pad_s, xbc_ref[...], length, halo)
    a_row = -jnp.exp(alog_ref[...])

    ri = lax.broadcasted_iota(jnp.int32, (t, t), 0)
    ci = lax.broadcasted_iota(jnp.int32, (t, t), 1)
    keep = (ci <= ri, ci >= ri)
    tril = keep[0].astype(F32)
    lane = lax.broadcasted_iota(jnp.int32, (t, LANES), 1)
    lo_half = lane < SSD_HEADDIM
    fwd_cols = lane < DT_COL0 + SSD_HEADS
    fwd_rows = ri < DT_COL0 + SSD_HEADS

    def prep(c, carry):
        base = pl.multiple_of(c * t, t)
        rows = pl.ds(base, t)
        for c0 in range(0, SSD_CONV_DIM, MXU_DIM):
            cols = slice(c0, c0 + MXU_DIM)
            conv = _silu(_short_conv_chunk(pad_s, cw_ref, cb_ref, base, t, halo, SSD_CONV, 2, cols))
            if c0 < SSD_INNER:
                xs_s[rows, cols] = conv
                yacc_s[rows, cols] = conv * dskip_ref[:, cols]
            else:
                bc_s[rows, c0 - SSD_INNER:c0 - SSD_INNER + MXU_DIM] = conv
        dtsp = _softplus(dt_ref[rows, :] + dtb_ref[...])
        a_c = dtsp * a_row
        pre = jnp.dot(tril, a_c, preferred_element_type=F32, precision=lax.Precision.HIGHEST)
        suf = pre[t - 1:t, :] - pre + a_c
        cum = jnp.where(fwd_cols, pre, suf) * LOG2E
        cum_s[rows, :] = cum
        cum_t = cum.T
        dt_t = dtsp.T
        edge_col = jnp.where(fwd_rows[:, 0:1], cum_t[:, t - 1:t], cum_t[:, 0:1])
        row_t_s[c] = cum_t - jnp.log(dt_t) * LOG2E
        w_t_s[c] = dt_t * jnp.exp2(edge_col - cum_t)
        edge_row = jnp.where(fwd_cols[0:1, :], cum[t - 1:t, :], cum[0:1, :])
        cd_s[c] = jnp.broadcast_to(jnp.exp2(edge_row), (SUBLANES, LANES))
        return carry

    lax.fori_loop(0, nc, prep, 0)

    for d in range(2):
        if has_h0:
            for k in range(SSD_INNER // LANES):
                ks = slice(k * LANES, (k + 1) * LANES)
                state_s[d, :, ks] = h0_ref[d, ks, :].T
        else:
            state_s[d] = jnp.zeros((SSD_STATE, SSD_INNER), F32)

    def block_diag(m):
        return jnp.concatenate([jnp.where(lo_half, m, 0.0).astype(BF16),
                                jnp.where(lo_half, 0.0, m).astype(BF16)], axis=0)

    def chunk_step(c, carry):
        for d in range(2):
            cidx = c if d == 0 else nc - 1 - c
            base = pl.multiple_of(cidx * t, t)
            rows = pl.ds(base, t)
            cum = cum_s[rows, :]
            bc = bc_s[rows, :]
            cd = cd_s[cidx][0:1, :]
            for g in range(SSD_GROUPS):
                col0 = DT_COL0 + d * SSD_HEADS + g * 2 * pairs_per_group
                b_g = bc[:, g * SSD_STATE:(g + 1) * SSD_STATE]
                c_g = bc[:, (SSD_GROUPS + g) * SSD_STATE:(SSD_GROUPS + g + 1) * SSD_STATE]
                gmat = _dot_nt(c_g.astype(BF16), b_g.astype(BF16))
                b_t = b_g.T
                cd_rep = _lane_pairs(cd, col0, pairs_per_group, 1)
                for kk in range(pairs_per_group):
                    lanes = slice(g * gw + kk * LANES, g * gw + (kk + 1) * LANES)
                    rhs_x = block_diag(xs_s[rows, lanes])
                    st = state_s[d, :, lanes]
                    gl, ce, bw = [], [], []
                    for hcur in (col0 + 2 * kk, col0 + 2 * kk + 1):
                        hrow = pl.ds(hcur, 1)
                        colb = jnp.broadcast_to(cum[:, hcur:hcur + 1], (t, t))
                        rowb = jnp.broadcast_to(row_t_s[cidx, hrow, :], (t, t))
                        wrow = jnp.broadcast_to(w_t_s[cidx, hrow, :], (t, t))
                        lmat = jnp.exp2(jnp.where(keep[d], colb - rowb, -jnp.inf))
                        gl.append((gmat * lmat).astype(BF16))
                        ce.append((c_g * jnp.exp2(colb)).astype(BF16))
                        bw.append((b_t * wrow).astype(BF16))
                    y_p = _dot(jnp.concatenate(gl + ce, axis=1),
                               jnp.concatenate([rhs_x, block_diag(st)], axis=0))
                    state_s[d, :, lanes] = (st * cd_rep[:, kk * LANES:(kk + 1) * LANES]
                                            + _dot(jnp.concatenate(bw, axis=1), rhs_x))
                    yacc_s[rows, lanes] += y_p
        return carry

    lax.fori_loop(0, nc, chunk_step, 0)

    def finish(c, carry):
        base = pl.multiple_of(c * t, t)
        rows = pl.ds(base, t)
        y = yacc_s[rows, :] * _silu(z_ref[rows, :])
        y_ref[rows, :] = _rms(y, nw_ref[...]).astype(y_ref.dtype)
        return carry

    lax.fori_loop(0, nc, finish, 0)

    if has_state_out:
        for d in range(2):
            for k in range(SSD_INNER // LANES):
                ks = slice(k * LANES, (k + 1) * LANES)
                st_ref[d, ks, :] = state_s[d, :, ks].T


def _ssd(group, z, xbc, dt, cw, cb, dtb, alog, dskip, nw, i, h0_all, y_prev, st_prev, layer_name):
    prompt = group == "prompt"
    length = SEQ if prompt else DEC_SEQ
    nb = BATCH if prompt else DEC_BATCH
    off = 0 if prompt else N_PROMPT_TOK // DEC_SEQ
    in_specs = _seq_specs(group, (SSD_INNER, SSD_CONV_DIM, DT_PAD))
    in_specs += [_const_spec(a.shape) for a in (cw, cb, dtb, alog, dskip, nw)]
    args = [z, xbc, dt, cw, cb, dtb, alog, dskip, nw]
    y_spec = pl.BlockSpec((length, SSD_INNER), lambda b: (b + off, 0))
    y_shape = jax.ShapeDtypeStruct((N_TOK, SSD_INNER), BF16)
    st_block = (None, None, 2, SSD_INNER, SSD_STATE)
    aliases = {}
    if prompt:
        out_specs = [y_spec, pl.BlockSpec(st_block, lambda b: (b, i, 0, 0, 0))]
        out_shape = [y_shape, jax.ShapeDtypeStruct((nb, N_SSM_LAYERS, 2, SSD_INNER, SSD_STATE), F32)]
        if st_prev is not None:
            in_specs.append(pl.BlockSpec(memory_space=pl.ANY))
            args.append(st_prev)
            aliases = {len(args) - 1: 1}
    else:
        in_specs += [pl.BlockSpec(st_block, lambda b: (b, i, 0, 0, 0)),
                     pl.BlockSpec(memory_space=pl.ANY)]
        args += [h0_all, y_prev]
        out_specs = [y_spec]
        out_shape = [y_shape]
        aliases = {len(args) - 1: 0}
    halo = SUBLANES
    nc = length // SSD_CHUNK
    scratch = [
        pltpu.VMEM((length + 2 * halo, SSD_CONV_DIM), F32),
        pltpu.VMEM((length, SSD_INNER), F32),
        pltpu.VMEM((length, 2 * SSD_GROUPS * SSD_STATE), F32),
        pltpu.VMEM((length, DT_PAD), F32),
        pltpu.VMEM((nc, DT_PAD, SSD_CHUNK), F32),
        pltpu.VMEM((nc, DT_PAD, SSD_CHUNK), F32),
        pltpu.VMEM((nc, SUBLANES, DT_PAD), F32),
        pltpu.VMEM((length, SSD_INNER), F32),
        pltpu.VMEM((2, SSD_STATE, SSD_INNER), F32),
    ]
    return pl.pallas_call(
        functools.partial(_ssd_kernel, length=length, has_h0=not prompt, has_state_out=prompt,
                          n_alias=len(aliases)),
        grid=(nb,),
        in_specs=in_specs,
        out_specs=out_specs,
        out_shape=out_shape,
        scratch_shapes=scratch,
        input_output_aliases=aliases,
        compiler_params=_cparams(1),
        name=f"ssd_{group}_{layer_name}",
    )(*args)


def _lru_kernel(*refs, length, has_h0, has_state_out, n_alias):
    it = iter(refs)
    xl_ref, gl_ref = next(it), next(it)
    cw_ref, cb_ref, wa_ref, wx_ref, ba_ref, bx_ref, lam_ref = (next(it) for _ in range(7))
    h0_ref = next(it) if has_h0 else None
    for _ in range(n_alias):
        next(it)
    o_ref = next(it)
    st_ref = next(it) if has_state_out else None
    pad_s, xc_s, a_s, u_s, h_s = (next(it) for _ in range(5))

    t = 128
    nc = length // t
    halo = SUBLANES
    n_tiles = LRU_WIDTH // MXU_DIM

    _fill_padded(pad_s, xl_ref[...], length, halo)

    def prep(c, carry):
        base = pl.multiple_of(c * t, t)
        for c0 in range(0, LRU_WIDTH, MXU_DIM):
            cols = slice(c0, c0 + MXU_DIM)
            xc_s[pl.ds(base, t), cols] = _short_conv_chunk(
                pad_s, cw_ref, cb_ref, base, t, halo, LRU_CONV, 2, cols)
        return carry

    lax.fori_loop(0, nc, prep, 0)

    row8 = lax.broadcasted_iota(jnp.int32, (SUBLANES, LRU_WIDTH), 0)
    n_groups = length // SUBLANES

    for d in range(2):
        log_a_unit = (-LRU_C) * _softplus(-lam_ref[d:d + 1, :])

        def gates(c, carry, d=d, log_a_unit=log_a_unit):
            base = pl.multiple_of(c * t, t)
            rows = pl.ds(base, t)
            xc = xc_s[rows, :]
            xb = xc.astype(BF16)
            ra, ri = [], []
            for j in range(n_tiles):
                js = slice(j * MXU_DIM, (j + 1) * MXU_DIM)
                ra.append(_dot(xb[:, js], wa_ref[d, j]))
                ri.append(_dot(xb[:, js], wx_ref[d, j]))
            r = _sigmoid_tanh(jnp.concatenate(ra, axis=1) + ba_ref[d:d + 1, :])
            gi = _sigmoid_tanh(jnp.concatenate(ri, axis=1) + bx_ref[d:d + 1, :])
            log_a = r * log_a_unit
            a = jnp.exp2(r * (log_a_unit * LOG2E))
            gap = -jnp.tanh(log_a) * (a * a + 1.0)
            root = jnp.where(gap > 0.0, gap * lax.rsqrt(gap), 0.0)
            a_s[rows, :] = a
            u_s[rows, :] = root * gi * xc
            return carry

        lax.fori_loop(0, nc, gates, 0)

        if has_h0:
            carry0 = jnp.broadcast_to(h0_ref[d:d + 1, :], (SUBLANES, LRU_WIDTH))
        else:
            carry0 = jnp.zeros((SUBLANES, LRU_WIDTH), F32)

        def scan(gi_, carry, d=d):
            g = gi_ if d == 0 else n_groups - 1 - gi_
            base = pl.multiple_of(g * SUBLANES, SUBLANES)
            rows = pl.ds(base, SUBLANES)
            av = a_s[rows, :]
            uv = u_s[rows, :]
            for k in (1, 2, 4):
                if d == 0:
                    shift, valid = k, row8 >= k
                else:
                    shift, valid = SUBLANES - k, row8 < SUBLANES - k
                a_sh = pltpu.roll(av, shift, 0)
                u_sh = pltpu.roll(uv, shift, 0)
                uv = jnp.where(valid, av * u_sh + uv, uv)
                av = jnp.where(valid, av * a_sh, av)
            h = av * carry + uv
            if d == 0:
                h_s[rows, :] = h
                edge = h[SUBLANES - 1:SUBLANES, :]
            else:
                h_s[rows, :] += h
                edge = h[0:1, :]
            return jnp.broadcast_to(edge, (SUBLANES, LRU_WIDTH))

        final = lax.fori_loop(0, n_groups, scan, carry0, unroll=4)
        if has_state_out:
            st_ref[d:d + 1, :] = final[0:1, :]

    def finish(c, carry):
        base = pl.multiple_of(c * t, t)
        rows = pl.ds(base, t)
        o_ref[rows, :] = (_gelu_tanh(gl_ref[rows, :]) * h_s[rows, :]).astype(o_ref.dtype)
        return carry

    lax.fori_loop(0, nc, finish, 0)


def _lru(group, xl, gl, cw, cb, wa, wx, ba, bx, lam, i, h0_all, o_prev, st_prev, layer_name):
    prompt = group == "prompt"
    length = SEQ if prompt else DEC_SEQ
    nb = BATCH if prompt else DEC_BATCH
    off = 0 if prompt else N_PROMPT_TOK // DEC_SEQ
    in_specs = _seq_specs(group, (LRU_WIDTH, LRU_WIDTH))
    in_specs += [_const_spec(a.shape) for a in (cw, cb, wa, wx, ba, bx, lam)]
    args = [xl, gl, cw, cb, wa, wx, ba, bx, lam]
    o_spec = pl.BlockSpec((length, LRU_WIDTH), lambda b: (b + off, 0))
    o_shape = jax.ShapeDtypeStruct((N_TOK, LRU_WIDTH), BF16)
    st_block = (None, None, 2, LRU_WIDTH)
    aliases = {}
    if prompt:
        out_specs = [o_spec, pl.BlockSpec(st_block, lambda b: (b, i, 0, 0))]
        out_shape = [o_shape, jax.ShapeDtypeStruct((nb, N_SSM_LAYERS, 2, LRU_WIDTH), F32)]
        if st_prev is not None:
            in_specs.append(pl.BlockSpec(memory_space=pl.ANY))
            args.append(st_prev)
            aliases = {len(args) - 1: 1}
    else:
        in_specs += [pl.BlockSpec(st_block, lambda b: (b, i, 0, 0)),
                     pl.BlockSpec(memory_space=pl.ANY)]
        args += [h0_all, o_prev]
        out_specs = [o_spec]
        out_shape = [o_shape]
        aliases = {len(args) - 1: 0}
    halo = SUBLANES
    scratch = [pltpu.VMEM((length + 2 * halo, LRU_WIDTH), F32)]
    scratch += [pltpu.VMEM((length, LRU_WIDTH), F32) for _ in range(4)]
    return pl.pallas_call(
        functools.partial(_lru_kernel, length=length, has_h0=not prompt, has_state_out=prompt,
                          n_alias=len(aliases)),
        grid=(nb,),
        in_specs=in_specs,
        out_specs=out_specs,
        out_shape=out_shape,
        scratch_shapes=scratch,
        input_output_aliases=aliases,
        compiler_params=_cparams(1),
        name=f"lru_{group}_{layer_name}",
    )(*args)


def _confconv_kernel(*refs, length, aliased):
    it = iter(refs)
    a_ref, g_ref, w_ref, b_ref, lng_ref, lnb_ref = (next(it) for _ in range(6))
    if aliased:
        next(it)
    o_ref = next(it)
    pad_s, acc_s = next(it), next(it)

    t = 128
    nc = length // t
    halo = 2 * SUBLANES
    left = (CONF_K - 1) // 2
    cblk = MXU_DIM
    n = t + 2 * halo

    _fill_padded(pad_s, a_ref[...] * _sigmoid_tanh(g_ref[...]), length, halo)

    def step(c, carry):
        base = pl.multiple_of(c * t, t)
        rows = pl.ds(base, t)
        for cb0 in range(0, CONV_WIDTH, cblk):
            cs = slice(cb0, cb0 + cblk)
            win = pad_s[pl.ds(base, n), cs]
            acc = jnp.broadcast_to(b_ref[:, cs], (t, cblk))
            for s in range(SUBLANES):
                shifted = win if s == 0 else pltpu.roll(win, n - s, 0)
                for m in range(n // SUBLANES):
                    j = SUBLANES * m + s - halo + left
                    if 0 <= j < CONF_K and SUBLANES * m + t <= n:
                        acc = acc + shifted[SUBLANES * m:SUBLANES * m + t] * w_ref[j:j + 1, cs]
            acc_s[:, cs] = acc
        cv = acc_s[...]
        mu = jnp.mean(cv, axis=-1, keepdims=True)
        xc = cv - mu
        var = jnp.mean(xc * xc, axis=-1, keepdims=True)
        y = (xc * lax.rsqrt(var + EPS)) * lng_ref[...] + lnb_ref[...]
        o_ref[rows, :] = _silu(y).astype(o_ref.dtype)
        return carry

    lax.fori_loop(0, nc, step, 0)


def _confconv(group, ga, gg, w, b, lng, lnb, o_prev, layer_name):
    prompt = group == "prompt"
    length = SEQ if prompt else DEC_SEQ
    nb = BATCH if prompt else DEC_BATCH
    off = 0 if prompt else N_PROMPT_TOK // DEC_SEQ
    in_specs = _seq_specs(group, (CONV_WIDTH, CONV_WIDTH))
    in_specs += [_const_spec(a.shape) for a in (w, b, lng, lnb)]
    args = [ga, gg, w, b, lng, lnb]
    aliases = {}
    if not prompt:
        in_specs.append(pl.BlockSpec(memory_space=pl.ANY))
        args.append(o_prev)
        aliases = {len(args) - 1: 0}
    halo = 2 * SUBLANES
    return pl.pallas_call(
        functools.partial(_confconv_kernel, length=length, aliased=not prompt),
        grid=(nb,),
        in_specs=in_specs,
        out_specs=pl.BlockSpec((length, CONV_WIDTH), lambda b: (b + off, 0)),
        out_shape=jax.ShapeDtypeStruct((N_TOK, CONV_WIDTH), BF16),
        scratch_shapes=[pltpu.VMEM((length + 2 * halo, CONV_WIDTH), F32),
                        pltpu.VMEM((128, CONV_WIDTH), F32)],
        input_output_aliases=aliases,
        compiler_params=_cparams(1),
        name=f"confconv_{group}_{layer_name}",
    )(*args)


def _rope(x, cos_t, sin_t):
    lane = lax.broadcasted_iota(jnp.int32, x.shape, 1)
    quarter = ROPE_AXIS_DIM // 2
    partner = jnp.where((lane & quarter) == 0,
                        pltpu.roll(x, HEAD_DIM - quarter, 1), pltpu.roll(x, quarter, 1))
    return x * cos_t + partner * sin_t


def _attn_kernel(*refs, length, n_ctx, use_rope, has_cache_out, n_alias, qb):
    it = iter(refs)
    q_ref, k_ref, v_ref, qg_ref, kg_ref = (next(it) for _ in range(5))
    if n_ctx:
        ck_ref, cv_ref = next(it), next(it)
    if use_rope:
        cos_ref, sin_ref = next(it), next(it)
    for _ in range(n_alias):
        next(it)
    o_ref = next(it)
    kn_ref, vc_ref = (next(it), next(it)) if has_cache_out else (None, None)
    kall_s, vall_s, s_s = next(it), next(it), next(it)

    nq = length // qb
    rep = N_HEADS // N_KV_HEADS
    scale = HEAD_DIM ** -0.5

    for g in range(N_KV_HEADS):
        gs = slice(g * HEAD_DIM, (g + 1) * HEAD_DIM)
        kn = _rms(k_ref[:, gs], kg_ref[...])
        if kn_ref is not None:
            kn_ref[:, gs] = kn
            vc_ref[:, gs] = v_ref[:, gs]
        if use_rope:
            kn = _rope(kn, cos_ref[...], sin_ref[...])
        if n_ctx:
            kall_s[g, 0:n_ctx, :] = ck_ref[:, gs].astype(BF16)
            vall_s[g, 0:n_ctx, 0:HEAD_DIM] = cv_ref[:, gs].astype(BF16)
        kall_s[g, n_ctx:n_ctx + length, :] = kn.astype(BF16)
        vall_s[g, n_ctx:n_ctx + length, 0:HEAD_DIM] = v_ref[:, gs].astype(BF16)
        vall_s[g, :, HEAD_DIM:] = jnp.ones((n_ctx + length, HEAD_DIM), BF16)

    c_exp = scale * LOG2E

    def scores(i, slot):
        rows = pl.ds(pl.multiple_of(i * qb, qb), qb)
        for g in range(N_KV_HEADS):
            qs = []
            for r in range(rep):
                hs = slice((g * rep + r) * HEAD_DIM, (g * rep + r + 1) * HEAD_DIM)
                qn = _rms(q_ref[rows, hs], qg_ref[...])
                if use_rope:
                    qn = _rope(qn, cos_ref[rows, :], sin_ref[rows, :])
                qs.append(qn.astype(BF16))
            s_s[slot, g] = _dot_nt(jnp.concatenate(qs, axis=0), kall_s[g])

    def outputs(i, slot):
        rows = pl.ds(pl.multiple_of(i * qb, qb), qb)
        for g in range(N_KV_HEADS):
            s = s_s[slot, g]
            m = jnp.max(s, axis=-1, keepdims=True)
            p = jnp.exp2((s - m) * c_exp)
            ov = _dot(p.astype(BF16), vall_s[g])
            o = ov[:, :HEAD_DIM] / ov[:, HEAD_DIM:]
            for r in range(rep):
                hs = slice((g * rep + r) * HEAD_DIM, (g * rep + r + 1) * HEAD_DIM)
                o_ref[rows, hs] = o[r * qb:(r + 1) * qb].astype(o_ref.dtype)

    scores(0, 0)

    def pair(j, carry):
        i = 2 * j
        scores(i + 1, 1)
        outputs(i, 0)
        scores(jnp.minimum(i + 2, nq - 1), 0)
        outputs(i + 1, 1)
        return carry

    lax.fori_loop(0, nq // 2, pair, 0)


def _attn(group, q, k, v, qg, kg, i, ck_all, cv_all, cos_t, sin_t, o_prev, kc_prev, vc_prev,
          layer_name):
    prompt = group == "prompt"
    length = SEQ if prompt else DEC_SEQ
    nb = BATCH if prompt else DEC_BATCH
    off = 0 if prompt else N_PROMPT_TOK // DEC_SEQ
    kvw = N_KV_HEADS * HEAD_DIM
    n_ctx = 0 if prompt else PAST_LEN
    qb = 128
    rep = N_HEADS // N_KV_HEADS
    in_specs = _seq_specs(group, (D_MODEL, kvw, kvw))
    in_specs += [_const_spec(qg.shape), _const_spec(kg.shape)]
    args = [q, k, v, qg, kg]
    o_spec = pl.BlockSpec((length, D_MODEL), lambda b: (b + off, 0))
    o_shape = jax.ShapeDtypeStruct((N_TOK, D_MODEL), BF16)
    aliases = {}
    if prompt:
        c_spec = pl.BlockSpec((None, None, length, kvw), lambda b: (b, i, 0, 0))
        c_shape = jax.ShapeDtypeStruct((nb, N_ATTN_LAYERS, length, kvw), F32)
        out_specs = [o_spec, c_spec, c_spec]
        out_shape = [o_shape, c_shape, c_shape]
        if kc_prev is not None:
            in_specs += [pl.BlockSpec(memory_space=pl.ANY)] * 2
            args += [kc_prev, vc_prev]
            aliases = {len(args) - 2: 1, len(args) - 1: 2}
    else:
        ctx_spec = pl.BlockSpec((None, None, n_ctx, kvw), lambda b: (b, i, 0, 0))
        in_specs += [ctx_spec, ctx_spec, _const_spec(cos_t.shape), _const_spec(sin_t.shape),
                     pl.BlockSpec(memory_space=pl.ANY)]
        args += [ck_all, cv_all, cos_t, sin_t, o_prev]
        out_specs = [o_spec]
        out_shape = [o_shape]
        aliases = {len(args) - 1: 0}
    return pl.pallas_call(
        functools.partial(_attn_kernel, length=length, n_ctx=n_ctx, use_rope=not prompt,
                          has_cache_out=prompt, n_alias=len(aliases), qb=qb),
        grid=(nb,),
        in_specs=in_specs,
        out_specs=out_specs,
        out_shape=out_shape,
        scratch_shapes=[pltpu.VMEM((N_KV_HEADS, n_ctx + length, HEAD_DIM), BF16),
                        pltpu.VMEM((N_KV_HEADS, n_ctx + length, 2 * HEAD_DIM), BF16),
                        pltpu.VMEM((2, N_KV_HEADS, rep * qb, n_ctx + length), F32)],
        input_output_aliases=aliases,
        compiler_params=_cparams(1),
        name=f"attn_{group}_{layer_name}",
    )(*args)


def _rope_tables(rows):
    row_pos = jnp.repeat(jnp.arange(rows, dtype=F32), GRID_W)
    col_pos = jnp.tile(jnp.arange(GRID_W, dtype=F32), rows)
    inv_freq = jnp.power(ROPE_THETA, -jnp.arange(0, ROPE_AXIS_DIM, 2, dtype=F32) / ROPE_AXIS_DIM)
    ang_r = row_pos[:, None] * inv_freq
    ang_c = col_pos[:, None] * inv_freq
    cos_t = jnp.concatenate([jnp.cos(ang_r), jnp.cos(ang_r), jnp.cos(ang_c), jnp.cos(ang_c)], axis=1)
    sin_t = jnp.concatenate([-jnp.sin(ang_r), jnp.sin(ang_r), -jnp.sin(ang_c), jnp.sin(ang_c)], axis=1)
    return cos_t, sin_t


def _block_diag_tiles(w):
    per = MXU_DIM // LRU_BW
    n_tiles = LRU_BLOCKS // per
    w = w.reshape(2, n_tiles, per, LRU_BW, LRU_BW)
    eye = jnp.eye(per, dtype=w.dtype)
    tiles = jnp.einsum('dtpio,pq->dtpiqo', w, eye)
    return tiles.reshape(2, n_tiles, MXU_DIM, MXU_DIM).astype(BF16)


def _row(v):
    return v.reshape(1, -1)


def kernel(x_prompt, x_sample, state_ssd, state_lru, cache_k, cache_v, c, c_ctx,
           w_mod, b_mod, norm_g, w_in_ssm, ssd_conv_w, ssd_conv_b, ssd_a_log, ssd_dt_bias,
           ssd_d, ssd_norm_w, lru_conv_w, lru_conv_b, lru_wa, lru_ba, lru_wx, lru_bx,
           lru_lambda, w_out_ssm, w_in_ca, conf_dw_w, conf_dw_b, conf_ln_g, conf_ln_b,
           q_norm_g, k_norm_g, w_out_ca, ffn_w_in, ffn_conv_w, ffn_conv_b, ffn_w_out):
    x = (x_prompt.reshape(N_PROMPT_TOK, D_MODEL), x_sample.reshape(N_SAMPLE_TOK, D_MODEL))
    cvec = jnp.concatenate(
        [c_ctx[None], c, jnp.zeros((N_MOD_ROWS - 1 - DEC_BATCH, D_MODEL), F32)], axis=0)
    mods = _modulation_all(cvec, w_mod, b_mod)
    cos_t, sin_t = _rope_tables(DEC_SEQ // GRID_W)
    kvw = N_KV_HEADS * HEAD_DIM

    w_ssm_in = jnp.pad(w_in_ssm, ((0, 0), (0, 0), (0, DT_COL0))).astype(BF16)
    w_ffn_in = ffn_w_in.astype(BF16)
    w_ffn_out = ffn_w_out.astype(BF16)
    ssd_h0 = state_ssd.reshape(DEC_BATCH, N_SSM_LAYERS, 2, SSD_INNER, SSD_STATE)
    ck_all = cache_k.reshape(DEC_BATCH, N_ATTN_LAYERS, PAST_LEN, kvw)
    cv_all = cache_v.reshape(DEC_BATCH, N_ATTN_LAYERS, PAST_LEN, kvw)

    ssd_st = lru_st = kc = vc = None
    for layer in range(DEPTH):
        i = layer // 2
        m = mods[layer]
        name = f"l{layer}"
        g0, g1, g2, g3 = (_row(norm_g[layer, j]) for j in range(4))
        if layer % 2 == 0:
            z, xbc, xl, gl, dt = _inproj(
                x, g0, m[0], m[1], w_ssm_in, i,
                (SSD_INNER, SSD_CONV_DIM, LRU_WIDTH, LRU_WIDTH, DT_PAD), f"inproj_ssm_{name}",
                rot=(SSM_O2, 2 * SSD_HEADS))
            dtb = jnp.pad(ssd_dt_bias[i].reshape(1, -1), ((0, 0), (DT_COL0, 0)))
            alog = jnp.pad(ssd_a_log[i].reshape(1, -1), ((0, 0), (DT_COL0, 0)))
            dskip = _row(jnp.repeat(ssd_d[i], SSD_HEADDIM))
            ssd_args = (ssd_conv_w[i], _row(ssd_conv_b[i]), dtb, alog, dskip, _row(ssd_norm_w[i]))
            y, ssd_st = _ssd("prompt", z, xbc, dt, *ssd_args, i, None, None, ssd_st, name)
            (y,) = _ssd("sample", z, xbc, dt, *ssd_args, i, ssd_h0, y, None, name)
            lru_args = (lru_conv_w[i], _row(lru_conv_b[i]), _block_diag_tiles(lru_wa[i]),
                        _block_diag_tiles(lru_wx[i]), lru_ba[i], lru_bx[i], lru_lambda[i])
            yl, lru_st = _lru("prompt", xl, gl, *lru_args, i, None, None, lru_st, name)
            (yl,) = _lru("sample", xl, gl, *lru_args, i, state_lru, yl, None, name)
            x = (_outproj(y, yl, w_out_ssm, i, x, g1, m[2], f"outproj_ssm_{name}"),)
        else:
            ga, gg, q, k, v = _inproj(
                x, g0, m[0], m[1], w_in_ca, i,
                (CONV_WIDTH, CONV_WIDTH, N_HEADS * HEAD_DIM, kvw, kvw), f"inproj_ca_{name}")
            conv_args = (conf_dw_w[i], _row(conf_dw_b[i]), _row(conf_ln_g[i]), _row(conf_ln_b[i]))
            cvo = _confconv("prompt", ga, gg, *conv_args, None, name)
            cvo = _confconv("sample", ga, gg, *conv_args, cvo, name)
            qg, kg = _row(q_norm_g[i]), _row(k_norm_g[i])
            o, kc, vc = _attn("prompt", q, k, v, qg, kg, i, None, None, None, None, None, kc, vc,
                              name)
            (o,) = _attn("sample", q, k, v, qg, kg, i, ck_all, cv_all, cos_t, sin_t, o, None, None,
                         name)
            x = (_outproj(cvo, o, w_out_ca, i, x, g1, m[2], f"outproj_ca_{name}"),)
        x = (_ffn(x[0], g2, m[3], m[4], w_ffn_in, ffn_conv_w[layer], _row(ffn_conv_b[layer]),
                  w_ffn_out, layer, g3, m[5], f"ffn_{name}"),)

    xp = x[0][:N_PROMPT_TOK].reshape(BATCH, SEQ, D_MODEL)
    xs = x[0][N_PROMPT_TOK:].reshape(DEC_BATCH, DEC_SEQ, D_MODEL)
    return (xp, xs,
            ssd_st.reshape(BATCH, N_SSM_LAYERS, 2, SSD_HEADS, SSD_HEADDIM, SSD_STATE),
            lru_st,
            kc.reshape(BATCH, N_ATTN_LAYERS, SEQ, N_KV_HEADS, HEAD_DIM),
            vc.reshape(BATCH, N_ATTN_LAYERS, SEQ, N_KV_HEADS, HEAD_DIM))
```

```python
import functools
import math

import jax
import jax.numpy as jnp
from jax import lax
from jax.experimental import pallas as pl
from jax.experimental.pallas import tpu as pltpu

F32 = jnp.float32
BF16 = jnp.bfloat16

D_MODEL = 1024
BATCH = 16
SEQ = 256
DEPTH = 4
N_SSM_LAYERS = (DEPTH + 1) // 2
N_ATTN_LAYERS = DEPTH // 2
DEC_BATCH = 4
DEC_SEQ = 1024
PAST_LEN = 512
GRID_W = 64
EPS = 1e-6
SSD_HEADDIM = 64
SSD_INNER = D_MODEL
SSD_HEADS = SSD_INNER // SSD_HEADDIM
SSD_GROUPS = 2
SSD_STATE = 128
SSD_CONV = 4
SSD_CHUNK = 128
SSD_CONV_DIM = SSD_INNER + 2 * SSD_GROUPS * SSD_STATE
LRU_WIDTH = D_MODEL
LRU_BW = 64
LRU_BLOCKS = LRU_WIDTH // LRU_BW
LRU_CONV = 4
LRU_C = 8.0
CONV_WIDTH = D_MODEL
CONF_K = 31
HEAD_DIM = 128
N_HEADS = D_MODEL // HEAD_DIM
N_KV_HEADS = 2
ROPE_THETA = 10000.0
ROPE_AXIS_DIM = HEAD_DIM // 2
D_FF = 2816
FFN_CONV = 3

N_PROMPT_TOK = BATCH * SEQ
N_SAMPLE_TOK = DEC_BATCH * DEC_SEQ
N_TOK = N_PROMPT_TOK + N_SAMPLE_TOK
N_MOD_ROWS = 8
LANES = 128
SUBLANES = 8
MXU_DIM = 256
DT_PAD = LANES
SSM_O2 = SSD_INNER + SSD_CONV_DIM
DT_COL0 = DT_PAD - 2 * SSD_HEADS
LOG2E = 1.0 / math.log(2.0)
VMEM_LIMIT = 58 * 1024 * 1024

TM_LINEAR = 512
TM_FFN = 1024


def _cparams(n_axes):
    return pltpu.CompilerParams(
        dimension_semantics=("arbitrary",) * n_axes,
        vmem_limit_bytes=VMEM_LIMIT)


def _const_spec(shape):
    nd = len(shape)
    return pl.BlockSpec(shape, lambda *_: (0,) * nd, pipeline_mode=pl.Buffered(1))


def _layer_spec(shape, layer):
    nd = len(shape) - 1
    return pl.BlockSpec((None,) + tuple(shape[1:]), lambda *_: (layer,) + (0,) * nd,
                        pipeline_mode=pl.Buffered(1))


def _mod_row(i, tm):
    start = i * tm
    return jnp.where(start < N_PROMPT_TOK, 0, 1 + (start - N_PROMPT_TOK) // DEC_SEQ)


def _mod_spec(tm):
    return pl.BlockSpec((None, 1, D_MODEL), lambda i: (_mod_row(i, tm), 0, 0))


def _x_specs(xs, tm):
    if len(xs) == 1:
        return [pl.BlockSpec((tm, D_MODEL), lambda i: (i, 0))]
    n_p = N_PROMPT_TOK // tm
    return [pl.BlockSpec((tm, D_MODEL), lambda i: (jnp.minimum(i, n_p - 1), 0)),
            pl.BlockSpec((tm, D_MODEL), lambda i: (jnp.maximum(i - n_p, 0), 0))]


def _x_tile(x_refs, tm):
    if len(x_refs) == 1:
        return x_refs[0][...]
    is_prompt = pl.program_id(0) * tm < N_PROMPT_TOK
    return jnp.where(is_prompt, x_refs[0][...], x_refs[1][...])


def _sigmoid(x):
    return jax.nn.sigmoid(x)


def _sigmoid_tanh(x):
    return 0.5 * jnp.tanh(0.5 * x) + 0.5


def _silu(x):
    return x * _sigmoid(x)


def _softplus(x):
    return jnp.maximum(x, 0.0) + jnp.log1p(jnp.exp(-jnp.abs(x)))


def _gelu_tanh(x):
    return 0.5 * x * (1.0 + jnp.tanh(math.sqrt(2.0 / math.pi) * (x + 0.044715 * (x * x * x))))


def _rms(x, g):
    ms = jnp.mean(x * x, axis=-1, keepdims=True)
    return (x * lax.rsqrt(ms + EPS)) * g


def _dot(a, b):
    return jnp.dot(a, b, preferred_element_type=F32)


def _dot_nt(a, b):
    return lax.dot_general(a, b, (((1,), (1,)), ((), ())), preferred_element_type=F32)


def _mod_kernel(c_ref, w_ref, b_ref, o_ref):
    c = c_ref[...]
    s = _silu(c).astype(BF16)
    o_ref[...] = _dot(s, w_ref[...].astype(BF16)) + b_ref[...]


def _modulation_all(cvec, w_mod, b_mod):
    tn = 1536
    n_out = 6 * D_MODEL
    out = pl.pallas_call(
        _mod_kernel,
        grid=(DEPTH, n_out // tn),
        in_specs=[
            pl.BlockSpec((N_MOD_ROWS, D_MODEL), lambda l, j: (0, 0)),
            pl.BlockSpec((None, D_MODEL, tn), lambda l, j: (l, 0, j)),
            pl.BlockSpec((None, 1, tn), lambda l, j: (l, 0, j)),
        ],
        out_specs=pl.BlockSpec((None, N_MOD_ROWS, tn), lambda l, j: (l, 0, j)),
        out_shape=jax.ShapeDtypeStruct((DEPTH, N_MOD_ROWS, n_out), F32),
        compiler_params=_cparams(2),
        name="modulation",
    )(cvec, w_mod, b_mod.reshape(DEPTH, 1, n_out))
    out = out.reshape(DEPTH, N_MOD_ROWS, 6, 1, D_MODEL)
    return jnp.transpose(out, (0, 2, 1, 3, 4))


def _inproj_kernel(*refs, n_x, widths, chunk, rot):
    x_refs = refs[:n_x]
    g_ref, shift_ref, scale_ref, w_ref = refs[n_x:n_x + 4]
    rest = refs[n_x + 4:]
    o_refs = rest[:len(widths)]
    h_ref = rest[len(widths)]
    if rot is not None:
        rot_start, rot_by = rot
        wr_ref = rest[len(widths) + 1]
        rot_w = w_ref.shape[1] - rot_start

        @pl.when(pl.program_id(0) == 0)
        def _():
            wr_ref[...] = pltpu.roll(w_ref[:, rot_start:], rot_w - rot_by, 1)
    elif w_ref.dtype != BF16:
        wb_ref = rest[len(widths) + 1]

        @pl.when(pl.program_id(0) == 0)
        def _():
            wb_ref[...] = w_ref[...].astype(BF16)
        w_ref = wb_ref

    x = _x_tile(x_refs, x_refs[0].shape[0])
    h = _rms(x, g_ref[...]) * (1.0 + scale_ref[...]) + shift_ref[...]
    h_ref[...] = h.astype(BF16)
    off = 0
    for o_ref, n in zip(o_refs, widths):
        for c0 in range(0, n, chunk):
            c1 = min(c0 + chunk, n)
            if rot is not None and off >= rot_start:
                w_blk = wr_ref[:, off - rot_start + c0:off - rot_start + c1]
            else:
                w_blk = w_ref[:, off + c0:off + c1]
            o_ref[:, c0:c1] = _dot(h_ref[...], w_blk)
        off += n


def _inproj(xs, g, shift, scale, w, layer, widths, name, rot=None):
    tm = TM_LINEAR
    assert sum(widths) == w.shape[2] and all(n % LANES == 0 for n in widths)
    scratch = [pltpu.VMEM((tm, D_MODEL), BF16)]
    if rot is not None:
        scratch.append(pltpu.VMEM((D_MODEL, w.shape[2] - rot[0]), BF16))
    elif w.dtype != BF16:
        scratch.append(pltpu.VMEM(w.shape[1:], BF16))
    return pl.pallas_call(
        functools.partial(_inproj_kernel, n_x=len(xs), widths=tuple(widths), chunk=512, rot=rot),
        grid=(N_TOK // tm,),
        in_specs=_x_specs(xs, tm) + [_const_spec((1, D_MODEL)), _mod_spec(tm), _mod_spec(tm),
                                     _layer_spec(w.shape, layer)],
        out_specs=[pl.BlockSpec((tm, n), lambda i: (i, 0)) for n in widths],
        out_shape=[jax.ShapeDtypeStruct((N_TOK, n), F32) for n in widths],
        scratch_shapes=scratch,
        compiler_params=_cparams(1),
        name=name,
    )(*xs, g, shift, scale, w)


def _outproj_kernel(a_ref, b_ref, w_ref, *rest, n_x):
    x_refs = rest[:n_x]
    g_ref, gate_ref, o_ref, wb_ref = rest[n_x:]

    @pl.when(pl.program_id(0) == 0)
    def _():
        wb_ref[...] = w_ref[...].astype(BF16)

    ka = a_ref.shape[1]
    acc = _dot(a_ref[...], wb_ref[0:ka, :]) + _dot(b_ref[...], wb_ref[ka:, :])
    o_ref[...] = _x_tile(x_refs, o_ref.shape[0]) + gate_ref[...] * _rms(acc, g_ref[...])


def _outproj(a, b, w, layer, xs, g, gate, name):
    tm = TM_LINEAR
    return pl.pallas_call(
        functools.partial(_outproj_kernel, n_x=len(xs)),
        grid=(N_TOK // tm,),
        in_specs=[
            pl.BlockSpec((tm, a.shape[1]), lambda i: (i, 0)),
            pl.BlockSpec((tm, b.shape[1]), lambda i: (i, 0)),
            _layer_spec(w.shape, layer),
        ] + _x_specs(xs, tm) + [_const_spec((1, D_MODEL)), _mod_spec(tm)],
        out_specs=pl.BlockSpec((tm, D_MODEL), lambda i: (i, 0)),
        out_shape=jax.ShapeDtypeStruct((N_TOK, D_MODEL), F32),
        scratch_shapes=[pltpu.VMEM(w.shape[1:], BF16)],
        compiler_params=_cparams(1),
        name=name,
    )(a, b, w, *xs, g, gate)


def _ffn_kernel(x_ref, g2_ref, shift_ref, scale_ref, wi_ref, cw_ref, cb_ref,
                wo_ref, g3_ref, gate_ref, o_ref, h_ref, act_ref, *, chunk, row_blk):
    tm = x_ref.shape[0]
    i = pl.program_id(0)
    h = _rms(x_ref[...], g2_ref[...]) * (1.0 + scale_ref[...]) + shift_ref[...]
    h_ref[...] = h.astype(BF16)
    lseq = jnp.where(i * tm < N_PROMPT_TOK, SEQ, DEC_SEQ)
    pos = lax.broadcasted_iota(jnp.int32, (tm, 1), 0) & (lseq - 1)
    first = pos == 0
    last = pos == lseq - 1
    for c0 in range(0, D_FF, chunk):
        c1 = min(c0 + chunk, D_FF)
        gt = _dot(h_ref[...], wi_ref[:, c0:c1])
        vl = _dot(h_ref[...], wi_ref[:, D_FF + c0:D_FF + c1])
        g_prev = jnp.where(first, 0.0, pltpu.roll(gt, 1, 0))
        g_next = jnp.where(last, 0.0, pltpu.roll(gt, tm - 1, 0))
        conv = (g_prev * cw_ref[0:1, c0:c1] + gt * cw_ref[1:2, c0:c1]
                + g_next * cw_ref[2:3, c0:c1] + cb_ref[:, c0:c1])
        act_ref[:, c0:c1] = (_silu(conv) * vl).astype(BF16)
    for r0 in range(0, tm, row_blk):
        rs = slice(r0, r0 + row_blk)
        acc = _dot(act_ref[rs, :], wo_ref[...])
        o_ref[rs, :] = x_ref[rs, :] + gate_ref[...] * _rms(acc, g3_ref[...])


def _ffn(x, g2, shift, scale, wi, cw, cb, wo, layer, g3, gate, name):
    tm = TM_FFN
    row_spec = pl.BlockSpec((tm, D_MODEL), lambda i: (i, 0))
    return pl.pallas_call(
        functools.partial(_ffn_kernel, chunk=2 * MXU_DIM, row_blk=MXU_DIM),
        grid=(N_TOK // tm,),
        in_specs=[
            row_spec, _const_spec((1, D_MODEL)), _mod_spec(tm), _mod_spec(tm),
            _layer_spec(wi.shape, layer), _const_spec(cw.shape),
            _const_spec(cb.shape), _layer_spec(wo.shape, layer), _const_spec((1, D_MODEL)),
            _mod_spec(tm),
        ],
        out_specs=row_spec,
        out_shape=jax.ShapeDtypeStruct((N_TOK, D_MODEL), F32),
        scratch_shapes=[pltpu.VMEM((tm, D_MODEL), BF16), pltpu.VMEM((tm, D_FF), BF16)],
        compiler_params=_cparams(1),
        name=name,
    )(x, g2, shift, scale, wi, cw, cb, wo, g3, gate)


def _seq_specs(group, widths):
    if group == "prompt":
        return [pl.BlockSpec((SEQ, w), lambda b: (b, 0)) for w in widths]
    off = N_PROMPT_TOK // DEC_SEQ
    return [pl.BlockSpec((DEC_SEQ, w), lambda b: (b + off, 0)) for w in widths]


def _short_conv_chunk(pad_ref, cw_ref, cb_ref, base, t, halo, taps, left, cols):
    win = pad_ref[pl.ds(base, t + 2 * halo), cols]
    n = t + 2 * halo
    acc = cb_ref[:, cols]
    for j in range(taps):
        s = (left - j) % n
        rolled = win if s == 0 else pltpu.roll(win, s, 0)
        acc = acc + rolled[halo:halo + t] * cw_ref[j:j + 1, cols]
    return acc


def _fill_padded(pad_ref, src, length, halo):
    width = pad_ref.shape[1]
    pad_ref[0:halo, :] = jnp.zeros((halo, width), F32)
    pad_ref[halo + length:2 * halo + length, :] = jnp.zeros((halo, width), F32)
    pad_ref[halo:halo + length, :] = src


def _lane_pairs(m, first_col, n_pairs, rows):
    lane = lax.broadcasted_iota(jnp.int32, (rows, LANES), 1)
    lo_half = lane < SSD_HEADDIM
    pieces = []
    for k in range(n_pairs):
        c = first_col + 2 * k
        lo = jnp.broadcast_to(m[:, c:c + 1], (rows, LANES))
        hi = jnp.broadcast_to(m[:, c + 1:c + 2], (rows, LANES))
        pieces.append(jnp.where(lo_half, lo, hi))
    return jnp.concatenate(pieces, axis=1)


def _ssd_kernel(*refs, length, has_h0, has_state_out, n_alias):
    it = iter(refs)
    z_ref, xbc_ref, dt_ref = next(it), next(it), next(it)
    cw_ref, cb_ref, dtb_ref, alog_ref, dskip_ref, nw_ref = (next(it) for _ in range(6))
    h0_ref = next(it) if has_h0 else None
    for _ in range(n_alias):
        next(it)
    y_ref = next(it)
    st_ref = next(it) if has_state_out else None
    pad_s, xs_s, bc_s, cum_s, row_t_s, w_t_s, cd_s, yacc_s, state_s = (
        next(it) for _ in range(9))

    t = SSD_CHUNK
    nc = length // t
    halo = SUBLANES
    gw = SSD_INNER // SSD_GROUPS
    pairs_per_group = SSD_HEADS // SSD_GROUPS // 2

    _fill_padded(pad_s, xbc_ref[...], length, halo)
    a_row = -jnp.exp(alog_ref[...])

    ri = lax.broadcasted_iota(jnp.int32, (t, t), 0)
    ci = lax.broadcasted_iota(jnp.int32, (t, t), 1)
    keep = (ci <= ri, ci >= ri)
    tril = keep[0].astype(F32)
    lane = lax.broadcasted_iota(jnp.int32, (t, LANES), 1)
    lo_half = lane < SSD_HEADDIM
    fwd_cols = lane < DT_COL0 + SSD_HEADS
    fwd_rows = ri < DT_COL0 + SSD_HEADS

    def prep(c, carry):
        base = pl.multiple_of(c * t, t)
        rows = pl.ds(base, t)
        for c0 in range(0, SSD_CONV_DIM, MXU_DIM):
            cols = slice(c0, c0 + MXU_DIM)
            conv = _silu(_short_conv_chunk(pad_s, cw_ref, cb_ref, base, t, halo, SSD_CONV, 2, cols))
            if c0 < SSD_INNER:
                xs_s[rows, cols] = conv
                yacc_s[rows, cols] = conv * dskip_ref[:, cols]
            else:
                bc_s[rows, c0 - SSD_INNER:c0 - SSD_INNER + MXU_DIM] = conv
        dtsp = _softplus(dt_ref[rows, :] + dtb_ref[...])
        a_c = dtsp * a_row
        pre = jnp.dot(tril, a_c, preferred_element_type=F32, precision=lax.Precision.HIGHEST)
        suf = pre[t - 1:t, :] - pre + a_c
        cum = jnp.where(fwd_cols, pre, suf) * LOG2E
        cum_s[rows, :] = cum
        cum_t = cum.T
        dt_t = dtsp.T
        edge_col = jnp.where(fwd_rows[:, 0:1], cum_t[:, t - 1:t], cum_t[:, 0:1])
        row_t_s[c] = cum_t - jnp.log(dt_t) * LOG2E
        w_t_s[c] = dt_t * jnp.exp2(edge_col - cum_t)
        edge_row = jnp.where(fwd_cols[0:1, :], cum[t - 1:t, :], cum[0:1, :])
        cd_s[c] = jnp.broadcast_to(jnp.exp2(edge_row), (SUBLANES, LANES))
        return carry

    lax.fori_loop(0, nc, prep, 0, unroll=min(4, nc))

    for d in range(2):
        if has_h0:
            for k in range(SSD_INNER // LANES):
                ks = slice(k * LANES, (k + 1) * LANES)
                state_s[d, :, ks] = h0_ref[d, ks, :].T
        else:
            state_s[d] = jnp.zeros((SSD_STATE, SSD_INNER), F32)

    def block_diag(m):
        return jnp.concatenate([jnp.where(lo_half, m, 0.0).astype(BF16),
                                jnp.where(lo_half, 0.0, m).astype(BF16)], axis=0)

    def chunk_step(c, carry):
        for d in range(2):
            cidx = c if d == 0 else nc - 1 - c
            base = pl.multiple_of(cidx * t, t)
            rows = pl.ds(base, t)
            cum = cum_s[rows, :]
            bc = bc_s[rows, :]
            cd = cd_s[cidx][0:1, :]
            for g in range(SSD_GROUPS):
                col0 = DT_COL0 + d * SSD_HEADS + g * 2 * pairs_per_group
                b_g = bc[:, g * SSD_STATE:(g + 1) * SSD_STATE]
                c_g = bc[:, (SSD_GROUPS + g) * SSD_STATE:(SSD_GROUPS + g + 1) * SSD_STATE]
                gmat = _dot_nt(c_g.astype(BF16), b_g.astype(BF16))
                b_t = b_g.T
                cd_rep = _lane_pairs(cd, col0, pairs_per_group, 1)
                for kk in range(pairs_per_group):
                    lanes = slice(g * gw + kk * LANES, g * gw + (kk + 1) * LANES)
                    rhs_x = block_diag(xs_s[rows, lanes])
                    st = state_s[d, :, lanes]
                    gl, ce, bw = [], [], []
                    for hcur in (col0 + 2 * kk, col0 + 2 * kk + 1):
                        hrow = pl.ds(hcur, 1)
                        colb = jnp.broadcast_to(cum[:, hcur:hcur + 1], (t, t))
                        rowb = jnp.broadcast_to(row_t_s[cidx, hrow, :], (t, t))
                        wrow = jnp.broadcast_to(w_t_s[cidx, hrow, :], (t, t))
                        lmat = jnp.exp2(jnp.where(keep[d], colb - rowb, -jnp.inf))
                        gl.append((gmat * lmat).astype(BF16))
                        ce.append((c_g * jnp.exp2(colb)).astype(BF16))
                        bw.append((b_t * wrow).astype(BF16))
                    y_p = _dot(jnp.concatenate(gl + ce, axis=1),
                               jnp.concatenate([rhs_x, block_diag(st)], axis=0))
                    state_s[d, :, lanes] = (st * cd_rep[:, kk * LANES:(kk + 1) * LANES]
                                            + _dot(jnp.concatenate(bw, axis=1), rhs_x))
                    yacc_s[rows, lanes] += y_p
        return carry

    lax.fori_loop(0, nc, chunk_step, 0, unroll=min(4, nc))

    def finish(c, carry):
        base = pl.multiple_of(c * t, t)
        rows = pl.ds(base, t)
        y = yacc_s[rows, :] * _silu(z_ref[rows, :])
        y_ref[rows, :] = _rms(y, nw_ref[...]).astype(y_ref.dtype)
        return carry

    lax.fori_loop(0, nc, finish, 0, unroll=min(4, nc))

    if has_state_out:
        for d in range(2):
            for k in range(SSD_INNER // LANES):
                ks = slice(k * LANES, (k + 1) * LANES)
                st_ref[d, ks, :] = state_s[d, :, ks].T


def _ssd(group, z, xbc, dt, cw, cb, dtb, alog, dskip, nw, i, h0_all, y_prev, st_prev, layer_name):
    prompt = group == "prompt"
    length = SEQ if prompt else DEC_SEQ
    nb = BATCH if prompt else DEC_BATCH
    off = 0 if prompt else N_PROMPT_TOK // DEC_SEQ
    in_specs = _seq_specs(group, (SSD_INNER, SSD_CONV_DIM, DT_PAD))
    in_specs += [_const_spec(a.shape) for a in (cw, cb, dtb, alog, dskip, nw)]
    args = [z, xbc, dt, cw, cb, dtb, alog, dskip, nw]
    y_spec = pl.BlockSpec((length, SSD_INNER), lambda b: (b + off, 0))
    y_shape = jax.ShapeDtypeStruct((N_TOK, SSD_INNER), BF16)
    st_block = (None, None, 2, SSD_INNER, SSD_STATE)
    aliases = {}
    if prompt:
        out_specs = [y_spec, pl.BlockSpec(st_block, lambda b: (b, i, 0, 0, 0))]
        out_shape = [y_shape, jax.ShapeDtypeStruct((nb, N_SSM_LAYERS, 2, SSD_INNER, SSD_STATE), F32)]
        if st_prev is not None:
            in_specs.append(pl.BlockSpec(memory_space=pl.ANY))
            args.append(st_prev)
            aliases = {len(args) - 1: 1}
    else:
        in_specs += [pl.BlockSpec(st_block, lambda b: (b, i, 0, 0, 0)),
                     pl.BlockSpec(memory_space=pl.ANY)]
        args += [h0_all, y_prev]
        out_specs = [y_spec]
        out_shape = [y_shape]
        aliases = {len(args) - 1: 0}
    halo = SUBLANES
    nc = length // SSD_CHUNK
    scratch = [
        pltpu.VMEM((length + 2 * halo, SSD_CONV_DIM), F32),
        pltpu.VMEM((length, SSD_INNER), F32),
        pltpu.VMEM((length, 2 * SSD_GROUPS * SSD_STATE), F32),
        pltpu.VMEM((length, DT_PAD), F32),
        pltpu.VMEM((nc, DT_PAD, SSD_CHUNK), F32),
        pltpu.VMEM((nc, DT_PAD, SSD_CHUNK), F32),
        pltpu.VMEM((nc, SUBLANES, DT_PAD), F32),
        pltpu.VMEM((length, SSD_INNER), F32),
        pltpu.VMEM((2, SSD_STATE, SSD_INNER), F32),
    ]
    return pl.pallas_call(
        functools.partial(_ssd_kernel, length=length, has_h0=not prompt, has_state_out=prompt,
                          n_alias=len(aliases)),
        grid=(nb,),
        in_specs=in_specs,
        out_specs=out_specs,
        out_shape=out_shape,
        scratch_shapes=scratch,
        input_output_aliases=aliases,
        compiler_params=_cparams(1),
        name=f"ssd_{group}_{layer_name}",
    )(*args)


def _lru_kernel(*refs, length, has_h0, has_state_out, n_alias):
    it = iter(refs)
    xl_ref, gl_ref = next(it), next(it)
    cw_ref, cb_ref, wa_ref, wx_ref, ba_ref, bx_ref, lam_ref = (next(it) for _ in range(7))
    h0_ref = next(it) if has_h0 else None
    for _ in range(n_alias):
        next(it)
    o_ref = next(it)
    st_ref = next(it) if has_state_out else None
    pad_s, xc_s, a_s, u_s, h_s = (next(it) for _ in range(5))

    t = 128
    nc = length // t
    halo = SUBLANES
    n_tiles = LRU_WIDTH // MXU_DIM

    _fill_padded(pad_s, xl_ref[...], length, halo)

    def prep(c, carry):
        base = pl.multiple_of(c * t, t)
        for c0 in range(0, LRU_WIDTH, MXU_DIM):
            cols = slice(c0, c0 + MXU_DIM)
            xc_s[pl.ds(base, t), cols] = _short_conv_chunk(
                pad_s, cw_ref, cb_ref, base, t, halo, LRU_CONV, 2, cols)
        return carry

    lax.fori_loop(0, nc, prep, 0, unroll=2)

    row8 = lax.broadcasted_iota(jnp.int32, (SUBLANES, LRU_WIDTH), 0)
    n_groups = length // SUBLANES

    for d in range(2):
        log_a_unit = (-LRU_C) * _softplus(-lam_ref[d:d + 1, :])

        def gates(c, carry, d=d, log_a_unit=log_a_unit):
            base = pl.multiple_of(c * t, t)
            rows = pl.ds(base, t)
            xc = xc_s[rows, :]
            xb = xc.astype(BF16)
            ra, ri = [], []
            for j in range(n_tiles):
                js = slice(j * MXU_DIM, (j + 1) * MXU_DIM)
                ra.append(_dot(xb[:, js], wa_ref[d, j]))
                ri.append(_dot(xb[:, js], wx_ref[d, j]))
            r = _sigmoid_tanh(jnp.concatenate(ra, axis=1) + ba_ref[d:d + 1, :])
            gi = _sigmoid_tanh(jnp.concatenate(ri, axis=1) + bx_ref[d:d + 1, :])
            log_a = r * log_a_unit
            a = jnp.exp2(r * (log_a_unit * LOG2E))
            gap = -jnp.tanh(log_a) * (a * a + 1.0)
            root = jnp.where(gap > 0.0, gap * lax.rsqrt(gap), 0.0)
            a_s[rows, :] = a
            u_s[rows, :] = root * gi * xc
            return carry

        lax.fori_loop(0, nc, gates, 0, unroll=min(4, nc))

        if has_h0:
            carry0 = jnp.broadcast_to(h0_ref[d:d + 1, :], (SUBLANES, LRU_WIDTH))
        else:
            carry0 = jnp.zeros((SUBLANES, LRU_WIDTH), F32)

        def scan(gi_, carry, d=d):
            g = gi_ if d == 0 else n_groups - 1 - gi_
            base = pl.multiple_of(g * SUBLANES, SUBLANES)
            rows = pl.ds(base, SUBLANES)
            av = a_s[rows, :]
            uv = u_s[rows, :]
            for k in (1, 2, 4):
                if d == 0:
                    shift, valid = k, row8 >= k
                else:
                    shift, valid = SUBLANES - k, row8 < SUBLANES - k
                a_sh = pltpu.roll(av, shift, 0)
                u_sh = pltpu.roll(uv, shift, 0)
                uv = jnp.where(valid, av * u_sh + uv, uv)
                av = jnp.where(valid, av * a_sh, av)
            h = av * carry + uv
            if d == 0:
                h_s[rows, :] = h
                edge = h[SUBLANES - 1:SUBLANES, :]
            else:
                h_s[rows, :] += h
                edge = h[0:1, :]
            return jnp.broadcast_to(edge, (SUBLANES, LRU_WIDTH))

        final = lax.fori_loop(0, n_groups, scan, carry0, unroll=4)
        if has_state_out:
            st_ref[d:d + 1, :] = final[0:1, :]

    def finish(c, carry):
        base = pl.multiple_of(c * t, t)
        rows = pl.ds(base, t)
        o_ref[rows, :] = (_gelu_tanh(gl_ref[rows, :]) * h_s[rows, :]).astype(o_ref.dtype)
        return carry

    lax.fori_loop(0, nc, finish, 0, unroll=2)


def _lru(group, xl, gl, cw, cb, wa, wx, ba, bx, lam, i, h0_all, o_prev, st_prev, layer_name):
    prompt = group == "prompt"
    length = SEQ if prompt else DEC_SEQ
    nb = BATCH if prompt else DEC_BATCH
    off = 0 if prompt else N_PROMPT_TOK // DEC_SEQ
    in_specs = _seq_specs(group, (LRU_WIDTH, LRU_WIDTH))
    in_specs += [_const_spec(a.shape) for a in (cw, cb, wa, wx, ba, bx, lam)]
    args = [xl, gl, cw, cb, wa, wx, ba, bx, lam]
    o_spec = pl.BlockSpec((length, LRU_WIDTH), lambda b: (b + off, 0))
    o_shape = jax.ShapeDtypeStruct((N_TOK, LRU_WIDTH), BF16)
    st_block = (None, None, 2, LRU_WIDTH)
    aliases = {}
    if prompt:
        out_specs = [o_spec, pl.BlockSpec(st_block, lambda b: (b, i, 0, 0))]
        out_shape = [o_shape, jax.ShapeDtypeStruct((nb, N_SSM_LAYERS, 2, LRU_WIDTH), F32)]
        if st_prev is not None:
            in_specs.append(pl.BlockSpec(memory_space=pl.ANY))
            args.append(st_prev)
            aliases = {len(args) - 1: 1}
    else:
        in_specs += [pl.BlockSpec(st_block, lambda b: (b, i, 0, 0)),
                     pl.BlockSpec(memory_space=pl.ANY)]
        args += [h0_all, o_prev]
        out_specs = [o_spec]
        out_shape = [o_shape]
        aliases = {len(args) - 1: 0}
    halo = SUBLANES
    scratch = [pltpu.VMEM((length + 2 * halo, LRU_WIDTH), F32)]
    scratch += [pltpu.VMEM((length, LRU_WIDTH), F32) for _ in range(4)]
    return pl.pallas_call(
        functools.partial(_lru_kernel, length=length, has_h0=not prompt, has_state_out=prompt,
                          n_alias=len(aliases)),
        grid=(nb,),
        in_specs=in_specs,
        out_specs=out_specs,
        out_shape=out_shape,
        scratch_shapes=scratch,
        input_output_aliases=aliases,
        compiler_params=_cparams(1),
        name=f"lru_{group}_{layer_name}",
    )(*args)


def _confconv_kernel(*refs, length, aliased):
    it = iter(refs)
    a_ref, g_ref, w_ref, b_ref, lng_ref, lnb_ref = (next(it) for _ in range(6))
    if aliased:
        next(it)
    o_ref = next(it)
    pad_s, acc_s = next(it), next(it)

    t = 128
    nc = length // t
    halo = 2 * SUBLANES
    left = (CONF_K - 1) // 2
    cblk = MXU_DIM
    n = t + 2 * halo

    _fill_padded(pad_s, a_ref[...] * _sigmoid_tanh(g_ref[...]), length, halo)

    def step(c, carry):
        base = pl.multiple_of(c * t, t)
        rows = pl.ds(base, t)
        for cb0 in range(0, CONV_WIDTH, cblk):
            cs = slice(cb0, cb0 + cblk)
            win = pad_s[pl.ds(base, n), cs]
            acc = jnp.broadcast_to(b_ref[:, cs], (t, cblk))
            for s in range(SUBLANES):
                shifted = win if s == 0 else pltpu.roll(win, n - s, 0)
                for m in range(n // SUBLANES):
                    j = SUBLANES * m + s - halo + left
                    if 0 <= j < CONF_K and SUBLANES * m + t <= n:
                        acc = acc + shifted[SUBLANES * m:SUBLANES * m + t] * w_ref[j:j + 1, cs]
            acc_s[:, cs] = acc
        cv = acc_s[...]
        mu = jnp.mean(cv, axis=-1, keepdims=True)
        xc = cv - mu
        var = jnp.mean(xc * xc, axis=-1, keepdims=True)
        y = (xc * lax.rsqrt(var + EPS)) * lng_ref[...] + lnb_ref[...]
        o_ref[rows, :] = _silu(y).astype(o_ref.dtype)
        return carry

    lax.fori_loop(0, nc, step, 0, unroll=2)


def _confconv(group, ga, gg, w, b, lng, lnb, o_prev, layer_name):
    prompt = group == "prompt"
    length = SEQ if prompt else DEC_SEQ
    nb = BATCH if prompt else DEC_BATCH
    off = 0 if prompt else N_PROMPT_TOK // DEC_SEQ
    in_specs = _seq_specs(group, (CONV_WIDTH, CONV_WIDTH))
    in_specs += [_const_spec(a.shape) for a in (w, b, lng, lnb)]
    args = [ga, gg, w, b, lng, lnb]
    aliases = {}
    if not prompt:
        in_specs.append(pl.BlockSpec(memory_space=pl.ANY))
        args.append(o_prev)
        aliases = {len(args) - 1: 0}
    halo = 2 * SUBLANES
    return pl.pallas_call(
        functools.partial(_confconv_kernel, length=length, aliased=not prompt),
        grid=(nb,),
        in_specs=in_specs,
        out_specs=pl.BlockSpec((length, CONV_WIDTH), lambda b: (b + off, 0)),
        out_shape=jax.ShapeDtypeStruct((N_TOK, CONV_WIDTH), BF16),
        scratch_shapes=[pltpu.VMEM((length + 2 * halo, CONV_WIDTH), F32),
                        pltpu.VMEM((128, CONV_WIDTH), F32)],
        input_output_aliases=aliases,
        compiler_params=_cparams(1),
        name=f"confconv_{group}_{layer_name}",
    )(*args)


def _rope(x, cos_t, sin_t):
    lane = lax.broadcasted_iota(jnp.int32, x.shape, 1)
    quarter = ROPE_AXIS_DIM // 2
    partner = jnp.where((lane & quarter) == 0,
                        pltpu.roll(x, HEAD_DIM - quarter, 1), pltpu.roll(x, quarter, 1))
    return x * cos_t + partner * sin_t


def _attn_kernel(*refs, length, n_ctx, use_rope, has_cache_out, n_alias, qb):
    it = iter(refs)
    q_ref, k_ref, v_ref, qg_ref, kg_ref = (next(it) for _ in range(5))
    if n_ctx:
        ck_ref, cv_ref = next(it), next(it)
    if use_rope:
        cos_ref, sin_ref = next(it), next(it)
    for _ in range(n_alias):
        next(it)
    o_ref = next(it)
    kn_ref, vc_ref = (next(it), next(it)) if has_cache_out else (None, None)
    kall_s, vall_s, s_s = next(it), next(it), next(it)

    nq = length // qb
    rep = N_HEADS // N_KV_HEADS
    scale = HEAD_DIM ** -0.5

    for g in range(N_KV_HEADS):
        gs = slice(g * HEAD_DIM, (g + 1) * HEAD_DIM)
        kn = _rms(k_ref[:, gs], kg_ref[...])
        if kn_ref is not None:
            kn_ref[:, gs] = kn
            vc_ref[:, gs] = v_ref[:, gs]
        if use_rope:
            kn = _rope(kn, cos_ref[...], sin_ref[...])
        if n_ctx:
            kall_s[g, 0:n_ctx, :] = ck_ref[:, gs].astype(BF16)
            vall_s[g, 0:n_ctx, 0:HEAD_DIM] = cv_ref[:, gs].astype(BF16)
        kall_s[g, n_ctx:n_ctx + length, :] = kn.astype(BF16)
        vall_s[g, n_ctx:n_ctx + length, 0:HEAD_DIM] = v_ref[:, gs].astype(BF16)
        vall_s[g, :, HEAD_DIM:] = jnp.ones((n_ctx + length, HEAD_DIM), BF16)

    c_exp = scale * LOG2E

    def scores(i, slot):
        rows = pl.ds(pl.multiple_of(i * qb, qb), qb)
        for g in range(N_KV_HEADS):
            qs = []
            for r in range(rep):
                hs = slice((g * rep + r) * HEAD_DIM, (g * rep + r + 1) * HEAD_DIM)
                qn = _rms(q_ref[rows, hs], qg_ref[...])
                if use_rope:
                    qn = _rope(qn, cos_ref[rows, :], sin_ref[rows, :])
                qs.append(qn.astype(BF16))
            s_s[slot, g] = _dot_nt(jnp.concatenate(qs, axis=0), kall_s[g])

    def outputs(i, slot):
        rows = pl.ds(pl.multiple_of(i * qb, qb), qb)
        for g in range(N_KV_HEADS):
            s = s_s[slot, g]
            m = jnp.max(s, axis=-1, keepdims=True)
            p = jnp.exp2((s - m) * c_exp)
            ov = _dot(p.astype(BF16), vall_s[g])
            o = ov[:, :HEAD_DIM] / ov[:, HEAD_DIM:]
            for r in range(rep):
                hs = slice((g * rep + r) * HEAD_DIM, (g * rep + r + 1) * HEAD_DIM)
                o_ref[rows, hs] = o[r * qb:(r + 1) * qb].astype(o_ref.dtype)

    scores(0, 0)

    def pair(j, carry):
        i = 2 * j
        scores(i + 1, 1)
        outputs(i, 0)
        scores(jnp.minimum(i + 2, nq - 1), 0)
        outputs(i + 1, 1)
        return carry

    lax.fori_loop(0, nq // 2, pair, 0, unroll=min(2, nq // 2))


def _attn(group, q, k, v, qg, kg, i, ck_all, cv_all, cos_t, sin_t, o_prev, kc_prev, vc_prev,
          layer_name):
    prompt = group == "prompt"
    length = SEQ if prompt else DEC_SEQ
    nb = BATCH if prompt else DEC_BATCH
    off = 0 if prompt else N_PROMPT_TOK // DEC_SEQ
    kvw = N_KV_HEADS * HEAD_DIM
    n_ctx = 0 if prompt else PAST_LEN
    qb = 128
    rep = N_HEADS // N_KV_HEADS
    in_specs = _seq_specs(group, (D_MODEL, kvw, kvw))
    in_specs += [_const_spec(qg.shape), _const_spec(kg.shape)]
    args = [q, k, v, qg, kg]
    o_spec = pl.BlockSpec((length, D_MODEL), lambda b: (b + off, 0))
    o_shape = jax.ShapeDtypeStruct((N_TOK, D_MODEL), BF16)
    aliases = {}
    if prompt:
        c_spec = pl.BlockSpec((None, None, length, kvw), lambda b: (b, i, 0, 0))
        c_shape = jax.ShapeDtypeStruct((nb, N_ATTN_LAYERS, length, kvw), F32)
        out_specs = [o_spec, c_spec, c_spec]
        out_shape = [o_shape, c_shape, c_shape]
        if kc_prev is not None:
            in_specs += [pl.BlockSpec(memory_space=pl.ANY)] * 2
            args += [kc_prev, vc_prev]
            aliases = {len(args) - 2: 1, len(args) - 1: 2}
    else:
        ctx_spec = pl.BlockSpec((None, None, n_ctx, kvw), lambda b: (b, i, 0, 0))
        in_specs += [ctx_spec, ctx_spec, _const_spec(cos_t.shape), _const_spec(sin_t.shape),
                     pl.BlockSpec(memory_space=pl.ANY)]
        args += [ck_all, cv_all, cos_t, sin_t, o_prev]
        out_specs = [o_spec]
        out_shape = [o_shape]
        aliases = {len(args) - 1: 0}
    return pl.pallas_call(
        functools.partial(_attn_kernel, length=length, n_ctx=n_ctx, use_rope=not prompt,
                          has_cache_out=prompt, n_alias=len(aliases), qb=qb),
        grid=(nb,),
        in_specs=in_specs,
        out_specs=out_specs,
        out_shape=out_shape,
        scratch_shapes=[pltpu.VMEM((N_KV_HEADS, n_ctx + length, HEAD_DIM), BF16),
                        pltpu.VMEM((N_KV_HEADS, n_ctx + length, 2 * HEAD_DIM), BF16),
                        pltpu.VMEM((2, N_KV_HEADS, rep * qb, n_ctx + length), F32)],
        input_output_aliases=aliases,
        compiler_params=_cparams(1),
        name=f"attn_{group}_{layer_name}",
    )(*args)


def _rope_tables(rows):
    row_pos = jnp.repeat(jnp.arange(rows, dtype=F32), GRID_W)
    col_pos = jnp.tile(jnp.arange(GRID_W, dtype=F32), rows)
    inv_freq = jnp.power(ROPE_THETA, -jnp.arange(0, ROPE_AXIS_DIM, 2, dtype=F32) / ROPE_AXIS_DIM)
    ang_r = row_pos[:, None] * inv_freq
    ang_c = col_pos[:, None] * inv_freq
    cos_t = jnp.concatenate([jnp.cos(ang_r), jnp.cos(ang_r), jnp.cos(ang_c), jnp.cos(ang_c)], axis=1)
    sin_t = jnp.concatenate([-jnp.sin(ang_r), jnp.sin(ang_r), -jnp.sin(ang_c), jnp.sin(ang_c)], axis=1)
    return cos_t, sin_t


def _block_diag_tiles(w):
    per = MXU_DIM // LRU_BW
    n_tiles = LRU_BLOCKS // per
    w = w.reshape(2, n_tiles, per, LRU_BW, LRU_BW)
    eye = jnp.eye(per, dtype=w.dtype)
    tiles = jnp.einsum('dtpio,pq->dtpiqo', w, eye)
    return tiles.reshape(2, n_tiles, MXU_DIM, MXU_DIM).astype(BF16)


def _row(v):
    return v.reshape(1, -1)


def kernel(x_prompt, x_sample, state_ssd, state_lru, cache_k, cache_v, c, c_ctx,
           w_mod, b_mod, norm_g, w_in_ssm, ssd_conv_w, ssd_conv_b, ssd_a_log, ssd_dt_bias,
           ssd_d, ssd_norm_w, lru_conv_w, lru_conv_b, lru_wa, lru_ba, lru_wx, lru_bx,
           lru_lambda, w_out_ssm, w_in_ca, conf_dw_w, conf_dw_b, conf_ln_g, conf_ln_b,
           q_norm_g, k_norm_g, w_out_ca, ffn_w_in, ffn_conv_w, ffn_conv_b, ffn_w_out):
    x = (x_prompt.reshape(N_PROMPT_TOK, D_MODEL), x_sample.reshape(N_SAMPLE_TOK, D_MODEL))
    cvec = jnp.concatenate(
        [c_ctx[None], c, jnp.zeros((N_MOD_ROWS - 1 - DEC_BATCH, D_MODEL), F32)], axis=0)
    mods = _modulation_all(cvec, w_mod, b_mod)
    cos_t, sin_t = _rope_tables(DEC_SEQ // GRID_W)
    kvw = N_KV_HEADS * HEAD_DIM

    w_ssm_in = jnp.pad(w_in_ssm, ((0, 0), (0, 0), (0, DT_COL0))).astype(BF16)
    w_ffn_in = ffn_w_in.astype(BF16)
    w_ffn_out = ffn_w_out.astype(BF16)
    ssd_h0 = state_ssd.reshape(DEC_BATCH, N_SSM_LAYERS, 2, SSD_INNER, SSD_STATE)
    ck_all = cache_k.reshape(DEC_BATCH, N_ATTN_LAYERS, PAST_LEN, kvw)
    cv_all = cache_v.reshape(DEC_BATCH, N_ATTN_LAYERS, PAST_LEN, kvw)

    ssd_st = lru_st = kc = vc = None
    for layer in range(DEPTH):
        i = layer // 2
        m = mods[layer]
        name = f"l{layer}"
        g0, g1, g2, g3 = (_row(norm_g[layer, j]) for j in range(4))
        if layer % 2 == 0:
            z, xbc, xl, gl, dt = _inproj(
                x, g0, m[0], m[1], w_ssm_in, i,
                (SSD_INNER, SSD_CONV_DIM, LRU_WIDTH, LRU_WIDTH, DT_PAD), f"inproj_ssm_{name}",
                rot=(SSM_O2, 2 * SSD_HEADS))
            dtb = jnp.pad(ssd_dt_bias[i].reshape(1, -1), ((0, 0), (DT_COL0, 0)))
            alog = jnp.pad(ssd_a_log[i].reshape(1, -1), ((0, 0), (DT_COL0, 0)))
            dskip = _row(jnp.repeat(ssd_d[i], SSD_HEADDIM))
            ssd_args = (ssd_conv_w[i], _row(ssd_conv_b[i]), dtb, alog, dskip, _row(ssd_norm_w[i]))
            y, ssd_st = _ssd("prompt", z, xbc, dt, *ssd_args, i, None, None, ssd_st, name)
            (y,) = _ssd("sample", z, xbc, dt, *ssd_args, i, ssd_h0, y, None, name)
            lru_args = (lru_conv_w[i], _row(lru_conv_b[i]), _block_diag_tiles(lru_wa[i]),
                        _block_diag_tiles(lru_wx[i]), lru_ba[i], lru_bx[i], lru_lambda[i])
            yl, lru_st = _lru("prompt", xl, gl, *lru_args, i, None, None, lru_st, name)
            (yl,) = _lru("sample", xl, gl, *lru_args, i, state_lru, yl, None, name)
            x = (_outproj(y, yl, w_out_ssm, i, x, g1, m[2], f"outproj_ssm_{name}"),)
        else:
            ga, gg, q, k, v = _inproj(
                x, g0, m[0], m[1], w_in_ca, i,
                (CONV_WIDTH, CONV_WIDTH, N_HEADS * HEAD_DIM, kvw, kvw), f"inproj_ca_{name}")
            conv_args = (conf_dw_w[i], _row(conf_dw_b[i]), _row(conf_ln_g[i]), _row(conf_ln_b[i]))
            cvo = _confconv("prompt", ga, gg, *conv_args, None, name)
            cvo = _confconv("sample", ga, gg, *conv_args, cvo, name)
            qg, kg = _row(q_norm_g[i]), _row(k_norm_g[i])
            o, kc, vc = _attn("prompt", q, k, v, qg, kg, i, None, None, None, None, None, kc, vc,
                              name)
            (o,) = _attn("sample", q, k, v, qg, kg, i, ck_all, cv_all, cos_t, sin_t, o, None, None,
                         name)
            x = (_outproj(cvo, o, w_out_ca, i, x, g1, m[2], f"outproj_ca_{name}"),)
        x = (_ffn(x[0], g2, m[3], m[4], w_ffn_in, ffn_conv_w[layer], _row(ffn_conv_b[layer]),
                  w_ffn_out, layer, g3, m[5], f"ffn_{name}"),)

    xp = x[0][:N_PROMPT_TOK].reshape(BATCH, SEQ, D_MODEL)
    xs = x[0][N_PROMPT_TOK:].reshape(DEC_BATCH, DEC_SEQ, D_MODEL)
    return (xp, xs,
            ssd_st.reshape(BATCH, N_SSM_LAYERS, 2, SSD_HEADS, SSD_HEADDIM, SSD_STATE),
            lru_st,
            kc.reshape(BATCH, N_ATTN_LAYERS, SEQ, N_KV_HEADS, HEAD_DIM),
            vc.reshape(BATCH, N_ATTN_LAYERS, SEQ, N_KV_HEADS, HEAD_DIM))
```

```python
import functools
import math

import jax
import jax.numpy as jnp
from jax import lax
from jax.experimental import pallas as pl
from jax.experimental.pallas import tpu as pltpu

F32 = jnp.float32
BF16 = jnp.bfloat16

D_MODEL = 1024
BATCH = 16
SEQ = 256
DEPTH = 4
N_SSM_LAYERS = (DEPTH + 1) // 2
N_ATTN_LAYERS = DEPTH // 2
DEC_BATCH = 4
DEC_SEQ = 1024
PAST_LEN = 512
GRID_W = 64
EPS = 1e-6
SSD_HEADDIM = 64
SSD_INNER = D_MODEL
SSD_HEADS = SSD_INNER // SSD_HEADDIM
SSD_GROUPS = 2
SSD_STATE = 128
SSD_CONV = 4
SSD_CHUNK = 128
SSD_CONV_DIM = SSD_INNER + 2 * SSD_GROUPS * SSD_STATE
LRU_WIDTH = D_MODEL
LRU_BW = 64
LRU_BLOCKS = LRU_WIDTH // LRU_BW
LRU_CONV = 4
LRU_C = 8.0
CONV_WIDTH = D_MODEL
CONF_K = 31
HEAD_DIM = 128
N_HEADS = D_MODEL // HEAD_DIM
N_KV_HEADS = 2
ROPE_THETA = 10000.0
ROPE_AXIS_DIM = HEAD_DIM // 2
D_FF = 2816
FFN_CONV = 3

N_PROMPT_TOK = BATCH * SEQ
N_SAMPLE_TOK = DEC_BATCH * DEC_SEQ
N_TOK = N_PROMPT_TOK + N_SAMPLE_TOK
N_MOD_ROWS = 8
LANES = 128
SUBLANES = 8
MXU_DIM = 256
DT_PAD = LANES
SSM_O2 = SSD_INNER + SSD_CONV_DIM
DT_COL0 = DT_PAD - 2 * SSD_HEADS
LOG2E = 1.0 / math.log(2.0)
VMEM_LIMIT = 58 * 1024 * 1024

TM_LINEAR = 512
TM_FFN = 1024


def _cparams(n_axes):
    return pltpu.CompilerParams(
        dimension_semantics=("arbitrary",) * n_axes,
        vmem_limit_bytes=VMEM_LIMIT)


def _const_spec(shape):
    nd = len(shape)
    return pl.BlockSpec(shape, lambda *_: (0,) * nd, pipeline_mode=pl.Buffered(1))


def _layer_spec(shape, layer):
    nd = len(shape) - 1
    return pl.BlockSpec((None,) + tuple(shape[1:]), lambda *_: (layer,) + (0,) * nd,
                        pipeline_mode=pl.Buffered(1))


def _mod_row(i, tm):
    start = i * tm
    return jnp.where(start < N_PROMPT_TOK, 0, 1 + (start - N_PROMPT_TOK) // DEC_SEQ)


def _mod_spec(tm):
    return pl.BlockSpec((None, 1, D_MODEL), lambda i: (_mod_row(i, tm), 0, 0))


def _x_specs(xs, tm):
    if len(xs) == 1:
        return [pl.BlockSpec((tm, D_MODEL), lambda i: (i, 0))]
    n_p = N_PROMPT_TOK // tm
    return [pl.BlockSpec((tm, D_MODEL), lambda i: (jnp.minimum(i, n_p - 1), 0)),
            pl.BlockSpec((tm, D_MODEL), lambda i: (jnp.maximum(i - n_p, 0), 0))]


def _x_tile(x_refs, tm):
    if len(x_refs) == 1:
        return x_refs[0][...]
    is_prompt = pl.program_id(0) * tm < N_PROMPT_TOK
    return jnp.where(is_prompt, x_refs[0][...], x_refs[1][...])


def _sigmoid(x):
    return jax.nn.sigmoid(x)


def _sigmoid_tanh(x):
    return 0.5 * jnp.tanh(0.5 * x) + 0.5


def _silu(x):
    return x * _sigmoid(x)


def _softplus(x):
    return jnp.maximum(x, 0.0) + jnp.log1p(jnp.exp(-jnp.abs(x)))


def _gelu_tanh(x):
    return 0.5 * x * (1.0 + jnp.tanh(math.sqrt(2.0 / math.pi) * (x + 0.044715 * (x * x * x))))


def _rms(x, g):
    ms = jnp.mean(x * x, axis=-1, keepdims=True)
    return (x * lax.rsqrt(ms + EPS)) * g


def _dot(a, b):
    return jnp.dot(a, b, preferred_element_type=F32)


def _dot_nt(a, b):
    return lax.dot_general(a, b, (((1,), (1,)), ((), ())), preferred_element_type=F32)


def _mod_kernel(c_ref, w_ref, b_ref, o_ref):
    c = c_ref[...]
    s = _silu(c).astype(BF16)
    o_ref[...] = _dot(s, w_ref[...].astype(BF16)) + b_ref[...]


def _modulation_all(cvec, w_mod, b_mod):
    tn = 1536
    n_out = 6 * D_MODEL
    out = pl.pallas_call(
        _mod_kernel,
        grid=(DEPTH, n_out // tn),
        in_specs=[
            pl.BlockSpec((N_MOD_ROWS, D_MODEL), lambda l, j: (0, 0)),
            pl.BlockSpec((None, D_MODEL, tn), lambda l, j: (l, 0, j)),
            pl.BlockSpec((None, 1, tn), lambda l, j: (l, 0, j)),
        ],
        out_specs=pl.BlockSpec((None, N_MOD_ROWS, tn), lambda l, j: (l, 0, j)),
        out_shape=jax.ShapeDtypeStruct((DEPTH, N_MOD_ROWS, n_out), F32),
        compiler_params=_cparams(2),
        name="modulation",
    )(cvec, w_mod, b_mod.reshape(DEPTH, 1, n_out))
    out = out.reshape(DEPTH, N_MOD_ROWS, 6, 1, D_MODEL)
    return jnp.transpose(out, (0, 2, 1, 3, 4))


def _inproj_kernel(*refs, n_x, widths, chunk, rot):
    x_refs = refs[:n_x]
    g_ref, shift_ref, scale_ref, w_ref = refs[n_x:n_x + 4]
    rest = refs[n_x + 4:]
    o_refs = rest[:len(widths)]
    h_ref = rest[len(widths)]
    if rot is not None:
        rot_start, rot_by = rot
        wr_ref = rest[len(widths) + 1]
        rot_w = w_ref.shape[1] - rot_start

        @pl.when(pl.program_id(0) == 0)
        def _():
            wr_ref[...] = pltpu.roll(w_ref[:, rot_start:], rot_w - rot_by, 1)
    elif w_ref.dtype != BF16:
        wb_ref = rest[len(widths) + 1]

        @pl.when(pl.program_id(0) == 0)
        def _():
            wb_ref[...] = w_ref[...].astype(BF16)
        w_ref = wb_ref

    x = _x_tile(x_refs, x_refs[0].shape[0])
    h = _rms(x, g_ref[...]) * (1.0 + scale_ref[...]) + shift_ref[...]
    h_ref[...] = h.astype(BF16)
    off = 0
    for o_ref, n in zip(o_refs, widths):
        for c0 in range(0, n, chunk):
            c1 = min(c0 + chunk, n)
            if rot is not None and off >= rot_start:
                w_blk = wr_ref[:, off - rot_start + c0:off - rot_start + c1]
            else:
                w_blk = w_ref[:, off + c0:off + c1]
            o_ref[:, c0:c1] = _dot(h_ref[...], w_blk)
        off += n


def _inproj(xs, g, shift, scale, w, layer, widths, name, rot=None):
    tm = TM_LINEAR
    assert sum(widths) == w.shape[2] and all(n % LANES == 0 for n in widths)
    scratch = [pltpu.VMEM((tm, D_MODEL), BF16)]
    if rot is not None:
        scratch.append(pltpu.VMEM((D_MODEL, w.shape[2] - rot[0]), BF16))
    elif w.dtype != BF16:
        scratch.append(pltpu.VMEM(w.shape[1:], BF16))
    return pl.pallas_call(
        functools.partial(_inproj_kernel, n_x=len(xs), widths=tuple(widths), chunk=512, rot=rot),
        grid=(N_TOK // tm,),
        in_specs=_x_specs(xs, tm) + [_const_spec((1, D_MODEL)), _mod_spec(tm), _mod_spec(tm),
                                     _layer_spec(w.shape, layer)],
        out_specs=[pl.BlockSpec((tm, n), lambda i: (i, 0)) for n in widths],
        out_shape=[jax.ShapeDtypeStruct((N_TOK, n), F32) for n in widths],
        scratch_shapes=scratch,
        compiler_params=_cparams(1),
        name=name,
    )(*xs, g, shift, scale, w)


def _outproj_kernel(a_ref, b_ref, w_ref, *rest, n_x):
    x_refs = rest[:n_x]
    g_ref, gate_ref, fw1_ref, fw2_ref, o_ref, fb1_ref, fb2_ref, wb_ref = rest[n_x:]
    fb1_ref[...] = fw1_ref[...].astype(BF16)
    fb2_ref[...] = fw2_ref[...].astype(BF16)

    @pl.when(pl.program_id(0) == 0)
    def _():
        wb_ref[...] = w_ref[...].astype(BF16)

    ka = a_ref.shape[1]
    acc = _dot(a_ref[...], wb_ref[0:ka, :]) + _dot(b_ref[...], wb_ref[ka:, :])
    o_ref[...] = _x_tile(x_refs, o_ref.shape[0]) + gate_ref[...] * _rms(acc, g_ref[...])


def _outproj(a, b, w, layer, xs, g, gate, fw1, fw2, flayer, name):
    tm = TM_LINEAR
    steps = N_TOK // tm
    r1, r2 = fw1.shape[1] // steps, fw2.shape[1] // steps
    assert r1 * steps == fw1.shape[1] and r2 * steps == fw2.shape[1] and r1 % 16 == 0 and r2 % 16 == 0
    return pl.pallas_call(
        functools.partial(_outproj_kernel, n_x=len(xs)),
        grid=(steps,),
        in_specs=[
            pl.BlockSpec((tm, a.shape[1]), lambda i: (i, 0)),
            pl.BlockSpec((tm, b.shape[1]), lambda i: (i, 0)),
            _layer_spec(w.shape, layer),
        ] + _x_specs(xs, tm) + [
            _const_spec((1, D_MODEL)), _mod_spec(tm),
            pl.BlockSpec((None, r1, fw1.shape[2]), lambda i: (flayer, i, 0)),
            pl.BlockSpec((None, r2, fw2.shape[2]), lambda i: (flayer, i, 0)),
        ],
        out_specs=[pl.BlockSpec((tm, D_MODEL), lambda i: (i, 0)),
                   pl.BlockSpec((r1, fw1.shape[2]), lambda i: (i, 0)),
                   pl.BlockSpec((r2, fw2.shape[2]), lambda i: (i, 0))],
        out_shape=[jax.ShapeDtypeStruct((N_TOK, D_MODEL), F32),
                   jax.ShapeDtypeStruct(fw1.shape[1:], BF16),
                   jax.ShapeDtypeStruct(fw2.shape[1:], BF16)],
        scratch_shapes=[pltpu.VMEM(w.shape[1:], BF16)],
        compiler_params=_cparams(1),
        name=name,
    )(a, b, w, *xs, g, gate, fw1, fw2)


def _ffn_kernel(x_ref, g2_ref, shift_ref, scale_ref, wi_ref, cw_ref, cb_ref,
                wo_ref, g3_ref, gate_ref, o_ref, h_ref, act_ref, *, chunk, row_blk):
    tm = x_ref.shape[0]
    i = pl.program_id(0)
    h = _rms(x_ref[...], g2_ref[...]) * (1.0 + scale_ref[...]) + shift_ref[...]
    h_ref[...] = h.astype(BF16)
    lseq = jnp.where(i * tm < N_PROMPT_TOK, SEQ, DEC_SEQ)
    pos = lax.broadcasted_iota(jnp.int32, (tm, 1), 0) & (lseq - 1)
    first = pos == 0
    last = pos == lseq - 1
    for c0 in range(0, D_FF, chunk):
        c1 = min(c0 + chunk, D_FF)
        gt = _dot(h_ref[...], wi_ref[:, c0:c1])
        vl = _dot(h_ref[...], wi_ref[:, D_FF + c0:D_FF + c1])
        g_prev = jnp.where(first, 0.0, pltpu.roll(gt, 1, 0))
        g_next = jnp.where(last, 0.0, pltpu.roll(gt, tm - 1, 0))
        conv = (g_prev * cw_ref[0:1, c0:c1] + gt * cw_ref[1:2, c0:c1]
                + g_next * cw_ref[2:3, c0:c1] + cb_ref[:, c0:c1])
        act_ref[:, c0:c1] = (_silu(conv) * vl).astype(BF16)
    for r0 in range(0, tm, row_blk):
        rs = slice(r0, r0 + row_blk)
        acc = _dot(act_ref[rs, :], wo_ref[...])
        o_ref[rs, :] = x_ref[rs, :] + gate_ref[...] * _rms(acc, g3_ref[...])


def _ffn(x, g2, shift, scale, wi, cw, cb, wo, g3, gate, name):
    tm = TM_FFN
    row_spec = pl.BlockSpec((tm, D_MODEL), lambda i: (i, 0))
    return pl.pallas_call(
        functools.partial(_ffn_kernel, chunk=2 * MXU_DIM, row_blk=MXU_DIM),
        grid=(N_TOK // tm,),
        in_specs=[
            row_spec, _const_spec((1, D_MODEL)), _mod_spec(tm), _mod_spec(tm),
            _const_spec(wi.shape), _const_spec(cw.shape),
            _const_spec(cb.shape), _const_spec(wo.shape), _const_spec((1, D_MODEL)),
            _mod_spec(tm),
        ],
        out_specs=row_spec,
        out_shape=jax.ShapeDtypeStruct((N_TOK, D_MODEL), F32),
        scratch_shapes=[pltpu.VMEM((tm, D_MODEL), BF16), pltpu.VMEM((tm, D_FF), BF16)],
        compiler_params=_cparams(1),
        name=name,
    )(x, g2, shift, scale, wi, cw, cb, wo, g3, gate)


def _seq_specs(group, widths):
    if group == "prompt":
        return [pl.BlockSpec((SEQ, w), lambda b: (b, 0)) for w in widths]
    off = N_PROMPT_TOK // DEC_SEQ
    return [pl.BlockSpec((DEC_SEQ, w), lambda b: (b + off, 0)) for w in widths]


def _short_conv_chunk(pad_ref, cw_ref, cb_ref, base, t, halo, taps, left, cols):
    win = pad_ref[pl.ds(base, t + 2 * halo), cols]
    n = t + 2 * halo
    acc = cb_ref[:, cols]
    for j in range(taps):
        s = (left - j) % n
        rolled = win if s == 0 else pltpu.roll(win, s, 0)
        acc = acc + rolled[halo:halo + t] * cw_ref[j:j + 1, cols]
    return acc


def _fill_padded(pad_ref, src, length, halo):
    width = pad_ref.shape[1]
    pad_ref[0:halo, :] = jnp.zeros((halo, width), F32)
    pad_ref[halo + length:2 * halo + length, :] = jnp.zeros((halo, width), F32)
    pad_ref[halo:halo + length, :] = src


def _lane_pairs(m, first_col, n_pairs, rows):
    lane = lax.broadcasted_iota(jnp.int32, (rows, LANES), 1)
    lo_half = lane < SSD_HEADDIM
    pieces = []
    for k in range(n_pairs):
        c = first_col + 2 * k
        lo = jnp.broadcast_to(m[:, c:c + 1], (rows, LANES))
        hi = jnp.broadcast_to(m[:, c + 1:c + 2], (rows, LANES))
        pieces.append(jnp.where(lo_half, lo, hi))
    return jnp.concatenate(pieces, axis=1)


def _ssd_kernel(*refs, length, has_h0, has_state_out, n_alias):
    it = iter(refs)
    z_ref, xbc_ref, dt_ref = next(it), next(it), next(it)
    cw_ref, cb_ref, dtb_ref, alog_ref, dskip_ref, nw_ref = (next(it) for _ in range(6))
    h0_ref = next(it) if has_h0 else None
    for _ in range(n_alias):
        next(it)
    y_ref = next(it)
    st_ref = next(it) if has_state_out else None
    pad_s, xs_s, bc_s, cum_s, row_t_s, w_t_s, cd_s, yacc_s, state_s = (
        next(it) for _ in range(9))

    t = SSD_CHUNK
    nc = length // t
    halo = SUBLANES
    gw = SSD_INNER // SSD_GROUPS
    pairs_per_group = SSD_HEADS // SSD_GROUPS // 2

    _fill_padded(pad_s, xbc_ref[...], length, halo)
    a_row = -jnp.exp(alog_ref[...])

    ri = lax.broadcasted_iota(jnp.int32, (t, t), 0)
    ci = lax.broadcasted_iota(jnp.int32, (t, t), 1)
    keep = (ci <= ri, ci >= ri)
    tril = keep[0].astype(F32)
    lane = lax.broadcasted_iota(jnp.int32, (t, LANES), 1)
    lo_half = lane < SSD_HEADDIM
    fwd_cols = lane < DT_COL0 + SSD_HEADS
    fwd_rows = ri < DT_COL0 + SSD_HEADS

    def prep(c, carry):
        base = pl.multiple_of(c * t, t)
        rows = pl.ds(base, t)
        for c0 in range(0, SSD_CONV_DIM, MXU_DIM):
            cols = slice(c0, c0 + MXU_DIM)
            conv = _silu(_short_conv_chunk(pad_s, cw_ref, cb_ref, base, t, halo, SSD_CONV, 2, cols))
            if c0 < SSD_INNER:
                xs_s[rows, cols] = conv
                yacc_s[rows, cols] = conv * dskip_ref[:, cols]
            else:
                bc_s[rows, c0 - SSD_INNER:c0 - SSD_INNER + MXU_DIM] = conv
        dtsp = _softplus(dt_ref[rows, :] + dtb_ref[...])
        a_c = dtsp * a_row
        pre = jnp.dot(tril, a_c, preferred_element_type=F32, precision=lax.Precision.HIGHEST)
        suf = pre[t - 1:t, :] - pre + a_c
        cum = jnp.where(fwd_cols, pre, suf) * LOG2E
        cum_s[rows, :] = cum
        cum_t = cum.T
        dt_t = dtsp.T
        edge_col = jnp.where(fwd_rows[:, 0:1], cum_t[:, t - 1:t], cum_t[:, 0:1])
        row_t_s[c] = cum_t - jnp.log(dt_t) * LOG2E
        w_t_s[c] = dt_t * jnp.exp2(edge_col - cum_t)
        edge_row = jnp.where(fwd_cols[0:1, :], cum[t - 1:t, :], cum[0:1, :])
        cd_s[c] = jnp.broadcast_to(jnp.exp2(edge_row), (SUBLANES, LANES))
        return carry

    lax.fori_loop(0, nc, prep, 0, unroll=min(4, nc))

    for d in range(2):
        if has_h0:
            for k in range(SSD_INNER // LANES):
                ks = slice(k * LANES, (k + 1) * LANES)
                state_s[d, :, ks] = h0_ref[d, ks, :].T
        else:
            state_s[d] = jnp.zeros((SSD_STATE, SSD_INNER), F32)

    def block_diag(m):
        return jnp.concatenate([jnp.where(lo_half, m, 0.0).astype(BF16),
                                jnp.where(lo_half, 0.0, m).astype(BF16)], axis=0)

    def chunk_step(c, carry):
        for d in range(2):
            cidx = c if d == 0 else nc - 1 - c
            base = pl.multiple_of(cidx * t, t)
            rows = pl.ds(base, t)
            cum = cum_s[rows, :]
            bc = bc_s[rows, :]
            cd = cd_s[cidx][0:1, :]
            for g in range(SSD_GROUPS):
                col0 = DT_COL0 + d * SSD_HEADS + g * 2 * pairs_per_group
                b_g = bc[:, g * SSD_STATE:(g + 1) * SSD_STATE]
                c_g = bc[:, (SSD_GROUPS + g) * SSD_STATE:(SSD_GROUPS + g + 1) * SSD_STATE]
                gmat = _dot_nt(c_g.astype(BF16), b_g.astype(BF16))
                b_t = b_g.T
                cd_rep = _lane_pairs(cd, col0, pairs_per_group, 1)
                for kk in range(pairs_per_group):
                    lanes = slice(g * gw + kk * LANES, g * gw + (kk + 1) * LANES)
                    rhs_x = block_diag(xs_s[rows, lanes])
                    st = state_s[d, :, lanes]
                    gl, ce, bw = [], [], []
                    for hcur in (col0 + 2 * kk, col0 + 2 * kk + 1):
                        hrow = pl.ds(hcur, 1)
                        colb = jnp.broadcast_to(cum[:, hcur:hcur + 1], (t, t))
                        rowb = jnp.broadcast_to(row_t_s[cidx, hrow, :], (t, t))
                        wrow = jnp.broadcast_to(w_t_s[cidx, hrow, :], (t, t))
                        lmat = jnp.exp2(jnp.where(keep[d], colb - rowb, -jnp.inf))
                        gl.append((gmat * lmat).astype(BF16))
                        ce.append((c_g * jnp.exp2(colb)).astype(BF16))
                        bw.append((b_t * wrow).astype(BF16))
                    y_p = _dot(jnp.concatenate(gl + ce, axis=1),
                               jnp.concatenate([rhs_x, block_diag(st)], axis=0))
                    state_s[d, :, lanes] = (st * cd_rep[:, kk * LANES:(kk + 1) * LANES]
                                            + _dot(jnp.concatenate(bw, axis=1), rhs_x))
                    yacc_s[rows, lanes] += y_p
        return carry

    lax.fori_loop(0, nc, chunk_step, 0, unroll=min(4, nc))

    def finish(c, carry):
        base = pl.multiple_of(c * t, t)
        rows = pl.ds(base, t)
        y = yacc_s[rows, :] * _silu(z_ref[rows, :])
        y_ref[rows, :] = _rms(y, nw_ref[...]).astype(y_ref.dtype)
        return carry

    lax.fori_loop(0, nc, finish, 0, unroll=min(4, nc))

    if has_state_out:
        for d in range(2):
            for k in range(SSD_INNER // LANES):
                ks = slice(k * LANES, (k + 1) * LANES)
                st_ref[d, ks, :] = state_s[d, :, ks].T


def _ssd(group, z, xbc, dt, cw, cb, dtb, alog, dskip, nw, i, h0_all, y_prev, st_prev, layer_name):
    prompt = group == "prompt"
    length = SEQ if prompt else DEC_SEQ
    nb = BATCH if prompt else DEC_BATCH
    off = 0 if prompt else N_PROMPT_TOK // DEC_SEQ
    in_specs = _seq_specs(group, (SSD_INNER, SSD_CONV_DIM, DT_PAD))
    in_specs += [_const_spec(a.shape) for a in (cw, cb, dtb, alog, dskip, nw)]
    args = [z, xbc, dt, cw, cb, dtb, alog, dskip, nw]
    y_spec = pl.BlockSpec((length, SSD_INNER), lambda b: (b + off, 0))
    y_shape = jax.ShapeDtypeStruct((N_TOK, SSD_INNER), BF16)
    st_block = (None, None, 2, SSD_INNER, SSD_STATE)
    aliases = {}
    if prompt:
        out_specs = [y_spec, pl.BlockSpec(st_block, lambda b: (b, i, 0, 0, 0))]
        out_shape = [y_shape, jax.ShapeDtypeStruct((nb, N_SSM_LAYERS, 2, SSD_INNER, SSD_STATE), F32)]
        if st_prev is not None:
            in_specs.append(pl.BlockSpec(memory_space=pl.ANY))
            args.append(st_prev)
            aliases = {len(args) - 1: 1}
    else:
        in_specs += [pl.BlockSpec(st_block, lambda b: (b, i, 0, 0, 0)),
                     pl.BlockSpec(memory_space=pl.ANY)]
        args += [h0_all, y_prev]
        out_specs = [y_spec]
        out_shape = [y_shape]
        aliases = {len(args) - 1: 0}
    halo = SUBLANES
    nc = length // SSD_CHUNK
    scratch = [
        pltpu.VMEM((length + 2 * halo, SSD_CONV_DIM), F32),
        pltpu.VMEM((length, SSD_INNER), F32),
        pltpu.VMEM((length, 2 * SSD_GROUPS * SSD_STATE), F32),
        pltpu.VMEM((length, DT_PAD), F32),
        pltpu.VMEM((nc, DT_PAD, SSD_CHUNK), F32),
        pltpu.VMEM((nc, DT_PAD, SSD_CHUNK), F32),
        pltpu.VMEM((nc, SUBLANES, DT_PAD), F32),
        pltpu.VMEM((length, SSD_INNER), F32),
        pltpu.VMEM((2, SSD_STATE, SSD_INNER), F32),
    ]
    return pl.pallas_call(
        functools.partial(_ssd_kernel, length=length, has_h0=not prompt, has_state_out=prompt,
                          n_alias=len(aliases)),
        grid=(nb,),
        in_specs=in_specs,
        out_specs=out_specs,
        out_shape=out_shape,
        scratch_shapes=scratch,
        input_output_aliases=aliases,
        compiler_params=_cparams(1),
        name=f"ssd_{group}_{layer_name}",
    )(*args)


def _lru_kernel(*refs, length, has_h0, has_state_out, n_alias):
    it = iter(refs)
    xl_ref, gl_ref = next(it), next(it)
    cw_ref, cb_ref, wa_ref, wx_ref, ba_ref, bx_ref, lam_ref = (next(it) for _ in range(7))
    h0_ref = next(it) if has_h0 else None
    for _ in range(n_alias):
        next(it)
    o_ref = next(it)
    st_ref = next(it) if has_state_out else None
    pad_s, xc_s, a_s, u_s, h_s = (next(it) for _ in range(5))

    t = 128
    nc = length // t
    halo = SUBLANES
    n_tiles = LRU_WIDTH // MXU_DIM

    _fill_padded(pad_s, xl_ref[...], length, halo)

    def prep(c, carry):
        base = pl.multiple_of(c * t, t)
        for c0 in range(0, LRU_WIDTH, MXU_DIM):
            cols = slice(c0, c0 + MXU_DIM)
            xc_s[pl.ds(base, t), cols] = _short_conv_chunk(
                pad_s, cw_ref, cb_ref, base, t, halo, LRU_CONV, 2, cols)
        return carry

    lax.fori_loop(0, nc, prep, 0, unroll=2)

    row8 = lax.broadcasted_iota(jnp.int32, (SUBLANES, LRU_WIDTH), 0)
    n_groups = length // SUBLANES

    for d in range(2):
        log_a_unit = (-LRU_C) * _softplus(-lam_ref[d:d + 1, :])

        def gates(c, carry, d=d, log_a_unit=log_a_unit):
            base = pl.multiple_of(c * t, t)
            rows = pl.ds(base, t)
            xc = xc_s[rows, :]
            xb = xc.astype(BF16)
            ra, ri = [], []
            for j in range(n_tiles):
                js = slice(j * MXU_DIM, (j + 1) * MXU_DIM)
                ra.append(_dot(xb[:, js], wa_ref[d, j]))
                ri.append(_dot(xb[:, js], wx_ref[d, j]))
            r = _sigmoid_tanh(jnp.concatenate(ra, axis=1) + ba_ref[d:d + 1, :])
            gi = _sigmoid_tanh(jnp.concatenate(ri, axis=1) + bx_ref[d:d + 1, :])
            log_a = r * log_a_unit
            a = jnp.exp2(r * (log_a_unit * LOG2E))
            gap = -jnp.tanh(log_a) * (a * a + 1.0)
            root = jnp.where(gap > 0.0, gap * lax.rsqrt(gap), 0.0)
            a_s[rows, :] = a
            u_s[rows, :] = root * gi * xc
            return carry

        lax.fori_loop(0, nc, gates, 0, unroll=min(4, nc))

        if has_h0:
            carry0 = jnp.broadcast_to(h0_ref[d:d + 1, :], (SUBLANES, LRU_WIDTH))
        else:
            carry0 = jnp.zeros((SUBLANES, LRU_WIDTH), F32)

        def scan(gi_, carry, d=d):
            g = gi_ if d == 0 else n_groups - 1 - gi_
            base = pl.multiple_of(g * SUBLANES, SUBLANES)
            rows = pl.ds(base, SUBLANES)
            av = a_s[rows, :]
            uv = u_s[rows, :]
            for k in (1, 2, 4):
                if d == 0:
                    shift, valid = k, row8 >= k
                else:
                    shift, valid = SUBLANES - k, row8 < SUBLANES - k
                a_sh = pltpu.roll(av, shift, 0)
                u_sh = pltpu.roll(uv, shift, 0)
                uv = jnp.where(valid, av * u_sh + uv, uv)
                av = jnp.where(valid, av * a_sh, av)
            h = av * carry + uv
            if d == 0:
                h_s[rows, :] = h
                edge = h[SUBLANES - 1:SUBLANES, :]
            else:
                h_s[rows, :] += h
                edge = h[0:1, :]
            return jnp.broadcast_to(edge, (SUBLANES, LRU_WIDTH))

        final = lax.fori_loop(0, n_groups, scan, carry0, unroll=4)
        if has_state_out:
            st_ref[d:d + 1, :] = final[0:1, :]

    def finish(c, carry):
        base = pl.multiple_of(c * t, t)
        rows = pl.ds(base, t)
        o_ref[rows, :] = (_gelu_tanh(gl_ref[rows, :]) * h_s[rows, :]).astype(o_ref.dtype)
        return carry

    lax.fori_loop(0, nc, finish, 0, unroll=2)


def _lru(group, xl, gl, cw, cb, wa, wx, ba, bx, lam, i, h0_all, o_prev, st_prev, layer_name):
    prompt = group == "prompt"
    length = SEQ if prompt else DEC_SEQ
    nb = BATCH if prompt else DEC_BATCH
    off = 0 if prompt else N_PROMPT_TOK // DEC_SEQ
    in_specs = _seq_specs(group, (LRU_WIDTH, LRU_WIDTH))
    in_specs += [_const_spec(a.shape) for a in (cw, cb, wa, wx, ba, bx, lam)]
    args = [xl, gl, cw, cb, wa, wx, ba, bx, lam]
    o_spec = pl.BlockSpec((length, LRU_WIDTH), lambda b: (b + off, 0))
    o_shape = jax.ShapeDtypeStruct((N_TOK, LRU_WIDTH), BF16)
    st_block = (None, None, 2, LRU_WIDTH)
    aliases = {}
    if prompt:
        out_specs = [o_spec, pl.BlockSpec(st_block, lambda b: (b, i, 0, 0))]
        out_shape = [o_shape, jax.ShapeDtypeStruct((nb, N_SSM_LAYERS, 2, LRU_WIDTH), F32)]
        if st_prev is not None:
            in_specs.append(pl.BlockSpec(memory_space=pl.ANY))
            args.append(st_prev)
            aliases = {len(args) - 1: 1}
    else:
        in_specs += [pl.BlockSpec(st_block, lambda b: (b, i, 0, 0)),
                     pl.BlockSpec(memory_space=pl.ANY)]
        args += [h0_all, o_prev]
        out_specs = [o_spec]
        out_shape = [o_shape]
        aliases = {len(args) - 1: 0}
    halo = SUBLANES
    scratch = [pltpu.VMEM((length + 2 * halo, LRU_WIDTH), F32)]
    scratch += [pltpu.VMEM((length, LRU_WIDTH), F32) for _ in range(4)]
    return pl.pallas_call(
        functools.partial(_lru_kernel, length=length, has_h0=not prompt, has_state_out=prompt,
                          n_alias=len(aliases)),
        grid=(nb,),
        in_specs=in_specs,
        out_specs=out_specs,
        out_shape=out_shape,
        scratch_shapes=scratch,
        input_output_aliases=aliases,
        compiler_params=_cparams(1),
        name=f"lru_{group}_{layer_name}",
    )(*args)


def _confconv_kernel(*refs, length, aliased):
    it = iter(refs)
    a_ref, g_ref, w_ref, b_ref, lng_ref, lnb_ref = (next(it) for _ in range(6))
    if aliased:
        next(it)
    o_ref = next(it)
    pad_s, acc_s = next(it), next(it)

    t = 128
    nc = length // t
    halo = 2 * SUBLANES
    left = (CONF_K - 1) // 2
    cblk = MXU_DIM
    n = t + 2 * halo

    _fill_padded(pad_s, a_ref[...] * _sigmoid_tanh(g_ref[...]), length, halo)

    def step(c, carry):
        base = pl.multiple_of(c * t, t)
        rows = pl.ds(base, t)
        for cb0 in range(0, CONV_WIDTH, cblk):
            cs = slice(cb0, cb0 + cblk)
            win = pad_s[pl.ds(base, n), cs]
            acc = jnp.broadcast_to(b_ref[:, cs], (t, cblk))
            for s in range(SUBLANES):
                shifted = win if s == 0 else pltpu.roll(win, n - s, 0)
                for m in range(n // SUBLANES):
                    j = SUBLANES * m + s - halo + left
                    if 0 <= j < CONF_K and SUBLANES * m + t <= n:
                        acc = acc + shifted[SUBLANES * m:SUBLANES * m + t] * w_ref[j:j + 1, cs]
            acc_s[:, cs] = acc
        cv = acc_s[...]
        mu = jnp.mean(cv, axis=-1, keepdims=True)
        xc = cv - mu
        var = jnp.mean(xc * xc, axis=-1, keepdims=True)
        y = (xc * lax.rsqrt(var + EPS)) * lng_ref[...] + lnb_ref[...]
        o_ref[rows, :] = _silu(y).astype(o_ref.dtype)
        return carry

    lax.fori_loop(0, nc, step, 0, unroll=2)


def _confconv(group, ga, gg, w, b, lng, lnb, o_prev, layer_name):
    prompt = group == "prompt"
    length = SEQ if prompt else DEC_SEQ
    nb = BATCH if prompt else DEC_BATCH
    off = 0 if prompt else N_PROMPT_TOK // DEC_SEQ
    in_specs = _seq_specs(group, (CONV_WIDTH, CONV_WIDTH))
    in_specs += [_const_spec(a.shape) for a in (w, b, lng, lnb)]
    args = [ga, gg, w, b, lng, lnb]
    aliases = {}
    if not prompt:
        in_specs.append(pl.BlockSpec(memory_space=pl.ANY))
        args.append(o_prev)
        aliases = {len(args) - 1: 0}
    halo = 2 * SUBLANES
    return pl.pallas_call(
        functools.partial(_confconv_kernel, length=length, aliased=not prompt),
        grid=(nb,),
        in_specs=in_specs,
        out_specs=pl.BlockSpec((length, CONV_WIDTH), lambda b: (b + off, 0)),
        out_shape=jax.ShapeDtypeStruct((N_TOK, CONV_WIDTH), BF16),
        scratch_shapes=[pltpu.VMEM((length + 2 * halo, CONV_WIDTH), F32),
                        pltpu.VMEM((128, CONV_WIDTH), F32)],
        input_output_aliases=aliases,
        compiler_params=_cparams(1),
        name=f"confconv_{group}_{layer_name}",
    )(*args)


def _rope(x, cos_t, sin_t):
    lane = lax.broadcasted_iota(jnp.int32, x.shape, 1)
    quarter = ROPE_AXIS_DIM // 2
    partner = jnp.where((lane & quarter) == 0,
                        pltpu.roll(x, HEAD_DIM - quarter, 1), pltpu.roll(x, quarter, 1))
    return x * cos_t + partner * sin_t


def _attn_kernel(*refs, length, n_ctx, use_rope, has_cache_out, n_alias, qb):
    it = iter(refs)
    q_ref, k_ref, v_ref, qg_ref, kg_ref = (next(it) for _ in range(5))
    if n_ctx:
        ck_ref, cv_ref = next(it), next(it)
    if use_rope:
        cos_ref, sin_ref = next(it), next(it)
    for _ in range(n_alias):
        next(it)
    o_ref = next(it)
    kn_ref, vc_ref = (next(it), next(it)) if has_cache_out else (None, None)
    kall_s, vall_s, s_s = next(it), next(it), next(it)

    nq = length // qb
    rep = N_HEADS // N_KV_HEADS
    scale = HEAD_DIM ** -0.5

    for g in range(N_KV_HEADS):
        gs = slice(g * HEAD_DIM, (g + 1) * HEAD_DIM)
        kn = _rms(k_ref[:, gs], kg_ref[...])
        if kn_ref is not None:
            kn_ref[:, gs] = kn
            vc_ref[:, gs] = v_ref[:, gs]
        if use_rope:
            kn = _rope(kn, cos_ref[...], sin_ref[...])
        if n_ctx:
            kall_s[g, 0:n_ctx, :] = ck_ref[:, gs].astype(BF16)
            vall_s[g, 0:n_ctx, 0:HEAD_DIM] = cv_ref[:, gs].astype(BF16)
        kall_s[g, n_ctx:n_ctx + length, :] = kn.astype(BF16)
        vall_s[g, n_ctx:n_ctx + length, 0:HEAD_DIM] = v_ref[:, gs].astype(BF16)
        vall_s[g, :, HEAD_DIM:] = jnp.ones((n_ctx + length, HEAD_DIM), BF16)

    c_exp = scale * LOG2E

    def scores(i, slot):
        rows = pl.ds(pl.multiple_of(i * qb, qb), qb)
        for g in range(N_KV_HEADS):
            qs = []
            for r in range(rep):
                hs = slice((g * rep + r) * HEAD_DIM, (g * rep + r + 1) * HEAD_DIM)
                qn = _rms(q_ref[rows, hs], qg_ref[...])
                if use_rope:
                    qn = _rope(qn, cos_ref[rows, :], sin_ref[rows, :])
                qs.append(qn.astype(BF16))
            s_s[slot, g] = _dot_nt(jnp.concatenate(qs, axis=0), kall_s[g])

    def outputs(i, slot):
        rows = pl.ds(pl.multiple_of(i * qb, qb), qb)
        for g in range(N_KV_HEADS):
            s = s_s[slot, g]
            m = jnp.max(s, axis=-1, keepdims=True)
            p = jnp.exp2((s - m) * c_exp)
            ov = _dot(p.astype(BF16), vall_s[g])
            o = ov[:, :HEAD_DIM] / ov[:, HEAD_DIM:]
            for r in range(rep):
                hs = slice((g * rep + r) * HEAD_DIM, (g * rep + r + 1) * HEAD_DIM)
                o_ref[rows, hs] = o[r * qb:(r + 1) * qb].astype(o_ref.dtype)

    scores(0, 0)

    def pair(j, carry):
        i = 2 * j
        scores(i + 1, 1)
        outputs(i, 0)
        scores(jnp.minimum(i + 2, nq - 1), 0)
        outputs(i + 1, 1)
        return carry

    lax.fori_loop(0, nq // 2, pair, 0, unroll=min(2, nq // 2))


def _attn(group, q, k, v, qg, kg, i, ck_all, cv_all, cos_t, sin_t, o_prev, kc_prev, vc_prev,
          layer_name):
    prompt = group == "prompt"
    length = SEQ if prompt else DEC_SEQ
    nb = BATCH if prompt else DEC_BATCH
    off = 0 if prompt else N_PROMPT_TOK // DEC_SEQ
    kvw = N_KV_HEADS * HEAD_DIM
    n_ctx = 0 if prompt else PAST_LEN
    qb = 128
    rep = N_HEADS // N_KV_HEADS
    in_specs = _seq_specs(group, (D_MODEL, kvw, kvw))
    in_specs += [_const_spec(qg.shape), _const_spec(kg.shape)]
    args = [q, k, v, qg, kg]
    o_spec = pl.BlockSpec((length, D_MODEL), lambda b: (b + off, 0))
    o_shape = jax.ShapeDtypeStruct((N_TOK, D_MODEL), BF16)
    aliases = {}
    if prompt:
        c_spec = pl.BlockSpec((None, None, length, kvw), lambda b: (b, i, 0, 0))
        c_shape = jax.ShapeDtypeStruct((nb, N_ATTN_LAYERS, length, kvw), F32)
        out_specs = [o_spec, c_spec, c_spec]
        out_shape = [o_shape, c_shape, c_shape]
        if kc_prev is not None:
            in_specs += [pl.BlockSpec(memory_space=pl.ANY)] * 2
            args += [kc_prev, vc_prev]
            aliases = {len(args) - 2: 1, len(args) - 1: 2}
    else:
        ctx_spec = pl.BlockSpec((None, None, n_ctx, kvw), lambda b: (b, i, 0, 0))
        in_specs += [ctx_spec, ctx_spec, _const_spec(cos_t.shape), _const_spec(sin_t.shape),
                     pl.BlockSpec(memory_space=pl.ANY)]
        args += [ck_all, cv_all, cos_t, sin_t, o_prev]
        out_specs = [o_spec]
        out_shape = [o_shape]
        aliases = {len(args) - 1: 0}
    return pl.pallas_call(
        functools.partial(_attn_kernel, length=length, n_ctx=n_ctx, use_rope=not prompt,
                          has_cache_out=prompt, n_alias=len(aliases), qb=qb),
        grid=(nb,),
        in_specs=in_specs,
        out_specs=out_specs,
        out_shape=out_shape,
        scratch_shapes=[pltpu.VMEM((N_KV_HEADS, n_ctx + length, HEAD_DIM), BF16),
                        pltpu.VMEM((N_KV_HEADS, n_ctx + length, 2 * HEAD_DIM), BF16),
                        pltpu.VMEM((2, N_KV_HEADS, rep * qb, n_ctx + length), F32)],
        input_output_aliases=aliases,
        compiler_params=_cparams(1),
        name=f"attn_{group}_{layer_name}",
    )(*args)


def _rope_tables(rows):
    row_pos = jnp.repeat(jnp.arange(rows, dtype=F32), GRID_W)
    col_pos = jnp.tile(jnp.arange(GRID_W, dtype=F32), rows)
    inv_freq = jnp.power(ROPE_THETA, -jnp.arange(0, ROPE_AXIS_DIM, 2, dtype=F32) / ROPE_AXIS_DIM)
    ang_r = row_pos[:, None] * inv_freq
    ang_c = col_pos[:, None] * inv_freq
    cos_t = jnp.concatenate([jnp.cos(ang_r), jnp.cos(ang_r), jnp.cos(ang_c), jnp.cos(ang_c)], axis=1)
    sin_t = jnp.concatenate([-jnp.sin(ang_r), jnp.sin(ang_r), -jnp.sin(ang_c), jnp.sin(ang_c)], axis=1)
    return cos_t, sin_t


def _block_diag_tiles(w):
    per = MXU_DIM // LRU_BW
    n_tiles = LRU_BLOCKS // per
    w = w.reshape(2, n_tiles, per, LRU_BW, LRU_BW)
    eye = jnp.eye(per, dtype=w.dtype)
    tiles = jnp.einsum('dtpio,pq->dtpiqo', w, eye)
    return tiles.reshape(2, n_tiles, MXU_DIM, MXU_DIM).astype(BF16)


def _row(v):
    return v.reshape(1, -1)


def kernel(x_prompt, x_sample, state_ssd, state_lru, cache_k, cache_v, c, c_ctx,
           w_mod, b_mod, norm_g, w_in_ssm, ssd_conv_w, ssd_conv_b, ssd_a_log, ssd_dt_bias,
           ssd_d, ssd_norm_w, lru_conv_w, lru_conv_b, lru_wa, lru_ba, lru_wx, lru_bx,
           lru_lambda, w_out_ssm, w_in_ca, conf_dw_w, conf_dw_b, conf_ln_g, conf_ln_b,
           q_norm_g, k_norm_g, w_out_ca, ffn_w_in, ffn_conv_w, ffn_conv_b, ffn_w_out):
    x = (x_prompt.reshape(N_PROMPT_TOK, D_MODEL), x_sample.reshape(N_SAMPLE_TOK, D_MODEL))
    cvec = jnp.concatenate(
        [c_ctx[None], c, jnp.zeros((N_MOD_ROWS - 1 - DEC_BATCH, D_MODEL), F32)], axis=0)
    mods = _modulation_all(cvec, w_mod, b_mod)
    cos_t, sin_t = _rope_tables(DEC_SEQ // GRID_W)
    kvw = N_KV_HEADS * HEAD_DIM

    w_ssm_in = jnp.pad(w_in_ssm, ((0, 0), (0, 0), (0, DT_COL0))).astype(BF16)
    ssd_h0 = state_ssd.reshape(DEC_BATCH, N_SSM_LAYERS, 2, SSD_INNER, SSD_STATE)
    ck_all = cache_k.reshape(DEC_BATCH, N_ATTN_LAYERS, PAST_LEN, kvw)
    cv_all = cache_v.reshape(DEC_BATCH, N_ATTN_LAYERS, PAST_LEN, kvw)

    ssd_st = lru_st = kc = vc = None
    for layer in range(DEPTH):
        i = layer // 2
        m = mods[layer]
        name = f"l{layer}"
        g0, g1, g2, g3 = (_row(norm_g[layer, j]) for j in range(4))
        if layer % 2 == 0:
            z, xbc, xl, gl, dt = _inproj(
                x, g0, m[0], m[1], w_ssm_in, i,
                (SSD_INNER, SSD_CONV_DIM, LRU_WIDTH, LRU_WIDTH, DT_PAD), f"inproj_ssm_{name}",
                rot=(SSM_O2, 2 * SSD_HEADS))
            dtb = jnp.pad(ssd_dt_bias[i].reshape(1, -1), ((0, 0), (DT_COL0, 0)))
            alog = jnp.pad(ssd_a_log[i].reshape(1, -1), ((0, 0), (DT_COL0, 0)))
            dskip = _row(jnp.repeat(ssd_d[i], SSD_HEADDIM))
            ssd_args = (ssd_conv_w[i], _row(ssd_conv_b[i]), dtb, alog, dskip, _row(ssd_norm_w[i]))
            y, ssd_st = _ssd("prompt", z, xbc, dt, *ssd_args, i, None, None, ssd_st, name)
            (y,) = _ssd("sample", z, xbc, dt, *ssd_args, i, ssd_h0, y, None, name)
            lru_args = (lru_conv_w[i], _row(lru_conv_b[i]), _block_diag_tiles(lru_wa[i]),
                        _block_diag_tiles(lru_wx[i]), lru_ba[i], lru_bx[i], lru_lambda[i])
            yl, lru_st = _lru("prompt", xl, gl, *lru_args, i, None, None, lru_st, name)
            (yl,) = _lru("sample", xl, gl, *lru_args, i, state_lru, yl, None, name)
            xo, w_ffn_in, w_ffn_out = _outproj(y, yl, w_out_ssm, i, x, g1, m[2], ffn_w_in, ffn_w_out,
                                               layer, f"outproj_ssm_{name}")
        else:
            ga, gg, q, k, v = _inproj(
                x, g0, m[0], m[1], w_in_ca, i,
                (CONV_WIDTH, CONV_WIDTH, N_HEADS * HEAD_DIM, kvw, kvw), f"inproj_ca_{name}")
            conv_args = (conf_dw_w[i], _row(conf_dw_b[i]), _row(conf_ln_g[i]), _row(conf_ln_b[i]))
            cvo = _confconv("prompt", ga, gg, *conv_args, None, name)
            cvo = _confconv("sample", ga, gg, *conv_args, cvo, name)
            qg, kg = _row(q_norm_g[i]), _row(k_norm_g[i])
            o, kc, vc = _attn("prompt", q, k, v, qg, kg, i, None, None, None, None, None, kc, vc,
                              name)
            (o,) = _attn("sample", q, k, v, qg, kg, i, ck_all, cv_all, cos_t, sin_t, o, None, None,
                         name)
            xo, w_ffn_in, w_ffn_out = _outproj(cvo, o, w_out_ca, i, x, g1, m[2], ffn_w_in, ffn_w_out,
                                               layer, f"outproj_ca_{name}")
        x = (_ffn(xo, g2, m[3], m[4], w_ffn_in, ffn_conv_w[layer], _row(ffn_conv_b[layer]),
                  w_ffn_out, g3, m[5], f"ffn_{name}"),)

    xp = x[0][:N_PROMPT_TOK].reshape(BATCH, SEQ, D_MODEL)
    xs = x[0][N_PROMPT_TOK:].reshape(DEC_BATCH, DEC_SEQ, D_MODEL)
    return (xp, xs,
            ssd_st.reshape(BATCH, N_SSM_LAYERS, 2, SSD_HEADS, SSD_HEADDIM, SSD_STATE),
            lru_st,
            kc.reshape(BATCH, N_ATTN_LAYERS, SEQ, N_KV_HEADS, HEAD_DIM),
            vc.reshape(BATCH, N_ATTN_LAYERS, SEQ, N_KV_HEADS, HEAD_DIM))
```

```python
import functools
import math

import jax
import jax.numpy as jnp
from jax import lax
from jax.experimental import pallas as pl
from jax.experimental.pallas import tpu as pltpu

F32 = jnp.float32
BF16 = jnp.bfloat16

D_MODEL = 1024
BATCH = 16
SEQ = 256
DEPTH = 4
N_SSM_LAYERS = (DEPTH + 1) // 2
N_ATTN_LAYERS = DEPTH // 2
DEC_BATCH = 4
DEC_SEQ = 1024
PAST_LEN = 512
GRID_W = 64
EPS = 1e-6
SSD_HEADDIM = 64
SSD_INNER = D_MODEL
SSD_HEADS = SSD_INNER // SSD_HEADDIM
SSD_GROUPS = 2
SSD_STATE = 128
SSD_CONV = 4
SSD_CHUNK = 128
SSD_CONV_DIM = SSD_INNER + 2 * SSD_GROUPS * SSD_STATE
LRU_WIDTH = D_MODEL
LRU_BW = 64
LRU_BLOCKS = LRU_WIDTH // LRU_BW
LRU_CONV = 4
LRU_C = 8.0
CONV_WIDTH = D_MODEL
CONF_K = 31
HEAD_DIM = 128
N_HEADS = D_MODEL // HEAD_DIM
N_KV_HEADS = 2
ROPE_THETA = 10000.0
ROPE_AXIS_DIM = HEAD_DIM // 2
D_FF = 2816
FFN_CONV = 3

N_PROMPT_TOK = BATCH * SEQ
N_SAMPLE_TOK = DEC_BATCH * DEC_SEQ
N_TOK = N_PROMPT_TOK + N_SAMPLE_TOK
N_MOD_ROWS = 8
LANES = 128
SUBLANES = 8
MXU_DIM = 256
DT_PAD = LANES
SSM_O2 = SSD_INNER + SSD_CONV_DIM
DT_COL0 = DT_PAD - 2 * SSD_HEADS
LOG2E = 1.0 / math.log(2.0)
VMEM_LIMIT = 58 * 1024 * 1024

TM_LINEAR = 512
TM_FFN = 1024


def _cparams(n_axes):
    return pltpu.CompilerParams(
        dimension_semantics=("arbitrary",) * n_axes,
        vmem_limit_bytes=VMEM_LIMIT)


def _const_spec(shape):
    nd = len(shape)
    return pl.BlockSpec(shape, lambda *_: (0,) * nd, pipeline_mode=pl.Buffered(1))


def _layer_spec(shape, layer):
    nd = len(shape) - 1
    return pl.BlockSpec((None,) + tuple(shape[1:]), lambda *_: (layer,) + (0,) * nd,
                        pipeline_mode=pl.Buffered(1))


def _mod_row(i, tm):
    start = i * tm
    return jnp.where(start < N_PROMPT_TOK, 0, 1 + (start - N_PROMPT_TOK) // DEC_SEQ)


def _mod_spec(tm):
    return pl.BlockSpec((None, 1, D_MODEL), lambda i: (_mod_row(i, tm), 0, 0))


def _x_specs(xs, tm):
    if len(xs) == 1:
        return [pl.BlockSpec((tm, D_MODEL), lambda i: (i, 0))]
    n_p = N_PROMPT_TOK // tm
    return [pl.BlockSpec((tm, D_MODEL), lambda i: (jnp.minimum(i, n_p - 1), 0)),
            pl.BlockSpec((tm, D_MODEL), lambda i: (jnp.maximum(i - n_p, 0), 0))]


def _x_tile(x_refs, tm):
    if len(x_refs) == 1:
        return x_refs[0][...]
    is_prompt = pl.program_id(0) * tm < N_PROMPT_TOK
    return jnp.where(is_prompt, x_refs[0][...], x_refs[1][...])


def _sigmoid(x):
    return jax.nn.sigmoid(x)


def _sigmoid_tanh(x):
    return 0.5 * jnp.tanh(0.5 * x) + 0.5


def _silu(x):
    return x * _sigmoid(x)


def _softplus(x):
    return jnp.maximum(x, 0.0) + jnp.log1p(jnp.exp(-jnp.abs(x)))


def _gelu_tanh(x):
    return 0.5 * x * (1.0 + jnp.tanh(math.sqrt(2.0 / math.pi) * (x + 0.044715 * (x * x * x))))


def _rms(x, g):
    ms = jnp.mean(x * x, axis=-1, keepdims=True)
    return (x * lax.rsqrt(ms + EPS)) * g


def _dot(a, b):
    return jnp.dot(a, b, preferred_element_type=F32)


def _dot_nt(a, b):
    return lax.dot_general(a, b, (((1,), (1,)), ((), ())), preferred_element_type=F32)


def _mod_kernel(c_ref, w_ref, b_ref, o_ref):
    c = c_ref[...]
    s = _silu(c).astype(BF16)
    o_ref[...] = _dot(s, w_ref[...].astype(BF16)) + b_ref[...]


def _modulation_all(cvec, w_mod, b_mod):
    tn = 1536
    n_out = 6 * D_MODEL
    out = pl.pallas_call(
        _mod_kernel,
        grid=(DEPTH, n_out // tn),
        in_specs=[
            pl.BlockSpec((N_MOD_ROWS, D_MODEL), lambda l, j: (0, 0)),
            pl.BlockSpec((None, D_MODEL, tn), lambda l, j: (l, 0, j)),
            pl.BlockSpec((None, 1, tn), lambda l, j: (l, 0, j)),
        ],
        out_specs=pl.BlockSpec((None, N_MOD_ROWS, tn), lambda l, j: (l, 0, j)),
        out_shape=jax.ShapeDtypeStruct((DEPTH, N_MOD_ROWS, n_out), F32),
        compiler_params=_cparams(2),
        name="modulation",
    )(cvec, w_mod, b_mod.reshape(DEPTH, 1, n_out))
    out = out.reshape(DEPTH, N_MOD_ROWS, 6, 1, D_MODEL)
    return jnp.transpose(out, (0, 2, 1, 3, 4))


def _inproj_kernel(*refs, n_x, widths, chunk, rot):
    x_refs = refs[:n_x]
    g_ref, shift_ref, scale_ref, w_ref, fw1_ref, fw2_ref = refs[n_x:n_x + 6]
    rest = refs[n_x + 6:]
    o_refs = rest[:len(widths)]
    fb1_ref, fb2_ref, h_ref = rest[len(widths):len(widths) + 3]
    rest = rest[2:]
    fb1_ref[...] = fw1_ref[...].astype(BF16)
    fb2_ref[...] = fw2_ref[...].astype(BF16)
    if rot is not None:
        rot_start, rot_by = rot
        wr_ref = rest[len(widths) + 1]
        rot_w = w_ref.shape[1] - rot_start

        @pl.when(pl.program_id(0) == 0)
        def _():
            wr_ref[...] = pltpu.roll(w_ref[:, rot_start:], rot_w - rot_by, 1)
    elif w_ref.dtype != BF16:
        wb_ref = rest[len(widths) + 1]

        @pl.when(pl.program_id(0) == 0)
        def _():
            wb_ref[...] = w_ref[...].astype(BF16)
        w_ref = wb_ref

    x = _x_tile(x_refs, x_refs[0].shape[0])
    h = _rms(x, g_ref[...]) * (1.0 + scale_ref[...]) + shift_ref[...]
    h_ref[...] = h.astype(BF16)
    off = 0
    for o_ref, n in zip(o_refs, widths):
        for c0 in range(0, n, chunk):
            c1 = min(c0 + chunk, n)
            if rot is not None and off >= rot_start:
                w_blk = wr_ref[:, off - rot_start + c0:off - rot_start + c1]
            else:
                w_blk = w_ref[:, off + c0:off + c1]
            o_ref[:, c0:c1] = _dot(h_ref[...], w_blk)
        off += n


def _inproj(xs, g, shift, scale, w, layer, widths, fw1, fw2, flayer, name, rot=None):
    tm = TM_LINEAR
    assert sum(widths) == w.shape[2] and all(n % LANES == 0 for n in widths)
    steps = N_TOK // tm
    r1, r2 = fw1.shape[1] // steps, fw2.shape[1] // steps
    assert r1 * steps == fw1.shape[1] and r2 * steps == fw2.shape[1] and r1 % 16 == 0 and r2 % 16 == 0
    scratch = [pltpu.VMEM((tm, D_MODEL), BF16)]
    if rot is not None:
        scratch.append(pltpu.VMEM((D_MODEL, w.shape[2] - rot[0]), BF16))
    elif w.dtype != BF16:
        scratch.append(pltpu.VMEM(w.shape[1:], BF16))
    return pl.pallas_call(
        functools.partial(_inproj_kernel, n_x=len(xs), widths=tuple(widths), chunk=512, rot=rot),
        grid=(N_TOK // tm,),
        in_specs=_x_specs(xs, tm) + [
            _const_spec((1, D_MODEL)), _mod_spec(tm), _mod_spec(tm), _layer_spec(w.shape, layer),
            pl.BlockSpec((None, r1, fw1.shape[2]), lambda i: (flayer, i, 0)),
            pl.BlockSpec((None, r2, fw2.shape[2]), lambda i: (flayer, i, 0))],
        out_specs=[pl.BlockSpec((tm, n), lambda i: (i, 0)) for n in widths] + [
            pl.BlockSpec((r1, fw1.shape[2]), lambda i: (i, 0)),
            pl.BlockSpec((r2, fw2.shape[2]), lambda i: (i, 0))],
        out_shape=[jax.ShapeDtypeStruct((N_TOK, n), F32) for n in widths] + [
            jax.ShapeDtypeStruct(fw1.shape[1:], BF16), jax.ShapeDtypeStruct(fw2.shape[1:], BF16)],
        scratch_shapes=scratch,
        compiler_params=_cparams(1),
        name=name,
    )(*xs, g, shift, scale, w, fw1, fw2)


def _outproj_kernel(a_ref, b_ref, w_ref, *rest, n_x):
    x_refs = rest[:n_x]
    g_ref, gate_ref, o_ref, wb_ref = rest[n_x:]

    @pl.when(pl.program_id(0) == 0)
    def _():
        wb_ref[...] = w_ref[...].astype(BF16)

    ka = a_ref.shape[1]
    acc = _dot(a_ref[...], wb_ref[0:ka, :]) + _dot(b_ref[...], wb_ref[ka:, :])
    o_ref[...] = _x_tile(x_refs, o_ref.shape[0]) + gate_ref[...] * _rms(acc, g_ref[...])


def _outproj(a, b, w, layer, xs, g, gate, name):
    tm = TM_LINEAR
    return pl.pallas_call(
        functools.partial(_outproj_kernel, n_x=len(xs)),
        grid=(N_TOK // tm,),
        in_specs=[
            pl.BlockSpec((tm, a.shape[1]), lambda i: (i, 0)),
            pl.BlockSpec((tm, b.shape[1]), lambda i: (i, 0)),
            _layer_spec(w.shape, layer),
        ] + _x_specs(xs, tm) + [_const_spec((1, D_MODEL)), _mod_spec(tm)],
        out_specs=pl.BlockSpec((tm, D_MODEL), lambda i: (i, 0)),
        out_shape=jax.ShapeDtypeStruct((N_TOK, D_MODEL), F32),
        scratch_shapes=[pltpu.VMEM(w.shape[1:], BF16)],
        compiler_params=_cparams(1),
        name=name,
    )(a, b, w, *xs, g, gate)


def _ffn_kernel(x_ref, g2_ref, shift_ref, scale_ref, wi_ref, cw_ref, cb_ref,
                wo_ref, g3_ref, gate_ref, o_ref, h_ref, act_ref, *, chunk, row_blk):
    tm = x_ref.shape[0]
    i = pl.program_id(0)
    h = _rms(x_ref[...], g2_ref[...]) * (1.0 + scale_ref[...]) + shift_ref[...]
    h_ref[...] = h.astype(BF16)
    lseq = jnp.where(i * tm < N_PROMPT_TOK, SEQ, DEC_SEQ)
    pos = lax.broadcasted_iota(jnp.int32, (tm, 1), 0) & (lseq - 1)
    first = pos == 0
    last = pos == lseq - 1
    for c0 in range(0, D_FF, chunk):
        c1 = min(c0 + chunk, D_FF)
        gt = _dot(h_ref[...], wi_ref[:, c0:c1])
        vl = _dot(h_ref[...], wi_ref[:, D_FF + c0:D_FF + c1])
        g_prev = jnp.where(first, 0.0, pltpu.roll(gt, 1, 0))
        g_next = jnp.where(last, 0.0, pltpu.roll(gt, tm - 1, 0))
        conv = (g_prev * cw_ref[0:1, c0:c1] + gt * cw_ref[1:2, c0:c1]
                + g_next * cw_ref[2:3, c0:c1] + cb_ref[:, c0:c1])
        act_ref[:, c0:c1] = (_silu(conv) * vl).astype(BF16)
    for r0 in range(0, tm, row_blk):
        rs = slice(r0, r0 + row_blk)
        acc = _dot(act_ref[rs, :], wo_ref[...])
        o_ref[rs, :] = x_ref[rs, :] + gate_ref[...] * _rms(acc, g3_ref[...])


def _ffn(x, g2, shift, scale, wi, cw, cb, wo, g3, gate, name):
    tm = TM_FFN
    row_spec = pl.BlockSpec((tm, D_MODEL), lambda i: (i, 0))
    return pl.pallas_call(
        functools.partial(_ffn_kernel, chunk=2 * MXU_DIM, row_blk=MXU_DIM),
        grid=(N_TOK // tm,),
        in_specs=[
            row_spec, _const_spec((1, D_MODEL)), _mod_spec(tm), _mod_spec(tm),
            _const_spec(wi.shape), _const_spec(cw.shape),
            _const_spec(cb.shape), _const_spec(wo.shape), _const_spec((1, D_MODEL)),
            _mod_spec(tm),
        ],
        out_specs=row_spec,
        out_shape=jax.ShapeDtypeStruct((N_TOK, D_MODEL), F32),
        scratch_shapes=[pltpu.VMEM((tm, D_MODEL), BF16), pltpu.VMEM((tm, D_FF), BF16)],
        compiler_params=_cparams(1),
        name=name,
    )(x, g2, shift, scale, wi, cw, cb, wo, g3, gate)


def _seq_specs(group, widths):
    if group == "prompt":
        return [pl.BlockSpec((SEQ, w), lambda b: (b, 0)) for w in widths]
    off = N_PROMPT_TOK // DEC_SEQ
    return [pl.BlockSpec((DEC_SEQ, w), lambda b: (b + off, 0)) for w in widths]


def _short_conv_chunk(pad_ref, cw_ref, cb_ref, base, t, halo, taps, left, cols):
    win = pad_ref[pl.ds(base, t + 2 * halo), cols]
    n = t + 2 * halo
    acc = cb_ref[:, cols]
    for j in range(taps):
        s = (left - j) % n
        rolled = win if s == 0 else pltpu.roll(win, s, 0)
        acc = acc + rolled[halo:halo + t] * cw_ref[j:j + 1, cols]
    return acc


def _fill_padded(pad_ref, src, length, halo):
    width = pad_ref.shape[1]
    pad_ref[0:halo, :] = jnp.zeros((halo, width), F32)
    pad_ref[halo + length:2 * halo + length, :] = jnp.zeros((halo, width), F32)
    pad_ref[halo:halo + length, :] = src


def _lane_pairs(m, first_col, n_pairs, rows):
    lane = lax.broadcasted_iota(jnp.int32, (rows, LANES), 1)
    lo_half = lane < SSD_HEADDIM
    pieces = []
    for k in range(n_pairs):
        c = first_col + 2 * k
        lo = jnp.broadcast_to(m[:, c:c + 1], (rows, LANES))
        hi = jnp.broadcast_to(m[:, c + 1:c + 2], (rows, LANES))
        pieces.append(jnp.where(lo_half, lo, hi))
    return jnp.concatenate(pieces, axis=1)


def _ssd_kernel(*refs, length, has_h0, has_state_out, n_alias):
    it = iter(refs)
    z_ref, xbc_ref, dt_ref = next(it), next(it), next(it)
    cw_ref, cb_ref, dtb_ref, alog_ref, dskip_ref, nw_ref = (next(it) for _ in range(6))
    h0_ref = next(it) if has_h0 else None
    for _ in range(n_alias):
        next(it)
    y_ref = next(it)
    st_ref = next(it) if has_state_out else None
    pad_s, xs_s, bc_s, cum_s, row_t_s, w_t_s, cd_s, yacc_s, state_s = (
        next(it) for _ in range(9))

    t = SSD_CHUNK
    nc = length // t
    halo = SUBLANES
    gw = SSD_INNER // SSD_GROUPS
    pairs_per_group = SSD_HEADS // SSD_GROUPS // 2

    _fill_padded(pad_s, xbc_ref[...], length, halo)
    a_row = -jnp.exp(alog_ref[...])

    ri = lax.broadcasted_iota(jnp.int32, (t, t), 0)
    ci = lax.broadcasted_iota(jnp.int32, (t, t), 1)
    keep = (ci <= ri, ci >= ri)
    tril = keep[0].astype(F32)
    lane = lax.broadcasted_iota(jnp.int32, (t, LANES), 1)
    lo_half = lane < SSD_HEADDIM
    fwd_cols = lane < DT_COL0 + SSD_HEADS
    fwd_rows = ri < DT_COL0 + SSD_HEADS

    def prep(c, carry):
        base = pl.multiple_of(c * t, t)
        rows = pl.ds(base, t)
        for c0 in range(0, SSD_CONV_DIM, MXU_DIM):
            cols = slice(c0, c0 + MXU_DIM)
            conv = _silu(_short_conv_chunk(pad_s, cw_ref, cb_ref, base, t, halo, SSD_CONV, 2, cols))
            if c0 < SSD_INNER:
                xs_s[rows, cols] = conv
                yacc_s[rows, cols] = conv * dskip_ref[:, cols]
            else:
                bc_s[rows, c0 - SSD_INNER:c0 - SSD_INNER + MXU_DIM] = conv
        dtsp = _softplus(dt_ref[rows, :] + dtb_ref[...])
        a_c = dtsp * a_row
        pre = jnp.dot(tril, a_c, preferred_element_type=F32, precision=lax.Precision.HIGHEST)
        suf = pre[t - 1:t, :] - pre + a_c
        cum = jnp.where(fwd_cols, pre, suf) * LOG2E
        cum_s[rows, :] = cum
        cum_t = cum.T
        dt_t = dtsp.T
        edge_col = jnp.where(fwd_rows[:, 0:1], cum_t[:, t - 1:t], cum_t[:, 0:1])
        row_t_s[c] = cum_t - jnp.log(dt_t) * LOG2E
        w_t_s[c] = dt_t * jnp.exp2(edge_col - cum_t)
        edge_row = jnp.where(fwd_cols[0:1, :], cum[t - 1:t, :], cum[0:1, :])
        cd_s[c] = jnp.broadcast_to(jnp.exp2(edge_row), (SUBLANES, LANES))
        return carry

    lax.fori_loop(0, nc, prep, 0, unroll=min(4, nc))

    for d in range(2):
        if has_h0:
            for k in range(SSD_INNER // LANES):
                ks = slice(k * LANES, (k + 1) * LANES)
                state_s[d, :, ks] = h0_ref[d, ks, :].T
        else:
            state_s[d] = jnp.zeros((SSD_STATE, SSD_INNER), F32)

    def block_diag(m):
        return jnp.concatenate([jnp.where(lo_half, m, 0.0).astype(BF16),
                                jnp.where(lo_half, 0.0, m).astype(BF16)], axis=0)

    def chunk_step(c, carry):
        for d in range(2):
            cidx = c if d == 0 else nc - 1 - c
            base = pl.multiple_of(cidx * t, t)
            rows = pl.ds(base, t)
            cum = cum_s[rows, :]
            bc = bc_s[rows, :]
            cd = cd_s[cidx][0:1, :]
            for g in range(SSD_GROUPS):
                col0 = DT_COL0 + d * SSD_HEADS + g * 2 * pairs_per_group
                b_g = bc[:, g * SSD_STATE:(g + 1) * SSD_STATE]
                c_g = bc[:, (SSD_GROUPS + g) * SSD_STATE:(SSD_GROUPS + g + 1) * SSD_STATE]
                gmat = _dot_nt(c_g.astype(BF16), b_g.astype(BF16))
                b_t = b_g.T
                cd_rep = _lane_pairs(cd, col0, pairs_per_group, 1)
                for kk in range(pairs_per_group):
                    lanes = slice(g * gw + kk * LANES, g * gw + (kk + 1) * LANES)
                    rhs_x = block_diag(xs_s[rows, lanes])
                    st = state_s[d, :, lanes]
                    gl, ce, bw = [], [], []
                    for hcur in (col0 + 2 * kk, col0 + 2 * kk + 1):
                        hrow = pl.ds(hcur, 1)
                        colb = jnp.broadcast_to(cum[:, hcur:hcur + 1], (t, t))
                        rowb = jnp.broadcast_to(row_t_s[cidx, hrow, :], (t, t))
                        wrow = jnp.broadcast_to(w_t_s[cidx, hrow, :], (t, t))
                        lmat = jnp.exp2(jnp.where(keep[d], colb - rowb, -jnp.inf))
                        gl.append((gmat * lmat).astype(BF16))
                        ce.append((c_g * jnp.exp2(colb)).astype(BF16))
                        bw.append((b_t * wrow).astype(BF16))
                    y_p = _dot(jnp.concatenate(gl + ce, axis=1),
                               jnp.concatenate([rhs_x, block_diag(st)], axis=0))
                    state_s[d, :, lanes] = (st * cd_rep[:, kk * LANES:(kk + 1) * LANES]
                                            + _dot(jnp.concatenate(bw, axis=1), rhs_x))
                    yacc_s[rows, lanes] += y_p
        return carry

    lax.fori_loop(0, nc, chunk_step, 0, unroll=min(4, nc))

    def finish(c, carry):
        base = pl.multiple_of(c * t, t)
        rows = pl.ds(base, t)
        y = yacc_s[rows, :] * _silu(z_ref[rows, :])
        y_ref[rows, :] = _rms(y, nw_ref[...]).astype(y_ref.dtype)
        return carry

    lax.fori_loop(0, nc, finish, 0, unroll=min(4, nc))

    if has_state_out:
        for d in range(2):
            for k in range(SSD_INNER // LANES):
                ks = slice(k * LANES, (k + 1) * LANES)
                st_ref[d, ks, :] = state_s[d, :, ks].T


def _ssd(group, z, xbc, dt, cw, cb, dtb, alog, dskip, nw, i, h0_all, y_prev, st_prev, layer_name):
    prompt = group == "prompt"
    length = SEQ if prompt else DEC_SEQ
    nb = BATCH if prompt else DEC_BATCH
    off = 0 if prompt else N_PROMPT_TOK // DEC_SEQ
    in_specs = _seq_specs(group, (SSD_INNER, SSD_CONV_DIM, DT_PAD))
    in_specs += [_const_spec(a.shape) for a in (cw, cb, dtb, alog, dskip, nw)]
    args = [z, xbc, dt, cw, cb, dtb, alog, dskip, nw]
    y_spec = pl.BlockSpec((length, SSD_INNER), lambda b: (b + off, 0))
    y_shape = jax.ShapeDtypeStruct((N_TOK, SSD_INNER), BF16)
    st_block = (None, None, 2, SSD_INNER, SSD_STATE)
    aliases = {}
    if prompt:
        out_specs = [y_spec, pl.BlockSpec(st_block, lambda b: (b, i, 0, 0, 0))]
        out_shape = [y_shape, jax.ShapeDtypeStruct((nb, N_SSM_LAYERS, 2, SSD_INNER, SSD_STATE), F32)]
        if st_prev is not None:
            in_specs.append(pl.BlockSpec(memory_space=pl.ANY))
            args.append(st_prev)
            aliases = {len(args) - 1: 1}
    else:
        in_specs += [pl.BlockSpec(st_block, lambda b: (b, i, 0, 0, 0)),
                     pl.BlockSpec(memory_space=pl.ANY)]
        args += [h0_all, y_prev]
        out_specs = [y_spec]
        out_shape = [y_shape]
        aliases = {len(args) - 1: 0}
    halo = SUBLANES
    nc = length // SSD_CHUNK
    scratch = [
        pltpu.VMEM((length + 2 * halo, SSD_CONV_DIM), F32),
        pltpu.VMEM((length, SSD_INNER), F32),
        pltpu.VMEM((length, 2 * SSD_GROUPS * SSD_STATE), F32),
        pltpu.VMEM((length, DT_PAD), F32),
        pltpu.VMEM((nc, DT_PAD, SSD_CHUNK), F32),
        pltpu.VMEM((nc, DT_PAD, SSD_CHUNK), F32),
        pltpu.VMEM((nc, SUBLANES, DT_PAD), F32),
        pltpu.VMEM((length, SSD_INNER), F32),
        pltpu.VMEM((2, SSD_STATE, SSD_INNER), F32),
    ]
    return pl.pallas_call(
        functools.partial(_ssd_kernel, length=length, has_h0=not prompt, has_state_out=prompt,
                          n_alias=len(aliases)),
        grid=(nb,),
        in_specs=in_specs,
        out_specs=out_specs,
        out_shape=out_shape,
        scratch_shapes=scratch,
        input_output_aliases=aliases,
        compiler_params=_cparams(1),
        name=f"ssd_{group}_{layer_name}",
    )(*args)


def _lru_kernel(*refs, length, has_h0, has_state_out, n_alias):
    it = iter(refs)
    xl_ref, gl_ref = next(it), next(it)
    cw_ref, cb_ref, wa_ref, wx_ref, ba_ref, bx_ref, lam_ref = (next(it) for _ in range(7))
    h0_ref = next(it) if has_h0 else None
    for _ in range(n_alias):
        next(it)
    o_ref = next(it)
    st_ref = next(it) if has_state_out else None
    pad_s, xc_s, a_s, u_s, h_s = (next(it) for _ in range(5))

    t = 128
    nc = length // t
    halo = SUBLANES
    n_tiles = LRU_WIDTH // MXU_DIM

    _fill_padded(pad_s, xl_ref[...], length, halo)

    def prep(c, carry):
        base = pl.multiple_of(c * t, t)
        for c0 in range(0, LRU_WIDTH, MXU_DIM):
            cols = slice(c0, c0 + MXU_DIM)
            xc_s[pl.ds(base, t), cols] = _short_conv_chunk(
                pad_s, cw_ref, cb_ref, base, t, halo, LRU_CONV, 2, cols)
        return carry

    lax.fori_loop(0, nc, prep, 0, unroll=2)

    row8 = lax.broadcasted_iota(jnp.int32, (SUBLANES, LRU_WIDTH), 0)
    n_groups = length // SUBLANES

    for d in range(2):
        log_a_unit = (-LRU_C) * _softplus(-lam_ref[d:d + 1, :])

        def gates(c, carry, d=d, log_a_unit=log_a_unit):
            base = pl.multiple_of(c * t, t)
            rows = pl.ds(base, t)
            xc = xc_s[rows, :]
            xb = xc.astype(BF16)
            ra, ri = [], []
            for j in range(n_tiles):
                js = slice(j * MXU_DIM, (j + 1) * MXU_DIM)
                ra.append(_dot(xb[:, js], wa_ref[d, j]))
                ri.append(_dot(xb[:, js], wx_ref[d, j]))
            r = _sigmoid_tanh(jnp.concatenate(ra, axis=1) + ba_ref[d:d + 1, :])
            gi = _sigmoid_tanh(jnp.concatenate(ri, axis=1) + bx_ref[d:d + 1, :])
            log_a = r * log_a_unit
            a = jnp.exp2(r * (log_a_unit * LOG2E))
            gap = -jnp.tanh(log_a) * (a * a + 1.0)
            root = jnp.where(gap > 0.0, gap * lax.rsqrt(gap), 0.0)
            a_s[rows, :] = a
            u_s[rows, :] = root * gi * xc
            return carry

        lax.fori_loop(0, nc, gates, 0, unroll=min(4, nc))

        if has_h0:
            carry0 = jnp.broadcast_to(h0_ref[d:d + 1, :], (SUBLANES, LRU_WIDTH))
        else:
            carry0 = jnp.zeros((SUBLANES, LRU_WIDTH), F32)

        def scan(gi_, carry, d=d):
            g = gi_ if d == 0 else n_groups - 1 - gi_
            base = pl.multiple_of(g * SUBLANES, SUBLANES)
            rows = pl.ds(base, SUBLANES)
            av = a_s[rows, :]
            uv = u_s[rows, :]
            for k in (1, 2, 4):
                if d == 0:
                    shift, valid = k, row8 >= k
                else:
                    shift, valid = SUBLANES - k, row8 < SUBLANES - k
                a_sh = pltpu.roll(av, shift, 0)
                u_sh = pltpu.roll(uv, shift, 0)
                uv = jnp.where(valid, av * u_sh + uv, uv)
                av = jnp.where(valid, av * a_sh, av)
            h = av * carry + uv
            if d == 0:
                h_s[rows, :] = h
                edge = h[SUBLANES - 1:SUBLANES, :]
            else:
                h_s[rows, :] += h
                edge = h[0:1, :]
            return jnp.broadcast_to(edge, (SUBLANES, LRU_WIDTH))

        final = lax.fori_loop(0, n_groups, scan, carry0, unroll=4)
        if has_state_out:
            st_ref[d:d + 1, :] = final[0:1, :]

    def finish(c, carry):
        base = pl.multiple_of(c * t, t)
        rows = pl.ds(base, t)
        o_ref[rows, :] = (_gelu_tanh(gl_ref[rows, :]) * h_s[rows, :]).astype(o_ref.dtype)
        return carry

    lax.fori_loop(0, nc, finish, 0, unroll=2)


def _lru(group, xl, gl, cw, cb, wa, wx, ba, bx, lam, i, h0_all, o_prev, st_prev, layer_name):
    prompt = group == "prompt"
    length = SEQ if prompt else DEC_SEQ
    nb = BATCH if prompt else DEC_BATCH
    off = 0 if prompt else N_PROMPT_TOK // DEC_SEQ
    in_specs = _seq_specs(group, (LRU_WIDTH, LRU_WIDTH))
    in_specs += [_const_spec(a.shape) for a in (cw, cb, wa, wx, ba, bx, lam)]
    args = [xl, gl, cw, cb, wa, wx, ba, bx, lam]
    o_spec = pl.BlockSpec((length, LRU_WIDTH), lambda b: (b + off, 0))
    o_shape = jax.ShapeDtypeStruct((N_TOK, LRU_WIDTH), BF16)
    st_block = (None, None, 2, LRU_WIDTH)
    aliases = {}
    if prompt:
        out_specs = [o_spec, pl.BlockSpec(st_block, lambda b: (b, i, 0, 0))]
        out_shape = [o_shape, jax.ShapeDtypeStruct((nb, N_SSM_LAYERS, 2, LRU_WIDTH), F32)]
        if st_prev is not None:
            in_specs.append(pl.BlockSpec(memory_space=pl.ANY))
            args.append(st_prev)
            aliases = {len(args) - 1: 1}
    else:
        in_specs += [pl.BlockSpec(st_block, lambda b: (b, i, 0, 0)),
                     pl.BlockSpec(memory_space=pl.ANY)]
        args += [h0_all, o_prev]
        out_specs = [o_spec]
        out_shape = [o_shape]
        aliases = {len(args) - 1: 0}
    halo = SUBLANES
    scratch = [pltpu.VMEM((length + 2 * halo, LRU_WIDTH), F32)]
    scratch += [pltpu.VMEM((length, LRU_WIDTH), F32) for _ in range(4)]
    return pl.pallas_call(
        functools.partial(_lru_kernel, length=length, has_h0=not prompt, has_state_out=prompt,
                          n_alias=len(aliases)),
        grid=(nb,),
        in_specs=in_specs,
        out_specs=out_specs,
        out_shape=out_shape,
        scratch_shapes=scratch,
        input_output_aliases=aliases,
        compiler_params=_cparams(1),
        name=f"lru_{group}_{layer_name}",
    )(*args)


def _confconv_kernel(*refs, length, aliased):
    it = iter(refs)
    a_ref, g_ref, w_ref, b_ref, lng_ref, lnb_ref = (next(it) for _ in range(6))
    if aliased:
        next(it)
    o_ref = next(it)
    pad_s, acc_s = next(it), next(it)

    t = 128
    nc = length // t
    halo = 2 * SUBLANES
    left = (CONF_K - 1) // 2
    cblk = MXU_DIM
    n = t + 2 * halo

    _fill_padded(pad_s, a_ref[...] * _sigmoid_tanh(g_ref[...]), length, halo)

    def step(c, carry):
        base = pl.multiple_of(c * t, t)
        rows = pl.ds(base, t)
        for cb0 in range(0, CONV_WIDTH, cblk):
            cs = slice(cb0, cb0 + cblk)
            win = pad_s[pl.ds(base, n), cs]
            acc = jnp.broadcast_to(b_ref[:, cs], (t, cblk))
            for s in range(SUBLANES):
                shifted = win if s == 0 else pltpu.roll(win, n - s, 0)
                for m in range(n // SUBLANES):
                    j = SUBLANES * m + s - halo + left
                    if 0 <= j < CONF_K and SUBLANES * m + t <= n:
                        acc = acc + shifted[SUBLANES * m:SUBLANES * m + t] * w_ref[j:j + 1, cs]
            acc_s[:, cs] = acc
        cv = acc_s[...]
        mu = jnp.mean(cv, axis=-1, keepdims=True)
        xc = cv - mu
        var = jnp.mean(xc * xc, axis=-1, keepdims=True)
        y = (xc * lax.rsqrt(var + EPS)) * lng_ref[...] + lnb_ref[...]
        o_ref[rows, :] = _silu(y).astype(o_ref.dtype)
        return carry

    lax.fori_loop(0, nc, step, 0, unroll=2)


def _confconv(group, ga, gg, w, b, lng, lnb, o_prev, layer_name):
    prompt = group == "prompt"
    length = SEQ if prompt else DEC_SEQ
    nb = BATCH if prompt else DEC_BATCH
    off = 0 if prompt else N_PROMPT_TOK // DEC_SEQ
    in_specs = _seq_specs(group, (CONV_WIDTH, CONV_WIDTH))
    in_specs += [_const_spec(a.shape) for a in (w, b, lng, lnb)]
    args = [ga, gg, w, b, lng, lnb]
    aliases = {}
    if not prompt:
        in_specs.append(pl.BlockSpec(memory_space=pl.ANY))
        args.append(o_prev)
        aliases = {len(args) - 1: 0}
    halo = 2 * SUBLANES
    return pl.pallas_call(
        functools.partial(_confconv_kernel, length=length, aliased=not prompt),
        grid=(nb,),
        in_specs=in_specs,
        out_specs=pl.BlockSpec((length, CONV_WIDTH), lambda b: (b + off, 0)),
        out_shape=jax.ShapeDtypeStruct((N_TOK, CONV_WIDTH), BF16),
        scratch_shapes=[pltpu.VMEM((length + 2 * halo, CONV_WIDTH), F32),
                        pltpu.VMEM((128, CONV_WIDTH), F32)],
        input_output_aliases=aliases,
        compiler_params=_cparams(1),
        name=f"confconv_{group}_{layer_name}",
    )(*args)


def _rope(x, cos_t, sin_t):
    lane = lax.broadcasted_iota(jnp.int32, x.shape, 1)
    quarter = ROPE_AXIS_DIM // 2
    partner = jnp.where((lane & quarter) == 0,
                        pltpu.roll(x, HEAD_DIM - quarter, 1), pltpu.roll(x, quarter, 1))
    return x * cos_t + partner * sin_t


def _attn_kernel(*refs, length, n_ctx, use_rope, has_cache_out, n_alias, qb):
    it = iter(refs)
    q_ref, k_ref, v_ref, qg_ref, kg_ref = (next(it) for _ in range(5))
    if n_ctx:
        ck_ref, cv_ref = next(it), next(it)
    if use_rope:
        cos_ref, sin_ref = next(it), next(it)
    for _ in range(n_alias):
        next(it)
    o_ref = next(it)
    kn_ref, vc_ref = (next(it), next(it)) if has_cache_out else (None, None)
    kall_s, vall_s, s_s = next(it), next(it), next(it)

    nq = length // qb
    rep = N_HEADS // N_KV_HEADS
    scale = HEAD_DIM ** -0.5

    for g in range(N_KV_HEADS):
        gs = slice(g * HEAD_DIM, (g + 1) * HEAD_DIM)
        kn = _rms(k_ref[:, gs], kg_ref[...])
        if kn_ref is not None:
            kn_ref[:, gs] = kn
            vc_ref[:, gs] = v_ref[:, gs]
        if use_rope:
            kn = _rope(kn, cos_ref[...], sin_ref[...])
        if n_ctx:
            kall_s[g, 0:n_ctx, :] = ck_ref[:, gs].astype(BF16)
            vall_s[g, 0:n_ctx, 0:HEAD_DIM] = cv_ref[:, gs].astype(BF16)
        kall_s[g, n_ctx:n_ctx + length, :] = kn.astype(BF16)
        vall_s[g, n_ctx:n_ctx + length, 0:HEAD_DIM] = v_ref[:, gs].astype(BF16)
        vall_s[g, :, HEAD_DIM:] = jnp.ones((n_ctx + length, HEAD_DIM), BF16)

    c_exp = scale * LOG2E

    def scores(i, slot):
        rows = pl.ds(pl.multiple_of(i * qb, qb), qb)
        for g in range(N_KV_HEADS):
            qs = []
            for r in range(rep):
                hs = slice((g * rep + r) * HEAD_DIM, (g * rep + r + 1) * HEAD_DIM)
                qn = _rms(q_ref[rows, hs], qg_ref[...])
                if use_rope:
                    qn = _rope(qn, cos_ref[rows, :], sin_ref[rows, :])
                qs.append(qn.astype(BF16))
            s_s[slot, g] = _dot_nt(jnp.concatenate(qs, axis=0), kall_s[g])

    def outputs(i, slot):
        rows = pl.ds(pl.multiple_of(i * qb, qb), qb)
        for g in range(N_KV_HEADS):
            s = s_s[slot, g]
            m = jnp.max(s, axis=-1, keepdims=True)
            p = jnp.exp2((s - m) * c_exp)
            ov = _dot(p.astype(BF16), vall_s[g])
            o = ov[:, :HEAD_DIM] / ov[:, HEAD_DIM:]
            for r in range(rep):
                hs = slice((g * rep + r) * HEAD_DIM, (g * rep + r + 1) * HEAD_DIM)
                o_ref[rows, hs] = o[r * qb:(r + 1) * qb].astype(o_ref.dtype)

    scores(0, 0)

    def pair(j, carry):
        i = 2 * j
        scores(i + 1, 1)
        outputs(i, 0)
        scores(jnp.minimum(i + 2, nq - 1), 0)
        outputs(i + 1, 1)
        return carry

    lax.fori_loop(0, nq // 2, pair, 0, unroll=min(2, nq // 2))


def _attn(group, q, k, v, qg, kg, i, ck_all, cv_all, cos_t, sin_t, o_prev, kc_prev, vc_prev,
          layer_name):
    prompt = group == "prompt"
    length = SEQ if prompt else DEC_SEQ
    nb = BATCH if prompt else DEC_BATCH
    off = 0 if prompt else N_PROMPT_TOK // DEC_SEQ
    kvw = N_KV_HEADS * HEAD_DIM
    n_ctx = 0 if prompt else PAST_LEN
    qb = 128
    rep = N_HEADS // N_KV_HEADS
    in_specs = _seq_specs(group, (D_MODEL, kvw, kvw))
    in_specs += [_const_spec(qg.shape), _const_spec(kg.shape)]
    args = [q, k, v, qg, kg]
    o_spec = pl.BlockSpec((length, D_MODEL), lambda b: (b + off, 0))
    o_shape = jax.ShapeDtypeStruct((N_TOK, D_MODEL), BF16)
    aliases = {}
    if prompt:
        c_spec = pl.BlockSpec((None, None, length, kvw), lambda b: (b, i, 0, 0))
        c_shape = jax.ShapeDtypeStruct((nb, N_ATTN_LAYERS, length, kvw), F32)
        out_specs = [o_spec, c_spec, c_spec]
        out_shape = [o_shape, c_shape, c_shape]
        if kc_prev is not None:
            in_specs += [pl.BlockSpec(memory_space=pl.ANY)] * 2
            args += [kc_prev, vc_prev]
            aliases = {len(args) - 2: 1, len(args) - 1: 2}
    else:
        ctx_spec = pl.BlockSpec((None, None, n_ctx, kvw), lambda b: (b, i, 0, 0))
        in_specs += [ctx_spec, ctx_spec, _const_spec(cos_t.shape), _const_spec(sin_t.shape),
                     pl.BlockSpec(memory_space=pl.ANY)]
        args += [ck_all, cv_all, cos_t, sin_t, o_prev]
        out_specs = [o_spec]
        out_shape = [o_shape]
        aliases = {len(args) - 1: 0}
    return pl.pallas_call(
        functools.partial(_attn_kernel, length=length, n_ctx=n_ctx, use_rope=not prompt,
                          has_cache_out=prompt, n_alias=len(aliases), qb=qb),
        grid=(nb,),
        in_specs=in_specs,
        out_specs=out_specs,
        out_shape=out_shape,
        scratch_shapes=[pltpu.VMEM((N_KV_HEADS, n_ctx + length, HEAD_DIM), BF16),
                        pltpu.VMEM((N_KV_HEADS, n_ctx + length, 2 * HEAD_DIM), BF16),
                        pltpu.VMEM((2, N_KV_HEADS, rep * qb, n_ctx + length), F32)],
        input_output_aliases=aliases,
        compiler_params=_cparams(1),
        name=f"attn_{group}_{layer_name}",
    )(*args)


def _rope_tables(rows):
    row_pos = jnp.repeat(jnp.arange(rows, dtype=F32), GRID_W)
    col_pos = jnp.tile(jnp.arange(GRID_W, dtype=F32), rows)
    inv_freq = jnp.power(ROPE_THETA, -jnp.arange(0, ROPE_AXIS_DIM, 2, dtype=F32) / ROPE_AXIS_DIM)
    ang_r = row_pos[:, None] * inv_freq
    ang_c = col_pos[:, None] * inv_freq
    cos_t = jnp.concatenate([jnp.cos(ang_r), jnp.cos(ang_r), jnp.cos(ang_c), jnp.cos(ang_c)], axis=1)
    sin_t = jnp.concatenate([-jnp.sin(ang_r), jnp.sin(ang_r), -jnp.sin(ang_c), jnp.sin(ang_c)], axis=1)
    return cos_t, sin_t


def _block_diag_tiles(w):
    per = MXU_DIM // LRU_BW
    n_tiles = LRU_BLOCKS // per
    w = w.reshape(2, n_tiles, per, LRU_BW, LRU_BW)
    eye = jnp.eye(per, dtype=w.dtype)
    tiles = jnp.einsum('dtpio,pq->dtpiqo', w, eye)
    return tiles.reshape(2, n_tiles, MXU_DIM, MXU_DIM).astype(BF16)


def _row(v):
    return v.reshape(1, -1)


def kernel(x_prompt, x_sample, state_ssd, state_lru, cache_k, cache_v, c, c_ctx,
           w_mod, b_mod, norm_g, w_in_ssm, ssd_conv_w, ssd_conv_b, ssd_a_log, ssd_dt_bias,
           ssd_d, ssd_norm_w, lru_conv_w, lru_conv_b, lru_wa, lru_ba, lru_wx, lru_bx,
           lru_lambda, w_out_ssm, w_in_ca, conf_dw_w, conf_dw_b, conf_ln_g, conf_ln_b,
           q_norm_g, k_norm_g, w_out_ca, ffn_w_in, ffn_conv_w, ffn_conv_b, ffn_w_out):
    x = (x_prompt.reshape(N_PROMPT_TOK, D_MODEL), x_sample.reshape(N_SAMPLE_TOK, D_MODEL))
    cvec = jnp.concatenate(
        [c_ctx[None], c, jnp.zeros((N_MOD_ROWS - 1 - DEC_BATCH, D_MODEL), F32)], axis=0)
    mods = _modulation_all(cvec, w_mod, b_mod)
    cos_t, sin_t = _rope_tables(DEC_SEQ // GRID_W)
    kvw = N_KV_HEADS * HEAD_DIM

    w_ssm_in = jnp.pad(w_in_ssm, ((0, 0), (0, 0), (0, DT_COL0))).astype(BF16)
    ssd_h0 = state_ssd.reshape(DEC_BATCH, N_SSM_LAYERS, 2, SSD_INNER, SSD_STATE)
    ck_all = cache_k.reshape(DEC_BATCH, N_ATTN_LAYERS, PAST_LEN, kvw)
    cv_all = cache_v.reshape(DEC_BATCH, N_ATTN_LAYERS, PAST_LEN, kvw)

    ssd_st = lru_st = kc = vc = None
    for layer in range(DEPTH):
        i = layer // 2
        m = mods[layer]
        name = f"l{layer}"
        g0, g1, g2, g3 = (_row(norm_g[layer, j]) for j in range(4))
        if layer % 2 == 0:
            z, xbc, xl, gl, dt, w_ffn_in, w_ffn_out = _inproj(
                x, g0, m[0], m[1], w_ssm_in, i,
                (SSD_INNER, SSD_CONV_DIM, LRU_WIDTH, LRU_WIDTH, DT_PAD), ffn_w_in, ffn_w_out, layer,
                f"inproj_ssm_{name}", rot=(SSM_O2, 2 * SSD_HEADS))
            dtb = jnp.pad(ssd_dt_bias[i].reshape(1, -1), ((0, 0), (DT_COL0, 0)))
            alog = jnp.pad(ssd_a_log[i].reshape(1, -1), ((0, 0), (DT_COL0, 0)))
            dskip = _row(jnp.repeat(ssd_d[i], SSD_HEADDIM))
            ssd_args = (ssd_conv_w[i], _row(ssd_conv_b[i]), dtb, alog, dskip, _row(ssd_norm_w[i]))
            y, ssd_st = _ssd("prompt", z, xbc, dt, *ssd_args, i, None, None, ssd_st, name)
            (y,) = _ssd("sample", z, xbc, dt, *ssd_args, i, ssd_h0, y, None, name)
            lru_args = (lru_conv_w[i], _row(lru_conv_b[i]), _block_diag_tiles(lru_wa[i]),
                        _block_diag_tiles(lru_wx[i]), lru_ba[i], lru_bx[i], lru_lambda[i])
            yl, lru_st = _lru("prompt", xl, gl, *lru_args, i, None, None, lru_st, name)
            (yl,) = _lru("sample", xl, gl, *lru_args, i, state_lru, yl, None, name)
            xo = _outproj(y, yl, w_out_ssm, i, x, g1, m[2], f"outproj_ssm_{name}")
        else:
            ga, gg, q, k, v, w_ffn_in, w_ffn_out = _inproj(
                x, g0, m[0], m[1], w_in_ca, i,
                (CONV_WIDTH, CONV_WIDTH, N_HEADS * HEAD_DIM, kvw, kvw), ffn_w_in, ffn_w_out, layer,
                f"inproj_ca_{name}")
            conv_args = (conf_dw_w[i], _row(conf_dw_b[i]), _row(conf_ln_g[i]), _row(conf_ln_b[i]))
            cvo = _confconv("prompt", ga, gg, *conv_args, None, name)
            cvo = _confconv("sample", ga, gg, *conv_args, cvo, name)
            qg, kg = _row(q_norm_g[i]), _row(k_norm_g[i])
            o, kc, vc = _attn("prompt", q, k, v, qg, kg, i, None, None, None, None, None, kc, vc,
                              name)
            (o,) = _attn("sample", q, k, v, qg, kg, i, ck_all, cv_all, cos_t, sin_t, o, None, None,
                         name)
            xo = _outproj(cvo, o, w_out_ca, i, x, g1, m[2], f"outproj_ca_{name}")
        x = (_ffn(xo, g2, m[3], m[4], w_ffn_in, ffn_conv_w[layer], _row(ffn_conv_b[layer]),
                  w_ffn_out, g3, m[5], f"ffn_{name}"),)

    xp = x[0][:N_PROMPT_TOK].reshape(BATCH, SEQ, D_MODEL)
    xs = x[0][N_PROMPT_TOK:].reshape(DEC_BATCH, DEC_SEQ, D_MODEL)
    return (xp, xs,
            ssd_st.reshape(BATCH, N_SSM_LAYERS, 2, SSD_HEADS, SSD_HEADDIM, SSD_STATE),
            lru_st,
            kc.reshape(BATCH, N_ATTN_LAYERS, SEQ, N_KV_HEADS, HEAD_DIM),
            vc.reshape(BATCH, N_ATTN_LAYERS, SEQ, N_KV_HEADS, HEAD_DIM))
```

```python
import functools
import math

import jax
import jax.numpy as jnp
from jax import lax
from jax.experimental import pallas as pl
from jax.experimental.pallas import tpu as pltpu

F32 = jnp.float32
BF16 = jnp.bfloat16

D_MODEL = 1024
BATCH = 16
SEQ = 256
DEPTH = 4
N_SSM_LAYERS = (DEPTH + 1) // 2
N_ATTN_LAYERS = DEPTH // 2
DEC_BATCH = 4
DEC_SEQ = 1024
PAST_LEN = 512
GRID_W = 64
EPS = 1e-6
SSD_HEADDIM = 64
SSD_INNER = D_MODEL
SSD_HEADS = SSD_INNER // SSD_HEADDIM
SSD_GROUPS = 2
SSD_STATE = 128
SSD_CONV = 4
SSD_CHUNK = 128
SSD_CONV_DIM = SSD_INNER + 2 * SSD_GROUPS * SSD_STATE
LRU_WIDTH = D_MODEL
LRU_BW = 64
LRU_BLOCKS = LRU_WIDTH // LRU_BW
LRU_CONV = 4
LRU_C = 8.0
CONV_WIDTH = D_MODEL
CONF_K = 31
HEAD_DIM = 128
N_HEADS = D_MODEL // HEAD_DIM
N_KV_HEADS = 2
ROPE_THETA = 10000.0
ROPE_AXIS_DIM = HEAD_DIM // 2
D_FF = 2816
FFN_CONV = 3

N_PROMPT_TOK = BATCH * SEQ
N_SAMPLE_TOK = DEC_BATCH * DEC_SEQ
N_TOK = N_PROMPT_TOK + N_SAMPLE_TOK
N_MOD_ROWS = 8
LANES = 128
SUBLANES = 8
MXU_DIM = 256
DT_PAD = LANES
SSM_O2 = SSD_INNER + SSD_CONV_DIM
DT_COL0 = DT_PAD - 2 * SSD_HEADS
LOG2E = 1.0 / math.log(2.0)
VMEM_LIMIT = 58 * 1024 * 1024

TM_LINEAR = 512
TM_FFN = 1024


def _cparams(n_axes):
    return pltpu.CompilerParams(
        dimension_semantics=("arbitrary",) * n_axes,
        vmem_limit_bytes=VMEM_LIMIT)


def _const_spec(shape):
    nd = len(shape)
    return pl.BlockSpec(shape, lambda *_: (0,) * nd, pipeline_mode=pl.Buffered(1))


def _layer_spec(shape, layer):
    nd = len(shape) - 1
    return pl.BlockSpec((None,) + tuple(shape[1:]), lambda *_: (layer,) + (0,) * nd,
                        pipeline_mode=pl.Buffered(1))


def _mod_row(i, tm):
    start = i * tm
    return jnp.where(start < N_PROMPT_TOK, 0, 1 + (start - N_PROMPT_TOK) // DEC_SEQ)


def _mod_spec(tm):
    return pl.BlockSpec((None, 1, D_MODEL), lambda i: (_mod_row(i, tm), 0, 0))


def _x_specs(xs, tm):
    if len(xs) == 1:
        return [pl.BlockSpec((tm, D_MODEL), lambda i: (i, 0))]
    n_p = N_PROMPT_TOK // tm
    return [pl.BlockSpec((tm, D_MODEL), lambda i: (jnp.minimum(i, n_p - 1), 0)),
            pl.BlockSpec((tm, D_MODEL), lambda i: (jnp.maximum(i - n_p, 0), 0))]


def _x_tile(x_refs, tm):
    if len(x_refs) == 1:
        return x_refs[0][...]
    is_prompt = pl.program_id(0) * tm < N_PROMPT_TOK
    return jnp.where(is_prompt, x_refs[0][...], x_refs[1][...])


def _sigmoid(x):
    return jax.nn.sigmoid(x)


def _sigmoid_tanh(x):
    return 0.5 * jnp.tanh(0.5 * x) + 0.5


def _silu(x):
    return x * _sigmoid(x)


def _softplus(x):
    return jnp.maximum(x, 0.0) + jnp.log1p(jnp.exp(-jnp.abs(x)))


def _gelu_tanh(x):
    return 0.5 * x * (1.0 + jnp.tanh(math.sqrt(2.0 / math.pi) * (x + 0.044715 * (x * x * x))))


def _rms(x, g):
    ms = jnp.mean(x * x, axis=-1, keepdims=True)
    return (x * lax.rsqrt(ms + EPS)) * g


def _dot(a, b):
    return jnp.dot(a, b, preferred_element_type=F32)


def _dot_nt(a, b):
    return lax.dot_general(a, b, (((1,), (1,)), ((), ())), preferred_element_type=F32)


def _mod_kernel(c_ref, w_ref, b_ref, o_ref):
    c = c_ref[...]
    s = _silu(c).astype(BF16)
    o_ref[...] = _dot(s, w_ref[...].astype(BF16)) + b_ref[...]


def _modulation_all(cvec, w_mod, b_mod):
    tn = 1536
    n_out = 6 * D_MODEL
    out = pl.pallas_call(
        _mod_kernel,
        grid=(DEPTH, n_out // tn),
        in_specs=[
            pl.BlockSpec((N_MOD_ROWS, D_MODEL), lambda l, j: (0, 0)),
            pl.BlockSpec((None, D_MODEL, tn), lambda l, j: (l, 0, j)),
            pl.BlockSpec((None, 1, tn), lambda l, j: (l, 0, j)),
        ],
        out_specs=pl.BlockSpec((None, N_MOD_ROWS, tn), lambda l, j: (l, 0, j)),
        out_shape=jax.ShapeDtypeStruct((DEPTH, N_MOD_ROWS, n_out), F32),
        compiler_params=_cparams(2),
        name="modulation",
    )(cvec, w_mod, b_mod.reshape(DEPTH, 1, n_out))
    out = out.reshape(DEPTH, N_MOD_ROWS, 6, 1, D_MODEL)
    return jnp.transpose(out, (0, 2, 1, 3, 4))


def _inproj_kernel(*refs, n_x, widths, chunk, rot):
    x_refs = refs[:n_x]
    g_ref, shift_ref, scale_ref, w_ref, fw1_ref, fw2_ref = refs[n_x:n_x + 6]
    rest = refs[n_x + 6:]
    o_refs = rest[:len(widths)]
    fb1_ref, fb2_ref, h_ref = rest[len(widths):len(widths) + 3]
    rest = rest[2:]
    fb1_ref[...] = fw1_ref[...].astype(BF16)
    fb2_ref[...] = fw2_ref[...].astype(BF16)
    if rot is not None:
        rot_start, rot_by = rot
        wr_ref = rest[len(widths) + 1]
        rot_w = w_ref.shape[1] - rot_start

        @pl.when(pl.program_id(0) == 0)
        def _():
            wr_ref[...] = pltpu.roll(w_ref[:, rot_start:], rot_w - rot_by, 1)
    elif w_ref.dtype != BF16:
        wb_ref = rest[len(widths) + 1]

        @pl.when(pl.program_id(0) == 0)
        def _():
            wb_ref[...] = w_ref[...].astype(BF16)
        w_ref = wb_ref

    x = _x_tile(x_refs, x_refs[0].shape[0])
    h = _rms(x, g_ref[...]) * (1.0 + scale_ref[...]) + shift_ref[...]
    h_ref[...] = h.astype(BF16)
    off = 0
    for o_ref, n in zip(o_refs, widths):
        for c0 in range(0, n, chunk):
            c1 = min(c0 + chunk, n)
            if rot is not None and off >= rot_start:
                w_blk = wr_ref[:, off - rot_start + c0:off - rot_start + c1]
            else:
                w_blk = w_ref[:, off + c0:off + c1]
            o_ref[:, c0:c1] = _dot(h_ref[...], w_blk)
        off += n


def _inproj(xs, g, shift, scale, w, layer, widths, fw1, fw2, flayer, name, rot=None):
    tm = TM_LINEAR
    assert sum(widths) == w.shape[2] and all(n % LANES == 0 for n in widths)
    steps = N_TOK // tm
    r1, r2 = fw1.shape[1] // steps, fw2.shape[1] // steps
    assert r1 * steps == fw1.shape[1] and r2 * steps == fw2.shape[1] and r1 % 16 == 0 and r2 % 16 == 0
    scratch = [pltpu.VMEM((tm, D_MODEL), BF16)]
    if rot is not None:
        scratch.append(pltpu.VMEM((D_MODEL, w.shape[2] - rot[0]), BF16))
    elif w.dtype != BF16:
        scratch.append(pltpu.VMEM(w.shape[1:], BF16))
    return pl.pallas_call(
        functools.partial(_inproj_kernel, n_x=len(xs), widths=tuple(widths), chunk=512, rot=rot),
        grid=(N_TOK // tm,),
        in_specs=_x_specs(xs, tm) + [
            _const_spec((1, D_MODEL)), _mod_spec(tm), _mod_spec(tm), _layer_spec(w.shape, layer),
            pl.BlockSpec((None, r1, fw1.shape[2]), lambda i: (flayer, i, 0)),
            pl.BlockSpec((None, r2, fw2.shape[2]), lambda i: (flayer, i, 0))],
        out_specs=[pl.BlockSpec((tm, n), lambda i: (i, 0)) for n in widths] + [
            pl.BlockSpec((r1, fw1.shape[2]), lambda i: (i, 0)),
            pl.BlockSpec((r2, fw2.shape[2]), lambda i: (i, 0))],
        out_shape=[jax.ShapeDtypeStruct((N_TOK, n), F32) for n in widths] + [
            jax.ShapeDtypeStruct(fw1.shape[1:], BF16), jax.ShapeDtypeStruct(fw2.shape[1:], BF16)],
        scratch_shapes=scratch,
        compiler_params=_cparams(1),
        name=name,
    )(*xs, g, shift, scale, w, fw1, fw2)


def _outproj_kernel(a_ref, b_ref, w_ref, *rest, n_x):
    x_refs = rest[:n_x]
    g_ref, gate_ref, o_ref, wb_ref = rest[n_x:]

    @pl.when(pl.program_id(0) == 0)
    def _():
        wb_ref[...] = w_ref[...].astype(BF16)

    ka = a_ref.shape[1]
    acc = _dot(a_ref[...], wb_ref[0:ka, :]) + _dot(b_ref[...], wb_ref[ka:, :])
    o_ref[...] = _x_tile(x_refs, o_ref.shape[0]) + gate_ref[...] * _rms(acc, g_ref[...])


def _outproj(a, b, w, layer, xs, g, gate, name):
    tm = TM_LINEAR
    return pl.pallas_call(
        functools.partial(_outproj_kernel, n_x=len(xs)),
        grid=(N_TOK // tm,),
        in_specs=[
            pl.BlockSpec((tm, a.shape[1]), lambda i: (i, 0)),
            pl.BlockSpec((tm, b.shape[1]), lambda i: (i, 0)),
            _layer_spec(w.shape, layer),
        ] + _x_specs(xs, tm) + [_const_spec((1, D_MODEL)), _mod_spec(tm)],
        out_specs=pl.BlockSpec((tm, D_MODEL), lambda i: (i, 0)),
        out_shape=jax.ShapeDtypeStruct((N_TOK, D_MODEL), F32),
        scratch_shapes=[pltpu.VMEM(w.shape[1:], BF16)],
        compiler_params=_cparams(1),
        name=name,
    )(a, b, w, *xs, g, gate)


def _ffn_kernel(x_ref, g2_ref, shift_ref, scale_ref, wi_ref, cw_ref, cb_ref,
                wo_ref, g3_ref, gate_ref, o_ref, h_ref, act_ref, *, chunk, row_blk):
    tm = x_ref.shape[0]
    i = pl.program_id(0)
    h = _rms(x_ref[...], g2_ref[...]) * (1.0 + scale_ref[...]) + shift_ref[...]
    h_ref[...] = h.astype(BF16)
    lseq = jnp.where(i * tm < N_PROMPT_TOK, SEQ, DEC_SEQ)
    pos = lax.broadcasted_iota(jnp.int32, (tm, 1), 0) & (lseq - 1)
    first = pos == 0
    last = pos == lseq - 1
    for c0 in range(0, D_FF, chunk):
        c1 = min(c0 + chunk, D_FF)
        gt = _dot(h_ref[...], wi_ref[:, c0:c1])
        vl = _dot(h_ref[...], wi_ref[:, D_FF + c0:D_FF + c1])
        g_prev = jnp.where(first, 0.0, pltpu.roll(gt, 1, 0))
        g_next = jnp.where(last, 0.0, pltpu.roll(gt, tm - 1, 0))
        conv = (g_prev * cw_ref[0:1, c0:c1] + gt * cw_ref[1:2, c0:c1]
                + g_next * cw_ref[2:3, c0:c1] + cb_ref[:, c0:c1])
        act_ref[:, c0:c1] = (_silu(conv) * vl).astype(BF16)
    for r0 in range(0, tm, row_blk):
        rs = slice(r0, r0 + row_blk)
        acc = _dot(act_ref[rs, :], wo_ref[...])
        o_ref[rs, :] = x_ref[rs, :] + gate_ref[...] * _rms(acc, g3_ref[...])


def _ffn(x, g2, shift, scale, wi, cw, cb, wo, g3, gate, name):
    tm = TM_FFN
    row_spec = pl.BlockSpec((tm, D_MODEL), lambda i: (i, 0))
    return pl.pallas_call(
        functools.partial(_ffn_kernel, chunk=2 * MXU_DIM, row_blk=MXU_DIM),
        grid=(N_TOK // tm,),
        in_specs=[
            row_spec, _const_spec((1, D_MODEL)), _mod_spec(tm), _mod_spec(tm),
            _const_spec(wi.shape), _const_spec(cw.shape),
            _const_spec(cb.shape), _const_spec(wo.shape), _const_spec((1, D_MODEL)),
            _mod_spec(tm),
        ],
        out_specs=row_spec,
        out_shape=jax.ShapeDtypeStruct((N_TOK, D_MODEL), F32),
        scratch_shapes=[pltpu.VMEM((tm, D_MODEL), BF16), pltpu.VMEM((tm, D_FF), BF16)],
        compiler_params=_cparams(1),
        name=name,
    )(x, g2, shift, scale, wi, cw, cb, wo, g3, gate)


def _seq_specs(group, widths):
    if group == "prompt":
        return [pl.BlockSpec((SEQ, w), lambda b: (b, 0)) for w in widths]
    off = N_PROMPT_TOK // DEC_SEQ
    return [pl.BlockSpec((DEC_SEQ, w), lambda b: (b + off, 0)) for w in widths]


def _short_conv_chunk(pad_ref, cw_ref, cb_ref, base, t, halo, taps, left, cols):
    win = pad_ref[pl.ds(base, t + 2 * halo), cols]
    n = t + 2 * halo
    acc = cb_ref[:, cols]
    for j in range(taps):
        s = (left - j) % n
        rolled = win if s == 0 else pltpu.roll(win, s, 0)
        acc = acc + rolled[halo:halo + t] * cw_ref[j:j + 1, cols]
    return acc


def _fill_padded(pad_ref, src, length, halo):
    width = pad_ref.shape[1]
    pad_ref[0:halo, :] = jnp.zeros((halo, width), F32)
    pad_ref[halo + length:2 * halo + length, :] = jnp.zeros((halo, width), F32)
    pad_ref[halo:halo + length, :] = src


def _lane_pairs(m, first_col, n_pairs, rows):
    lane = lax.broadcasted_iota(jnp.int32, (rows, LANES), 1)
    lo_half = lane < SSD_HEADDIM
    pieces = []
    for k in range(n_pairs):
        c = first_col + 2 * k
        lo = jnp.broadcast_to(m[:, c:c + 1], (rows, LANES))
        hi = jnp.broadcast_to(m[:, c + 1:c + 2], (rows, LANES))
        pieces.append(jnp.where(lo_half, lo, hi))
    return jnp.concatenate(pieces, axis=1)


def _ssd_kernel(*refs, length, has_h0, has_state_out, n_alias):
    it = iter(refs)
    z_ref, xbc_ref, dt_ref = next(it), next(it), next(it)
    cw_ref, cb_ref, dtb_ref, alog_ref, dskip_ref, nw_ref = (next(it) for _ in range(6))
    h0_ref = next(it) if has_h0 else None
    for _ in range(n_alias):
        next(it)
    y_ref = next(it)
    st_ref = next(it) if has_state_out else None
    pad_s, xs_s, bc_s, cum_s, row_t_s, w_t_s, cd_s, yacc_s, state_s = (
        next(it) for _ in range(9))

    t = SSD_CHUNK
    nc = length // t
    halo = SUBLANES
    gw = SSD_INNER // SSD_GROUPS
    pairs_per_group = SSD_HEADS // SSD_GROUPS // 2

    _fill_padded(pad_s, xbc_ref[...], length, halo)
    a_row = -jnp.exp(alog_ref[...])

    ri = lax.broadcasted_iota(jnp.int32, (t, t), 0)
    ci = lax.broadcasted_iota(jnp.int32, (t, t), 1)
    keep = (ci <= ri, ci >= ri)
    tril = keep[0].astype(F32)
    lane = lax.broadcasted_iota(jnp.int32, (t, LANES), 1)
    lo_half = lane < SSD_HEADDIM
    fwd_cols = lane < DT_COL0 + SSD_HEADS
    fwd_rows = ri < DT_COL0 + SSD_HEADS

    def prep(c, carry):
        base = pl.multiple_of(c * t, t)
        rows = pl.ds(base, t)
        for c0 in range(0, SSD_CONV_DIM, MXU_DIM):
            cols = slice(c0, c0 + MXU_DIM)
            conv = _silu(_short_conv_chunk(pad_s, cw_ref, cb_ref, base, t, halo, SSD_CONV, 2, cols))
            if c0 < SSD_INNER:
                xs_s[rows, cols] = conv
                yacc_s[rows, cols] = conv * dskip_ref[:, cols]
            else:
                bc_s[rows, c0 - SSD_INNER:c0 - SSD_INNER + MXU_DIM] = conv
        dtsp = _softplus(dt_ref[rows, :] + dtb_ref[...])
        a_c = dtsp * a_row
        pre = jnp.dot(tril, a_c, preferred_element_type=F32, precision=lax.Precision.HIGHEST)
        suf = pre[t - 1:t, :] - pre + a_c
        cum = jnp.where(fwd_cols, pre, suf) * LOG2E
        cum_s[rows, :] = cum
        cum_t = cum.T
        dt_t = dtsp.T
        edge_col = jnp.where(fwd_rows[:, 0:1], cum_t[:, t - 1:t], cum_t[:, 0:1])
        row_t_s[c] = cum_t - jnp.log(dt_t) * LOG2E
        w_t_s[c] = dt_t * jnp.exp2(edge_col - cum_t)
        edge_row = jnp.where(fwd_cols[0:1, :], cum[t - 1:t, :], cum[0:1, :])
        cd_s[c] = jnp.broadcast_to(jnp.exp2(edge_row), (SUBLANES, LANES))
        return carry

    lax.fori_loop(0, nc, prep, 0, unroll=nc)

    for d in range(2):
        if has_h0:
            for k in range(SSD_INNER // LANES):
                ks = slice(k * LANES, (k + 1) * LANES)
                state_s[d, :, ks] = h0_ref[d, ks, :].T
        else:
            state_s[d] = jnp.zeros((SSD_STATE, SSD_INNER), F32)

    def block_diag(m):
        return jnp.concatenate([jnp.where(lo_half, m, 0.0).astype(BF16),
                                jnp.where(lo_half, 0.0, m).astype(BF16)], axis=0)

    def chunk_step(c, carry):
        for d in range(2):
            cidx = c if d == 0 else nc - 1 - c
            base = pl.multiple_of(cidx * t, t)
            rows = pl.ds(base, t)
            cum = cum_s[rows, :]
            bc = bc_s[rows, :]
            cd = cd_s[cidx][0:1, :]
            for g in range(SSD_GROUPS):
                col0 = DT_COL0 + d * SSD_HEADS + g * 2 * pairs_per_group
                b_g = bc[:, g * SSD_STATE:(g + 1) * SSD_STATE]
                c_g = bc[:, (SSD_GROUPS + g) * SSD_STATE:(SSD_GROUPS + g + 1) * SSD_STATE]
                gmat = _dot_nt(c_g.astype(BF16), b_g.astype(BF16))
                b_t = b_g.T
                cd_rep = _lane_pairs(cd, col0, pairs_per_group, 1)
                for kk in range(pairs_per_group):
                    lanes = slice(g * gw + kk * LANES, g * gw + (kk + 1) * LANES)
                    rhs_x = block_diag(xs_s[rows, lanes])
                    st = state_s[d, :, lanes]
                    gl, ce, bw = [], [], []
                    for hcur in (col0 + 2 * kk, col0 + 2 * kk + 1):
                        hrow = pl.ds(hcur, 1)
                        colb = jnp.broadcast_to(cum[:, hcur:hcur + 1], (t, t))
                        rowb = jnp.broadcast_to(row_t_s[cidx, hrow, :], (t, t))
                        wrow = jnp.broadcast_to(w_t_s[cidx, hrow, :], (t, t))
                        lmat = jnp.exp2(jnp.where(keep[d], colb - rowb, -jnp.inf))
                        gl.append((gmat * lmat).astype(BF16))
                        ce.append((c_g * jnp.exp2(colb)).astype(BF16))
                        bw.append((b_t * wrow).astype(BF16))
                    y_p = _dot(jnp.concatenate(gl + ce, axis=1),
                               jnp.concatenate([rhs_x, block_diag(st)], axis=0))
                    state_s[d, :, lanes] = (st * cd_rep[:, kk * LANES:(kk + 1) * LANES]
                                            + _dot(jnp.concatenate(bw, axis=1), rhs_x))
                    yacc_s[rows, lanes] += y_p
        return carry

    lax.fori_loop(0, nc, chunk_step, 0, unroll=nc)

    def finish(c, carry):
        base = pl.multiple_of(c * t, t)
        rows = pl.ds(base, t)
        y = yacc_s[rows, :] * _silu(z_ref[rows, :])
        y_ref[rows, :] = _rms(y, nw_ref[...]).astype(y_ref.dtype)
        return carry

    lax.fori_loop(0, nc, finish, 0, unroll=nc)

    if has_state_out:
        for d in range(2):
            for k in range(SSD_INNER // LANES):
                ks = slice(k * LANES, (k + 1) * LANES)
                st_ref[d, ks, :] = state_s[d, :, ks].T


def _ssd(group, z, xbc, dt, cw, cb, dtb, alog, dskip, nw, i, h0_all, y_prev, st_prev, layer_name):
    prompt = group == "prompt"
    length = SEQ if prompt else DEC_SEQ
    nb = BATCH if prompt else DEC_BATCH
    off = 0 if prompt else N_PROMPT_TOK // DEC_SEQ
    in_specs = _seq_specs(group, (SSD_INNER, SSD_CONV_DIM, DT_PAD))
    in_specs += [_const_spec(a.shape) for a in (cw, cb, dtb, alog, dskip, nw)]
    args = [z, xbc, dt, cw, cb, dtb, alog, dskip, nw]
    y_spec = pl.BlockSpec((length, SSD_INNER), lambda b: (b + off, 0))
    y_shape = jax.ShapeDtypeStruct((N_TOK, SSD_INNER), BF16)
    st_block = (None, None, 2, SSD_INNER, SSD_STATE)
    aliases = {}
    if prompt:
        out_specs = [y_spec, pl.BlockSpec(st_block, lambda b: (b, i, 0, 0, 0))]
        out_shape = [y_shape, jax.ShapeDtypeStruct((nb, N_SSM_LAYERS, 2, SSD_INNER, SSD_STATE), F32)]
        if st_prev is not None:
            in_specs.append(pl.BlockSpec(memory_space=pl.ANY))
            args.append(st_prev)
            aliases = {len(args) - 1: 1}
    else:
        in_specs += [pl.BlockSpec(st_block, lambda b: (b, i, 0, 0, 0)),
                     pl.BlockSpec(memory_space=pl.ANY)]
        args += [h0_all, y_prev]
        out_specs = [y_spec]
        out_shape = [y_shape]
        aliases = {len(args) - 1: 0}
    halo = SUBLANES
    nc = length // SSD_CHUNK
    scratch = [
        pltpu.VMEM((length + 2 * halo, SSD_CONV_DIM), F32),
        pltpu.VMEM((length, SSD_INNER), F32),
        pltpu.VMEM((length, 2 * SSD_GROUPS * SSD_STATE), F32),
        pltpu.VMEM((length, DT_PAD), F32),
        pltpu.VMEM((nc, DT_PAD, SSD_CHUNK), F32),
        pltpu.VMEM((nc, DT_PAD, SSD_CHUNK), F32),
        pltpu.VMEM((nc, SUBLANES, DT_PAD), F32),
        pltpu.VMEM((length, SSD_INNER), F32),
        pltpu.VMEM((2, SSD_STATE, SSD_INNER), F32),
    ]
    return pl.pallas_call(
        functools.partial(_ssd_kernel, length=length, has_h0=not prompt, has_state_out=prompt,
                          n_alias=len(aliases)),
        grid=(nb,),
        in_specs=in_specs,
        out_specs=out_specs,
        out_shape=out_shape,
        scratch_shapes=scratch,
        input_output_aliases=aliases,
        compiler_params=_cparams(1),
        name=f"ssd_{group}_{layer_name}",
    )(*args)


def _lru_kernel(*refs, length, has_h0, has_state_out, n_alias):
    it = iter(refs)
    xl_ref, gl_ref = next(it), next(it)
    cw_ref, cb_ref, wa_ref, wx_ref, ba_ref, bx_ref, lam_ref = (next(it) for _ in range(7))
    h0_ref = next(it) if has_h0 else None
    for _ in range(n_alias):
        next(it)
    o_ref = next(it)
    st_ref = next(it) if has_state_out else None
    pad_s, xc_s, a_s, u_s, h_s = (next(it) for _ in range(5))

    t = 128
    nc = length // t
    halo = SUBLANES
    n_tiles = LRU_WIDTH // MXU_DIM

    _fill_padded(pad_s, xl_ref[...], length, halo)

    def prep(c, carry):
        base = pl.multiple_of(c * t, t)
        for c0 in range(0, LRU_WIDTH, MXU_DIM):
            cols = slice(c0, c0 + MXU_DIM)
            xc_s[pl.ds(base, t), cols] = _short_conv_chunk(
                pad_s, cw_ref, cb_ref, base, t, halo, LRU_CONV, 2, cols)
        return carry

    lax.fori_loop(0, nc, prep, 0, unroll=2)

    row8 = lax.broadcasted_iota(jnp.int32, (SUBLANES, LRU_WIDTH), 0)
    n_groups = length // SUBLANES

    for d in range(2):
        log_a_unit = (-LRU_C) * _softplus(-lam_ref[d:d + 1, :])

        def gates(c, carry, d=d, log_a_unit=log_a_unit):
            base = pl.multiple_of(c * t, t)
            rows = pl.ds(base, t)
            xc = xc_s[rows, :]
            xb = xc.astype(BF16)
            ra, ri = [], []
            for j in range(n_tiles):
                js = slice(j * MXU_DIM, (j + 1) * MXU_DIM)
                ra.append(_dot(xb[:, js], wa_ref[d, j]))
                ri.append(_dot(xb[:, js], wx_ref[d, j]))
            r = _sigmoid_tanh(jnp.concatenate(ra, axis=1) + ba_ref[d:d + 1, :])
            gi = _sigmoid_tanh(jnp.concatenate(ri, axis=1) + bx_ref[d:d + 1, :])
            log_a = r * log_a_unit
            a = jnp.exp2(r * (log_a_unit * LOG2E))
            gap = -jnp.tanh(log_a) * (a * a + 1.0)
            root = jnp.where(gap > 0.0, gap * lax.rsqrt(gap), 0.0)
            a_s[rows, :] = a
            u_s[rows, :] = root * gi * xc
            return carry

        lax.fori_loop(0, nc, gates, 0, unroll=min(4, nc))

        if has_h0:
            carry0 = jnp.broadcast_to(h0_ref[d:d + 1, :], (SUBLANES, LRU_WIDTH))
        else:
            carry0 = jnp.zeros((SUBLANES, LRU_WIDTH), F32)

        def scan(gi_, carry, d=d):
            g = gi_ if d == 0 else n_groups - 1 - gi_
            base = pl.multiple_of(g * SUBLANES, SUBLANES)
            rows = pl.ds(base, SUBLANES)
            av = a_s[rows, :]
            uv = u_s[rows, :]
            for k in (1, 2, 4):
                if d == 0:
                    shift, valid = k, row8 >= k
                else:
                    shift, valid = SUBLANES - k, row8 < SUBLANES - k
                a_sh = pltpu.roll(av, shift, 0)
                u_sh = pltpu.roll(uv, shift, 0)
                uv = jnp.where(valid, av * u_sh + uv, uv)
                av = jnp.where(valid, av * a_sh, av)
            h = av * carry + uv
            if d == 0:
                h_s[rows, :] = h
                edge = h[SUBLANES - 1:SUBLANES, :]
            else:
                h_s[rows, :] += h
                edge = h[0:1, :]
            return jnp.broadcast_to(edge, (SUBLANES, LRU_WIDTH))

        final = lax.fori_loop(0, n_groups, scan, carry0, unroll=4)
        if has_state_out:
            st_ref[d:d + 1, :] = final[0:1, :]

    def finish(c, carry):
        base = pl.multiple_of(c * t, t)
        rows = pl.ds(base, t)
        o_ref[rows, :] = (_gelu_tanh(gl_ref[rows, :]) * h_s[rows, :]).astype(o_ref.dtype)
        return carry

    lax.fori_loop(0, nc, finish, 0, unroll=2)


def _lru(group, xl, gl, cw, cb, wa, wx, ba, bx, lam, i, h0_all, o_prev, st_prev, layer_name):
    prompt = group == "prompt"
    length = SEQ if prompt else DEC_SEQ
    nb = BATCH if prompt else DEC_BATCH
    off = 0 if prompt else N_PROMPT_TOK // DEC_SEQ
    in_specs = _seq_specs(group, (LRU_WIDTH, LRU_WIDTH))
    in_specs += [_const_spec(a.shape) for a in (cw, cb, wa, wx, ba, bx, lam)]
    args = [xl, gl, cw, cb, wa, wx, ba, bx, lam]
    o_spec = pl.BlockSpec((length, LRU_WIDTH), lambda b: (b + off, 0))
    o_shape = jax.ShapeDtypeStruct((N_TOK, LRU_WIDTH), BF16)
    st_block = (None, None, 2, LRU_WIDTH)
    aliases = {}
    if prompt:
        out_specs = [o_spec, pl.BlockSpec(st_block, lambda b: (b, i, 0, 0))]
        out_shape = [o_shape, jax.ShapeDtypeStruct((nb, N_SSM_LAYERS, 2, LRU_WIDTH), F32)]
        if st_prev is not None:
            in_specs.append(pl.BlockSpec(memory_space=pl.ANY))
            args.append(st_prev)
            aliases = {len(args) - 1: 1}
    else:
        in_specs += [pl.BlockSpec(st_block, lambda b: (b, i, 0, 0)),
                     pl.BlockSpec(memory_space=pl.ANY)]
        args += [h0_all, o_prev]
        out_specs = [o_spec]
        out_shape = [o_shape]
        aliases = {len(args) - 1: 0}
    halo = SUBLANES
    scratch = [pltpu.VMEM((length + 2 * halo, LRU_WIDTH), F32)]
    scratch += [pltpu.VMEM((length, LRU_WIDTH), F32) for _ in range(4)]
    return pl.pallas_call(
        functools.partial(_lru_kernel, length=length, has_h0=not prompt, has_state_out=prompt,
                          n_alias=len(aliases)),
        grid=(nb,),
        in_specs=in_specs,
        out_specs=out_specs,
        out_shape=out_shape,
        scratch_shapes=scratch,
        input_output_aliases=aliases,
        compiler_params=_cparams(1),
        name=f"lru_{group}_{layer_name}",
    )(*args)


def _confconv_kernel(*refs, length, aliased):
    it = iter(refs)
    a_ref, g_ref, w_ref, b_ref, lng_ref, lnb_ref = (next(it) for _ in range(6))
    if aliased:
        next(it)
    o_ref = next(it)
    pad_s, acc_s = next(it), next(it)

    t = 128
    nc = length // t
    halo = 2 * SUBLANES
    left = (CONF_K - 1) // 2
    cblk = MXU_DIM
    n = t + 2 * halo

    _fill_padded(pad_s, a_ref[...] * _sigmoid_tanh(g_ref[...]), length, halo)

    def step(c, carry):
        base = pl.multiple_of(c * t, t)
        rows = pl.ds(base, t)
        for cb0 in range(0, CONV_WIDTH, cblk):
            cs = slice(cb0, cb0 + cblk)
            win = pad_s[pl.ds(base, n), cs]
            acc = jnp.broadcast_to(b_ref[:, cs], (t, cblk))
            for s in range(SUBLANES):
                shifted = win if s == 0 else pltpu.roll(win, n - s, 0)
                for m in range(n // SUBLANES):
                    j = SUBLANES * m + s - halo + left
                    if 0 <= j < CONF_K and SUBLANES * m + t <= n:
                        acc = acc + shifted[SUBLANES * m:SUBLANES * m + t] * w_ref[j:j + 1, cs]
            acc_s[:, cs] = acc
        cv = acc_s[...]
        mu = jnp.mean(cv, axis=-1, keepdims=True)
        xc = cv - mu
        var = jnp.mean(xc * xc, axis=-1, keepdims=True)
        y = (xc * lax.rsqrt(var + EPS)) * lng_ref[...] + lnb_ref[...]
        o_ref[rows, :] = _silu(y).astype(o_ref.dtype)
        return carry

    lax.fori_loop(0, nc, step, 0, unroll=2)


def _confconv(group, ga, gg, w, b, lng, lnb, o_prev, layer_name):
    prompt = group == "prompt"
    length = SEQ if prompt else DEC_SEQ
    nb = BATCH if prompt else DEC_BATCH
    off = 0 if prompt else N_PROMPT_TOK // DEC_SEQ
    in_specs = _seq_specs(group, (CONV_WIDTH, CONV_WIDTH))
    in_specs += [_const_spec(a.shape) for a in (w, b, lng, lnb)]
    args = [ga, gg, w, b, lng, lnb]
    aliases = {}
    if not prompt:
        in_specs.append(pl.BlockSpec(memory_space=pl.ANY))
        args.append(o_prev)
        aliases = {len(args) - 1: 0}
    halo = 2 * SUBLANES
    return pl.pallas_call(
        functools.partial(_confconv_kernel, length=length, aliased=not prompt),
        grid=(nb,),
        in_specs=in_specs,
        out_specs=pl.BlockSpec((length, CONV_WIDTH), lambda b: (b + off, 0)),
        out_shape=jax.ShapeDtypeStruct((N_TOK, CONV_WIDTH), BF16),
        scratch_shapes=[pltpu.VMEM((length + 2 * halo, CONV_WIDTH), F32),
                        pltpu.VMEM((128, CONV_WIDTH), F32)],
        input_output_aliases=aliases,
        compiler_params=_cparams(1),
        name=f"confconv_{group}_{layer_name}",
    )(*args)


def _rope(x, cos_t, sin_t):
    lane = lax.broadcasted_iota(jnp.int32, x.shape, 1)
    quarter = ROPE_AXIS_DIM // 2
    partner = jnp.where((lane & quarter) == 0,
                        pltpu.roll(x, HEAD_DIM - quarter, 1), pltpu.roll(x, quarter, 1))
    return x * cos_t + partner * sin_t


def _attn_kernel(*refs, length, n_ctx, use_rope, has_cache_out, n_alias, qb):
    it = iter(refs)
    q_ref, k_ref, v_ref, qg_ref, kg_ref = (next(it) for _ in range(5))
    if n_ctx:
        ck_ref, cv_ref = next(it), next(it)
    if use_rope:
        cos_ref, sin_ref = next(it), next(it)
    for _ in range(n_alias):
        next(it)
    o_ref = next(it)
    kn_ref, vc_ref = (next(it), next(it)) if has_cache_out else (None, None)
    kall_s, vall_s, s_s = next(it), next(it), next(it)

    nq = length // qb
    rep = N_HEADS // N_KV_HEADS
    scale = HEAD_DIM ** -0.5

    for g in range(N_KV_HEADS):
        gs = slice(g * HEAD_DIM, (g + 1) * HEAD_DIM)
        kn = _rms(k_ref[:, gs], kg_ref[...])
        if kn_ref is not None:
            kn_ref[:, gs] = kn
            vc_ref[:, gs] = v_ref[:, gs]
        if use_rope:
            kn = _rope(kn, cos_ref[...], sin_ref[...])
        if n_ctx:
            kall_s[g, 0:n_ctx, :] = ck_ref[:, gs].astype(BF16)
            vall_s[g, 0:n_ctx, 0:HEAD_DIM] = cv_ref[:, gs].astype(BF16)
        kall_s[g, n_ctx:n_ctx + length, :] = kn.astype(BF16)
        vall_s[g, n_ctx:n_ctx + length, 0:HEAD_DIM] = v_ref[:, gs].astype(BF16)
        vall_s[g, :, HEAD_DIM:] = jnp.ones((n_ctx + length, HEAD_DIM), BF16)

    c_exp = scale * LOG2E

    def scores(i, slot):
        rows = pl.ds(pl.multiple_of(i * qb, qb), qb)
        for g in range(N_KV_HEADS):
            qs = []
            for r in range(rep):
                hs = slice((g * rep + r) * HEAD_DIM, (g * rep + r + 1) * HEAD_DIM)
                qn = _rms(q_ref[rows, hs], qg_ref[...])
                if use_rope:
                    qn = _rope(qn, cos_ref[rows, :], sin_ref[rows, :])
                qs.append(qn.astype(BF16))
            s_s[slot, g] = _dot_nt(jnp.concatenate(qs, axis=0), kall_s[g])

    def outputs(i, slot):
        rows = pl.ds(pl.multiple_of(i * qb, qb), qb)
        for g in range(N_KV_HEADS):
            s = s_s[slot, g]
            m = jnp.max(s, axis=-1, keepdims=True)
            p = jnp.exp2((s - m) * c_exp)
            ov = _dot(p.astype(BF16), vall_s[g])
            o = ov[:, :HEAD_DIM] / ov[:, HEAD_DIM:]
            for r in range(rep):
                hs = slice((g * rep + r) * HEAD_DIM, (g * rep + r + 1) * HEAD_DIM)
                o_ref[rows, hs] = o[r * qb:(r + 1) * qb].astype(o_ref.dtype)

    scores(0, 0)

    def pair(j, carry):
        i = 2 * j
        scores(i + 1, 1)
        outputs(i, 0)
        scores(jnp.minimum(i + 2, nq - 1), 0)
        outputs(i + 1, 1)
        return carry

    lax.fori_loop(0, nq // 2, pair, 0, unroll=nq // 2)


def _attn(group, q, k, v, qg, kg, i, ck_all, cv_all, cos_t, sin_t, o_prev, kc_prev, vc_prev,
          layer_name):
    prompt = group == "prompt"
    length = SEQ if prompt else DEC_SEQ
    nb = BATCH if prompt else DEC_BATCH
    off = 0 if prompt else N_PROMPT_TOK // DEC_SEQ
    kvw = N_KV_HEADS * HEAD_DIM
    n_ctx = 0 if prompt else PAST_LEN
    qb = 128
    rep = N_HEADS // N_KV_HEADS
    in_specs = _seq_specs(group, (D_MODEL, kvw, kvw))
    in_specs += [_const_spec(qg.shape), _const_spec(kg.shape)]
    args = [q, k, v, qg, kg]
    o_spec = pl.BlockSpec((length, D_MODEL), lambda b: (b + off, 0))
    o_shape = jax.ShapeDtypeStruct((N_TOK, D_MODEL), BF16)
    aliases = {}
    if prompt:
        c_spec = pl.BlockSpec((None, None, length, kvw), lambda b: (b, i, 0, 0))
        c_shape = jax.ShapeDtypeStruct((nb, N_ATTN_LAYERS, length, kvw), F32)
        out_specs = [o_spec, c_spec, c_spec]
        out_shape = [o_shape, c_shape, c_shape]
        if kc_prev is not None:
            in_specs += [pl.BlockSpec(memory_space=pl.ANY)] * 2
            args += [kc_prev, vc_prev]
            aliases = {len(args) - 2: 1, len(args) - 1: 2}
    else:
        ctx_spec = pl.BlockSpec((None, None, n_ctx, kvw), lambda b: (b, i, 0, 0))
        in_specs += [ctx_spec, ctx_spec, _const_spec(cos_t.shape), _const_spec(sin_t.shape),
                     pl.BlockSpec(memory_space=pl.ANY)]
        args += [ck_all, cv_all, cos_t, sin_t, o_prev]
        out_specs = [o_spec]
        out_shape = [o_shape]
        aliases = {len(args) - 1: 0}
    return pl.pallas_call(
        functools.partial(_attn_kernel, length=length, n_ctx=n_ctx, use_rope=not prompt,
                          has_cache_out=prompt, n_alias=len(aliases), qb=qb),
        grid=(nb,),
        in_specs=in_specs,
        out_specs=out_specs,
        out_shape=out_shape,
        scratch_shapes=[pltpu.VMEM((N_KV_HEADS, n_ctx + length, HEAD_DIM), BF16),
                        pltpu.VMEM((N_KV_HEADS, n_ctx + length, 2 * HEAD_DIM), BF16),
                        pltpu.VMEM((2, N_KV_HEADS, rep * qb, n_ctx + length), F32)],
        input_output_aliases=aliases,
        compiler_params=_cparams(1),
        name=f"attn_{group}_{layer_name}",
    )(*args)


def _rope_tables(rows):
    row_pos = jnp.repeat(jnp.arange(rows, dtype=F32), GRID_W)
    col_pos = jnp.tile(jnp.arange(GRID_W, dtype=F32), rows)
    inv_freq = jnp.power(ROPE_THETA, -jnp.arange(0, ROPE_AXIS_DIM, 2, dtype=F32) / ROPE_AXIS_DIM)
    ang_r = row_pos[:, None] * inv_freq
    ang_c = col_pos[:, None] * inv_freq
    cos_t = jnp.concatenate([jnp.cos(ang_r), jnp.cos(ang_r), jnp.cos(ang_c), jnp.cos(ang_c)], axis=1)
    sin_t = jnp.concatenate([-jnp.sin(ang_r), jnp.sin(ang_r), -jnp.sin(ang_c), jnp.sin(ang_c)], axis=1)
    return cos_t, sin_t


def _block_diag_tiles(w):
    per = MXU_DIM // LRU_BW
    n_tiles = LRU_BLOCKS // per
    w = w.reshape(2, n_tiles, per, LRU_BW, LRU_BW)
    eye = jnp.eye(per, dtype=w.dtype)
    tiles = jnp.einsum('dtpio,pq->dtpiqo', w, eye)
    return tiles.reshape(2, n_tiles, MXU_DIM, MXU_DIM).astype(BF16)


def _row(v):
    return v.reshape(1, -1)


def kernel(x_prompt, x_sample, state_ssd, state_lru, cache_k, cache_v, c, c_ctx,
           w_mod, b_mod, norm_g, w_in_ssm, ssd_conv_w, ssd_conv_b, ssd_a_log, ssd_dt_bias,
           ssd_d, ssd_norm_w, lru_conv_w, lru_conv_b, lru_wa, lru_ba, lru_wx, lru_bx,
           lru_lambda, w_out_ssm, w_in_ca, conf_dw_w, conf_dw_b, conf_ln_g, conf_ln_b,
           q_norm_g, k_norm_g, w_out_ca, ffn_w_in, ffn_conv_w, ffn_conv_b, ffn_w_out):
    x = (x_prompt.reshape(N_PROMPT_TOK, D_MODEL), x_sample.reshape(N_SAMPLE_TOK, D_MODEL))
    cvec = jnp.concatenate(
        [c_ctx[None], c, jnp.zeros((N_MOD_ROWS - 1 - DEC_BATCH, D_MODEL), F32)], axis=0)
    mods = _modulation_all(cvec, w_mod, b_mod)
    cos_t, sin_t = _rope_tables(DEC_SEQ // GRID_W)
    kvw = N_KV_HEADS * HEAD_DIM

    w_ssm_in = jnp.pad(w_in_ssm, ((0, 0), (0, 0), (0, DT_COL0))).astype(BF16)
    ssd_h0 = state_ssd.reshape(DEC_BATCH, N_SSM_LAYERS, 2, SSD_INNER, SSD_STATE)
    ck_all = cache_k.reshape(DEC_BATCH, N_ATTN_LAYERS, PAST_LEN, kvw)
    cv_all = cache_v.reshape(DEC_BATCH, N_ATTN_LAYERS, PAST_LEN, kvw)

    ssd_st = lru_st = kc = vc = None
    for layer in range(DEPTH):
        i = layer // 2
        m = mods[layer]
        name = f"l{layer}"
        g0, g1, g2, g3 = (_row(norm_g[layer, j]) for j in range(4))
        if layer % 2 == 0:
            z, xbc, xl, gl, dt, w_ffn_in, w_ffn_out = _inproj(
                x, g0, m[0], m[1], w_ssm_in, i,
                (SSD_INNER, SSD_CONV_DIM, LRU_WIDTH, LRU_WIDTH, DT_PAD), ffn_w_in, ffn_w_out, layer,
                f"inproj_ssm_{name}", rot=(SSM_O2, 2 * SSD_HEADS))
            dtb = jnp.pad(ssd_dt_bias[i].reshape(1, -1), ((0, 0), (DT_COL0, 0)))
            alog = jnp.pad(ssd_a_log[i].reshape(1, -1), ((0, 0), (DT_COL0, 0)))
            dskip = _row(jnp.repeat(ssd_d[i], SSD_HEADDIM))
            ssd_args = (ssd_conv_w[i], _row(ssd_conv_b[i]), dtb, alog, dskip, _row(ssd_norm_w[i]))
            y, ssd_st = _ssd("prompt", z, xbc, dt, *ssd_args, i, None, None, ssd_st, name)
            (y,) = _ssd("sample", z, xbc, dt, *ssd_args, i, ssd_h0, y, None, name)
            lru_args = (lru_conv_w[i], _row(lru_conv_b[i]), _block_diag_tiles(lru_wa[i]),
                        _block_diag_tiles(lru_wx[i]), lru_ba[i], lru_bx[i], lru_lambda[i])
            yl, lru_st = _lru("prompt", xl, gl, *lru_args, i, None, None, lru_st, name)
            (yl,) = _lru("sample", xl, gl, *lru_args, i, state_lru, yl, None, name)
            xo = _outproj(y, yl, w_out_ssm, i, x, g1, m[2], f"outproj_ssm_{name}")
        else:
            ga, gg, q, k, v, w_ffn_in, w_ffn_out = _inproj(
                x, g0, m[0], m[1], w_in_ca, i,
                (CONV_WIDTH, CONV_WIDTH, N_HEADS * HEAD_DIM, kvw, kvw), ffn_w_in, ffn_w_out, layer,
                f"inproj_ca_{name}")
            conv_args = (conf_dw_w[i], _row(conf_dw_b[i]), _row(conf_ln_g[i]), _row(conf_ln_b[i]))
            cvo = _confconv("prompt", ga, gg, *conv_args, None, name)
            cvo = _confconv("sample", ga, gg, *conv_args, cvo, name)
            qg, kg = _row(q_norm_g[i]), _row(k_norm_g[i])
            o, kc, vc = _attn("prompt", q, k, v, qg, kg, i, None, None, None, None, None, kc, vc,
                              name)
            (o,) = _attn("sample", q, k, v, qg, kg, i, ck_all, cv_all, cos_t, sin_t, o, None, None,
                         name)
            xo = _outproj(cvo, o, w_out_ca, i, x, g1, m[2], f"outproj_ca_{name}")
        x = (_ffn(xo, g2, m[3], m[4], w_ffn_in, ffn_conv_w[layer], _row(ffn_conv_b[layer]),
                  w_ffn_out, g3, m[5], f"ffn_{name}"),)

    xp = x[0][:N_PROMPT_TOK].reshape(BATCH, SEQ, D_MODEL)
    xs = x[0][N_PROMPT_TOK:].reshape(DEC_BATCH, DEC_SEQ, D_MODEL)
    return (xp, xs,
            ssd_st.reshape(BATCH, N_SSM_LAYERS, 2, SSD_HEADS, SSD_HEADDIM, SSD_STATE),
            lru_st,
            kc.reshape(BATCH, N_ATTN_LAYERS, SEQ, N_KV_HEADS, HEAD_DIM),
            vc.reshape(BATCH, N_ATTN_LAYERS, SEQ, N_KV_HEADS, HEAD_DIM))
```
